```python
import math
import jax
import jax.numpy as jnp
from jax import lax
import numpy as np

D_MODEL = 4096
BATCH = 2
SEQ = 8192
DEPTH = 4

CTX_LEN = 256
GRID_W = 64
W_MIX = D_MODEL // 4
W_A = W_MIX
H_A = 8
DH_A = W_A // H_A
QKV_BLOCK = 4
MLSTM_CHUNK = 64
W_B = W_MIX
S5_GROUP = 16
G_B = W_B // S5_GROUP
P_B = 64
W_C = W_MIX
HD_C = 64
H_C = W_C // HD_C
NG_C = 4
N_C = 128
SSD_CHUNK = 128
CONV_W = 5
XBC_W = W_C + 2 * NG_C * N_C
D_FF = 4 * D_MODEL
R_MOD = 512
R_GATE = 512
N_BRANCH = 3
N_MOD = 6
SPLIT_IDX = (W_A, 2 * W_A, 2 * W_A + W_B, 2 * W_A + W_B + W_C, 2 * W_A + W_B + W_C + XBC_W, 2 * W_A + W_B + W_C + XBC_W + 2 * H_C)
IN_COLS = 2 * W_A + W_B + W_C + XBC_W + 2 * H_C + R_GATE
EPS = 1e-6
F32 = jnp.float32

kernel_name = 'hybrid_mlstm_s5_ssd_flow_block'


def rmsnorm(x, g):
    xf = x.astype(F32)
    xf = xf * lax.rsqrt(jnp.mean(xf * xf, axis=-1, keepdims=True) + EPS)
    return xf.astype(x.dtype) * g


def modulate(x, g, shift, scale):
    return rmsnorm(x, g) * (1 + scale) + shift


def adaln(cvec, down, up, b):
    return (jax.nn.silu(cvec) @ down) @ up + b


def dwconv(x, w, b):
    pad = CONV_W // 2
    y = lax.conv_general_dilated(x, w[:, None, :].astype(x.dtype), window_strides=(1,), padding=[(pad, pad)],
                                 dimension_numbers=('NWC', 'WIO', 'NWC'), feature_group_count=x.shape[-1])
    return y + b


def to_colmajor(t, rows):
    bsz, n, ch = t.shape
    return t.reshape(bsz, rows, GRID_W, ch).swapaxes(1, 2).reshape(bsz, n, ch)


def from_colmajor(t, rows):
    bsz, n, ch = t.shape
    return t.reshape(bsz, GRID_W, rows, ch).swapaxes(1, 2).reshape(bsz, n, ch)


def blockdiag(x, w):
    xb = x.reshape(*x.shape[:-1], -1, QKV_BLOCK)
    return jnp.einsum('blnc,ncd->blnd', xb, w).reshape(x.shape)


def mlstm_chunked(q, k, v, log_i, log_f, state):
    bsz, nh, n, dh = q.shape
    nc = n // MLSTM_CHUNK

    def chunks(t):
        return jnp.moveaxis(t.reshape(bsz, nh, nc, MLSTM_CHUNK, *t.shape[3:]), 2, 0)

    causal = jnp.tril(jnp.ones((MLSTM_CHUNK, MLSTM_CHUNK), dtype=bool))

    def step(carry, inp):
        c_st, n_st, m_st = carry
        qc, kc, vc, ic, fc = inp
        b = jnp.cumsum(fc, axis=-1)
        d_log = jnp.where(causal, b[..., :, None] - b[..., None, :] + ic[..., None, :], -jnp.inf)
        inter = b + m_st[..., None]
        m_t = jnp.maximum(inter, jnp.max(d_log, axis=-1))
        w_carry = jnp.exp(inter - m_t)
        s = jnp.einsum('bhtd,bhsd->bhts', qc, kc) * jnp.exp(d_log - m_t[..., None])
        num = jnp.einsum('bhts,bhsd->bhtd', s, vc) + w_carry[..., None] * jnp.einsum('bhtk,bhkv->bhtv', qc, c_st)
        den = jnp.sum(s, axis=-1) + w_carry * jnp.einsum('bhtk,bhk->bht', qc, n_st)
        h = num / jnp.maximum(jnp.abs(den), jnp.exp(-m_t))[..., None]
        b_end = b[..., -1]
        log_ws = b_end[..., None] - b + ic
        m_new = jnp.maximum(b_end + m_st, jnp.max(log_ws, axis=-1))
        ws = jnp.exp(log_ws - m_new[..., None])[..., None]
        w_prev = jnp.exp(b_end + m_st - m_new)
        c_new = w_prev[..., None, None] * c_st + jnp.einsum('bhsk,bhsv->bhkv', kc * ws, vc)
        n_new = w_prev[..., None] * n_st + jnp.sum(kc * ws, axis=2)
        return (c_new, n_new, m_new), h

    state, h = lax.scan(step, state, tuple(chunks(t) for t in (q, k, v, log_i, log_f)))
    return state, jnp.moveaxis(h, 0, 2).reshape(bsz, nh, n, dh)


def mlstm_features(xm, conv_w, conv_b, wq, wk, wv, gate_w, gate_b):
    bsz, n, _ = xm.shape
    xc = jax.nn.silu(dwconv(xm, conv_w, conv_b))
    q, k, v = blockdiag(xc, wq), blockdiag(xc, wk), blockdiag(xm, wv)
    g = jnp.einsum('blc,dcj->bdjl', jnp.concatenate([q, k, v], axis=-1), gate_w).astype(F32)
    g = g + gate_b.astype(F32)[None, :, :, None]

    def heads(t):
        return t.astype(F32).reshape(bsz, n, H_A, DH_A).transpose(0, 2, 1, 3)

    return xc, heads(q) * DH_A ** -0.5, heads(k), heads(v), g[:, :, :H_A], jax.nn.log_sigmoid(g[:, :, H_A:])


def mlstm_dir(feats, d, state, reverse):
    _, q, k, v, li, lf = feats
    seq = (q, k, v, li[:, d], lf[:, d])
    if reverse:
        seq = tuple(jnp.flip(t, axis=2) for t in seq)
    state, h = mlstm_chunked(*seq, state)
    if reverse:
        h = jnp.flip(h, axis=2)
    return state, h


def mlstm_readout(h, xc, zm, norm_g, skip):
    bsz, n, _ = xc.shape
    h = h.transpose(0, 2, 1, 3)
    mu = jnp.mean(h, axis=-1, keepdims=True)
    var = jnp.mean(jnp.square(h - mu), axis=-1, keepdims=True)
    h = ((h - mu) * lax.rsqrt(var + EPS)).reshape(bsz, n, W_A).astype(xc.dtype)
    return (h * norm_g + skip * xc) * jax.nn.silu(zm)


def mlstm_mixer(xm_c, zm_c, xm_l, zm_l, conv_w, conv_b, wq, wk, wv, gate_w, gate_b, norm_g, skip):
    fc = mlstm_features(xm_c, conv_w, conv_b, wq, wk, wv, gate_w, gate_b)
    fl = mlstm_features(xm_l, conv_w, conv_b, wq, wk, wv, gate_w, gate_b)
    bsz = xm_c.shape[0]
    hs_c, hs_l = [], []
    for d in range(2):
        zero = (jnp.zeros((bsz, H_A, DH_A, DH_A), F32), jnp.zeros((bsz, H_A, DH_A), F32), jnp.zeros((bsz, H_A), F32))
        st, hc = mlstm_dir(fc, d, zero, d == 1)
        _, hl = mlstm_dir(fl, d, st, d == 1)
        hs_c.append(hc)
        hs_l.append(hl)
    y_c = mlstm_readout(hs_c[0] + hs_c[1], fc[0], zm_c, norm_g, skip)
    y_l = mlstm_readout(hs_l[0] + hs_l[1], fl[0], zm_l, norm_g, skip)
    return y_c, y_l


def s5_scan(u, lam_bar, b_bar, h0, reverse):
    bu = jnp.einsum('gpc,blgc->blgp', b_bar, u.astype(jnp.complex64))
    edge = -1 if reverse else 0
    bu = bu.at[:, edge].add(lam_bar * h0)
    a = jnp.broadcast_to(lam_bar, bu.shape)

    def combine(e1, e2):
        return e2[0] * e1[0], e2[0] * e1[1] + e2[1]

    _, h = lax.associative_scan(combine, (a, bu), reverse=reverse, axis=1)
    return h


def s5_readout(c_mat, h):
    bsz, n = h.shape[:2]
    return jnp.real(jnp.einsum('gcp,blgp->blgc', c_mat, h)).reshape(bsz, n, W_B)


def s5_glu(y, glu_w, glu_b, dtype):
    y = jax.nn.gelu(y)
    return (y * jax.nn.sigmoid(y @ glu_w.astype(F32) + glu_b.astype(F32))).astype(dtype)


def s5_mixer(u_c, u_l, lam_re, lam_im, log_dt, b_re, b_im, c_re, c_im, d_skip, glu_w, glu_b):
    def groups(u):
        return u.astype(F32).reshape(u.shape[0], u.shape[1], G_B, S5_GROUP)

    gc, gl = groups(u_c), groups(u_l)
    y_c = d_skip.astype(F32) * u_c.astype(F32)
    y_l = d_skip.astype(F32) * u_l.astype(F32)
    for d in range(2):
        lam = lax.complex(lam_re[d].astype(F32), lam_im[d].astype(F32))
        lam_bar = jnp.exp(lam * jnp.exp(log_dt[d].astype(F32))[:, None])
        b_bar = ((lam_bar - 1) / lam)[..., None] * lax.complex(b_re[d].astype(F32), b_im[d].astype(F32))
        c_mat = lax.complex(c_re[d].astype(F32), c_im[d].astype(F32))
        rev = d == 1
        h_c = s5_scan(gc, lam_bar, b_bar, jnp.zeros((gc.shape[0], G_B, P_B), jnp.complex64), rev)
        h_l = s5_scan(gl, lam_bar, b_bar, h_c[:, 0] if rev else h_c[:, -1], rev)
        y_c = y_c + s5_readout(c_mat, h_c)
        y_l = y_l + s5_readout(c_mat, h_l)
    return s5_glu(y_c, glu_w, glu_b, u_c.dtype), s5_glu(y_l, glu_w, glu_b, u_l.dtype)


def segsum(a):
    t_len = a.shape[-1]
    cs = jnp.cumsum(a, axis=-1)
    mask = jnp.tril(jnp.ones((t_len, t_len), dtype=bool))
    return jnp.where(mask, cs[..., :, None] - cs[..., None, :], -jnp.inf)


def ssd_chunked(xv, a, bh, ch, h0):
    bsz, n, nh, hp = xv.shape
    nc = n // SSD_CHUNK
    xv = xv.reshape(bsz, nc, SSD_CHUNK, nh, hp)
    bh = bh.reshape(bsz, nc, SSD_CHUNK, nh, N_C)
    ch = ch.reshape(bsz, nc, SSD_CHUNK, nh, N_C)
    a = a.reshape(bsz, nc, SSD_CHUNK, nh).transpose(0, 3, 1, 2)
    a_cs = jnp.cumsum(a, axis=-1)
    scores = jnp.einsum('bclhn,bcshn->bhcls', ch, bh) * jnp.exp(segsum(a))
    y_diag = jnp.einsum('bhcls,bcshp->bclhp', scores, xv)
    decay_states = jnp.exp(a_cs[..., -1:] - a_cs).transpose(0, 2, 3, 1)
    states = jnp.einsum('bclhn,bclhp->bchpn', bh * decay_states[..., None], xv)
    states = jnp.concatenate([h0[:, None], states], axis=1)
    decay_chunk = jnp.exp(segsum(jnp.pad(a_cs[..., -1], ((0, 0), (0, 0), (1, 0)))))
    new_states = jnp.einsum('bhzc,bchpn->bzhpn', decay_chunk, states)
    states, final = new_states[:, :-1], new_states[:, -1]
    y_off = jnp.einsum('bclhn,bchpn->bclhp', ch, states) * jnp.exp(a_cs).transpose(0, 2, 3, 1)[..., None]
    return (y_diag + y_off).reshape(bsz, n, nh, hp), final


def ssd_dir(xbc, dt_raw, a_log, dt_bias, h0, reverse):
    if reverse:
        xbc, dt_raw = jnp.flip(xbc, axis=1), jnp.flip(dt_raw, axis=1)
    bsz, n, _ = xbc.shape
    xs, bm, cm = jnp.split(xbc.astype(F32), (W_C, W_C + NG_C * N_C), axis=-1)
    rep = H_C // NG_C
    bh = jnp.repeat(bm.reshape(bsz, n, NG_C, N_C), rep, axis=2)
    ch = jnp.repeat(cm.reshape(bsz, n, NG_C, N_C), rep, axis=2)
    dt = jax.nn.softplus(dt_raw.astype(F32) + dt_bias.astype(F32))
    a = -jnp.exp(a_log.astype(F32))
    y, final = ssd_chunked(xs.reshape(bsz, n, H_C, HD_C) * dt[..., None], a * dt, bh, ch, h0)
    if reverse:
        y = jnp.flip(y, axis=1)
    return y, final


def ssd_mixer(zs_c, xbc_c, dt_c, zs_l, xbc_l, dt_l, rows, conv_w, conv_b, a_log, dt_bias, d_skip, norm_g):
    xbc_c = jax.nn.silu(dwconv(xbc_c, conv_w, conv_b))
    xbc_l = jax.nn.silu(dwconv(to_colmajor(xbc_l, rows), conv_w, conv_b))
    dt_l = to_colmajor(dt_l, rows)
    bsz = xbc_c.shape[0]

    def skip(xbc):
        return xbc[..., :W_C].astype(F32).reshape(bsz, xbc.shape[1], H_C, HD_C) * d_skip.astype(F32)[:, None]

    y_c, y_l = skip(xbc_c), skip(xbc_l)
    for d in range(2):
        rev = d == 1
        h0 = jnp.zeros((bsz, H_C, HD_C, N_C), F32)
        yc, st = ssd_dir(xbc_c, dt_c[..., d * H_C:(d + 1) * H_C], a_log[d], dt_bias[d], h0, rev)
        yl, _ = ssd_dir(xbc_l, dt_l[..., d * H_C:(d + 1) * H_C], a_log[d], dt_bias[d], st, rev)
        y_c = y_c + yc
        y_l = y_l + yl
    y_c = y_c.reshape(bsz, -1, W_C)
    y_l = from_colmajor(y_l.reshape(bsz, -1, W_C), rows)
    out_c = rmsnorm(y_c.astype(zs_c.dtype) * jax.nn.silu(zs_c), norm_g)
    out_l = rmsnorm(y_l.astype(zs_l.dtype) * jax.nn.silu(zs_l), norm_g)
    return out_c, out_l


def merge_branches(ys, gd, gate_w, gate_b, branch_w, w_out):
    terms = [jax.nn.sigmoid(gd @ gate_w[i] + gate_b[i]) * (ys[i].astype(gd.dtype) @ branch_w[i]) for i in range(N_BRANCH)]
    return (terms[0] + terms[1] + terms[2]) @ w_out


def sq_relu_mlp(h, w1, w2):
    return jnp.square(jax.nn.relu(h @ w1)) @ w2


def setup_inputs(seed: int = 0) -> dict:
    key = jax.random.key(seed)
    ks = iter(jax.random.split(key, 64))

    def nrm(shape, scale=1.0):
        return scale * jax.random.normal(next(ks), shape, F32)

    def unif(shape, lo, hi):
        return jax.random.uniform(next(ks), shape, F32, lo, hi)

    def gain(shape):
        return 1.0 + nrm(shape, 0.02)

    dt_ssd = jnp.exp(unif((DEPTH, 2, H_C), math.log(1e-3), math.log(1e-1)))
    return {
        'x': nrm((BATCH, SEQ, D_MODEL)),
        'c': nrm((BATCH, D_MODEL)),
        'ctx': nrm((BATCH, CTX_LEN, D_MODEL)),
        'c_ctx': nrm((D_MODEL,)),
        'norm1_g': gain((DEPTH, D_MODEL)),
        'norm2_g': gain((DEPTH, D_MODEL)),
        'final_g': gain((D_MODEL,)),
        'mod_down': nrm((DEPTH, D_MODEL, R_MOD), D_MODEL ** -0.5),
        'mod_up': nrm((DEPTH, R_MOD, N_MOD * D_MODEL), 0.5 * R_MOD ** -0.5),
        'mod_b': nrm((DEPTH, N_MOD * D_MODEL), 0.01),
        'w_in': nrm((DEPTH, D_MODEL, IN_COLS), D_MODEL ** -0.5),
        'm_conv_w': nrm((DEPTH, CONV_W, W_A), CONV_W ** -0.5),
        'm_conv_b': nrm((DEPTH, W_A), 0.01),
        'm_wq': nrm((DEPTH, W_A // QKV_BLOCK, QKV_BLOCK, QKV_BLOCK), QKV_BLOCK ** -0.5),
        'm_wk': nrm((DEPTH, W_A // QKV_BLOCK, QKV_BLOCK, QKV_BLOCK), QKV_BLOCK ** -0.5),
        'm_wv': nrm((DEPTH, W_A // QKV_BLOCK, QKV_BLOCK, QKV_BLOCK), QKV_BLOCK ** -0.5),
        'm_gate_w': nrm((DEPTH, 2, 3 * W_A, 2 * H_A), 0.1 * (3 * W_A) ** -0.5),
        'm_gate_b': jnp.concatenate([nrm((DEPTH, 2, H_A), 0.1), jnp.linspace(3.0, 6.0, H_A, dtype=F32) + nrm((DEPTH, 2, H_A), 0.1)], axis=-1),
        'm_norm_g': gain((DEPTH, W_A)),
        'm_skip': gain((DEPTH, W_A)),
        's5_lam_re': -0.5 + nrm((DEPTH, 2, G_B, P_B), 0.01),
        's5_lam_im': jnp.pi * jnp.arange(P_B, dtype=F32) + nrm((DEPTH, 2, G_B, P_B), 0.01),
        's5_log_dt': unif((DEPTH, 2, G_B), math.log(1e-3), math.log(1e-1)),
        's5_b_re': nrm((DEPTH, 2, G_B, P_B, S5_GROUP), (2 * S5_GROUP) ** -0.5),
        's5_b_im': nrm((DEPTH, 2, G_B, P_B, S5_GROUP), (2 * S5_GROUP) ** -0.5),
        's5_c_re': nrm((DEPTH, 2, G_B, S5_GROUP, P_B), (2 * P_B) ** -0.5),
        's5_c_im': nrm((DEPTH, 2, G_B, S5_GROUP, P_B), (2 * P_B) ** -0.5),
        's5_d': nrm((DEPTH, W_B)),
        's5_glu_w': nrm((DEPTH, W_B, W_B), W_B ** -0.5),
        's5_glu_b': nrm((DEPTH, W_B), 0.01),
        'ssd_conv_w': nrm((DEPTH, CONV_W, XBC_W), CONV_W ** -0.5),
        'ssd_conv_b': nrm((DEPTH, XBC_W), 0.01),
        'ssd_a_log': jnp.log(unif((DEPTH, 2, H_C), 1.0, 16.0)),
        'ssd_dt_bias': dt_ssd + jnp.log(-jnp.expm1(-dt_ssd)),
        'ssd_d': 1.0 + nrm((DEPTH, H_C), 0.1),
        'ssd_norm_g': gain((DEPTH, W_C)),
        'gate_w': nrm((DEPTH, N_BRANCH, R_GATE, D_MODEL), R_GATE ** -0.5),
        'gate_b': nrm((DEPTH, N_BRANCH, D_MODEL), 0.01),
        'branch_w': nrm((DEPTH, N_BRANCH, W_MIX, D_MODEL), W_MIX ** -0.5),
        'w_out': nrm((DEPTH, D_MODEL, D_MODEL), D_MODEL ** -0.5),
        'mlp_w1': nrm((DEPTH, D_MODEL, D_FF), D_MODEL ** -0.5),
        'mlp_w2': nrm((DEPTH, D_FF, D_MODEL), D_FF ** -0.5),
    }


def reference(x, c, ctx, c_ctx, norm1_g, norm2_g, final_g, mod_down, mod_up, mod_b, w_in,
              m_conv_w, m_conv_b, m_wq, m_wk, m_wv, m_gate_w, m_gate_b, m_norm_g, m_skip,
              s5_lam_re, s5_lam_im, s5_log_dt, s5_b_re, s5_b_im, s5_c_re, s5_c_im, s5_d, s5_glu_w, s5_glu_b,
              ssd_conv_w, ssd_conv_b, ssd_a_log, ssd_dt_bias, ssd_d, ssd_norm_g,
              gate_w, gate_b, branch_w, w_out, mlp_w1, mlp_w2):
    rows = x.shape[1] // GRID_W
    x_lat, x_ctx = x, ctx
    for l in range(DEPTH):
        last = l == DEPTH - 1
        sh1_l, sc1_l, g1_l, sh2_l, sc2_l, g2_l = jnp.split(adaln(c, mod_down[l], mod_up[l], mod_b[l])[:, None, :], N_MOD, axis=-1)
        sh1_c, sc1_c, g1_c, sh2_c, sc2_c, g2_c = jnp.split(adaln(c_ctx, mod_down[l], mod_up[l], mod_b[l]), N_MOD, axis=-1)
        p_l = modulate(x_lat, norm1_g[l], sh1_l, sc1_l) @ w_in[l]
        p_c = modulate(x_ctx, norm1_g[l], sh1_c, sc1_c) @ w_in[l]
        xm_l, zm_l, u_l, zs_l, xbc_l, dt_l, gd_l = jnp.split(p_l, SPLIT_IDX, axis=-1)
        xm_c, zm_c, u_c, zs_c, xbc_c, dt_c, gd_c = jnp.split(p_c, SPLIT_IDX, axis=-1)
        ya_c, ya_l = mlstm_mixer(xm_c, zm_c, xm_l, zm_l, m_conv_w[l], m_conv_b[l], m_wq[l], m_wk[l], m_wv[l],
                                 m_gate_w[l], m_gate_b[l], m_norm_g[l], m_skip[l])
        yb_c, yb_l = s5_mixer(u_c, u_l, s5_lam_re[l], s5_lam_im[l], s5_log_dt[l], s5_b_re[l], s5_b_im[l],
                              s5_c_re[l], s5_c_im[l], s5_d[l], s5_glu_w[l], s5_glu_b[l])
        yc_c, yc_l = ssd_mixer(zs_c, xbc_c, dt_c, zs_l, xbc_l, dt_l, rows, ssd_conv_w[l], ssd_conv_b[l],
                               ssd_a_log[l], ssd_dt_bias[l], ssd_d[l], ssd_norm_g[l])
        x_lat = x_lat + g1_l * merge_branches((ya_l, yb_l, yc_l), gd_l, gate_w[l], gate_b[l], branch_w[l], w_out[l])
        x_lat = x_lat + g2_l * sq_relu_mlp(modulate(x_lat, norm2_g[l], sh2_l, sc2_l), mlp_w1[l], mlp_w2[l])
        if not last:
            x_ctx = x_ctx + g1_c * merge_branches((ya_c, yb_c, yc_c), gd_c, gate_w[l], gate_b[l], branch_w[l], w_out[l])
            x_ctx = x_ctx + g2_c * sq_relu_mlp(modulate(x_ctx, norm2_g[l], sh2_c, sc2_c), mlp_w1[l], mlp_w2[l])
    return rmsnorm(x_lat, final_g)
```

```python
import functools
import math
import jax
import jax.numpy as jnp
from jax import lax
import numpy as np
from jax.experimental import pallas as pl
from jax.experimental.pallas import tpu as pltpu

D_MODEL = 4096
BATCH = 2
SEQ = 8192
DEPTH = 4

CTX_LEN = 256
GRID_W = 64
W_MIX = D_MODEL // 4
W_A = W_MIX
H_A = 8
DH_A = W_A // H_A
QKV_BLOCK = 4
MLSTM_CHUNK = 64
W_B = W_MIX
S5_GROUP = 16
G_B = W_B // S5_GROUP
P_B = 64
W_C = W_MIX
HD_C = 64
H_C = W_C // HD_C
NG_C = 4
N_C = 128
SSD_CHUNK = 128
CONV_W = 5
XBC_W = W_C + 2 * NG_C * N_C
D_FF = 4 * D_MODEL
R_MOD = 512
R_GATE = 512
N_BRANCH = 3
N_MOD = 6
SPLIT_IDX = (W_A, 2 * W_A, 2 * W_A + W_B, 2 * W_A + W_B + W_C, 2 * W_A + W_B + W_C + XBC_W, 2 * W_A + W_B + W_C + XBC_W + 2 * H_C)
IN_COLS = 2 * W_A + W_B + W_C + XBC_W + 2 * H_C + R_GATE
EPS = 1e-6
F32 = jnp.float32
BF16 = jnp.bfloat16

V7X_VMEM_LIMIT_BYTES = 56 * 1024 * 1024
MOD_ROWS = 8


def rmsnorm(x, g):
    xf = x.astype(F32)
    xf = xf * lax.rsqrt(jnp.mean(xf * xf, axis=-1, keepdims=True) + EPS)
    return xf.astype(x.dtype) * g


def modulate(x, g, shift, scale):
    return rmsnorm(x, g) * (1 + scale) + shift


def adaln(cvec, down, up, b):
    return (jax.nn.silu(cvec) @ down) @ up + b


def dwconv(x, w, b):
    pad = CONV_W // 2
    y = lax.conv_general_dilated(x, w[:, None, :].astype(x.dtype), window_strides=(1,), padding=[(pad, pad)],
                                 dimension_numbers=('NWC', 'WIO', 'NWC'), feature_group_count=x.shape[-1])
    return y + b


def to_colmajor(t, rows):
    bsz, n, ch = t.shape
    return t.reshape(bsz, rows, GRID_W, ch).swapaxes(1, 2).reshape(bsz, n, ch)


def from_colmajor(t, rows):
    bsz, n, ch = t.shape
    return t.reshape(bsz, GRID_W, rows, ch).swapaxes(1, 2).reshape(bsz, n, ch)


def blockdiag(x, w):
    xb = x.reshape(*x.shape[:-1], -1, QKV_BLOCK)
    return jnp.einsum('blnc,ncd->blnd', xb, w).reshape(x.shape)


def mlstm_chunked(q, k, v, log_i, log_f, state):
    bsz, nh, n, dh = q.shape
    nc = n // MLSTM_CHUNK

    def chunks(t):
        return jnp.moveaxis(t.reshape(bsz, nh, nc, MLSTM_CHUNK, *t.shape[3:]), 2, 0)

    causal = jnp.tril(jnp.ones((MLSTM_CHUNK, MLSTM_CHUNK), dtype=bool))

    def step(carry, inp):
        c_st, n_st, m_st = carry
        qc, kc, vc, ic, fc = inp
        b = jnp.cumsum(fc, axis=-1)
        d_log = jnp.where(causal, b[..., :, None] - b[..., None, :] + ic[..., None, :], -jnp.inf)
        inter = b + m_st[..., None]
        m_t = jnp.maximum(inter, jnp.max(d_log, axis=-1))
        w_carry = jnp.exp(inter - m_t)
        s = jnp.einsum('bhtd,bhsd->bhts', qc, kc) * jnp.exp(d_log - m_t[..., None])
        num = jnp.einsum('bhts,bhsd->bhtd', s, vc) + w_carry[..., None] * jnp.einsum('bhtk,bhkv->bhtv', qc, c_st)
        den = jnp.sum(s, axis=-1) + w_carry * jnp.einsum('bhtk,bhk->bht', qc, n_st)
        h = num / jnp.maximum(jnp.abs(den), jnp.exp(-m_t))[..., None]
        b_end = b[..., -1]
        log_ws = b_end[..., None] - b + ic
        m_new = jnp.maximum(b_end + m_st, jnp.max(log_ws, axis=-1))
        ws = jnp.exp(log_ws - m_new[..., None])[..., None]
        w_prev = jnp.exp(b_end + m_st - m_new)
        c_new = w_prev[..., None, None] * c_st + jnp.einsum('bhsk,bhsv->bhkv', kc * ws, vc)
        n_new = w_prev[..., None] * n_st + jnp.sum(kc * ws, axis=2)
        return (c_new, n_new, m_new), h

    state, h = lax.scan(step, state, tuple(chunks(t) for t in (q, k, v, log_i, log_f)))
    return state, jnp.moveaxis(h, 0, 2).reshape(bsz, nh, n, dh)


def mlstm_features(xm, conv_w, conv_b, wq, wk, wv, gate_w, gate_b):
    bsz, n, _ = xm.shape
    xc = jax.nn.silu(dwconv(xm, conv_w, conv_b))
    q, k, v = blockdiag(xc, wq), blockdiag(xc, wk), blockdiag(xm, wv)
    g = jnp.einsum('blc,dcj->bdjl', jnp.concatenate([q, k, v], axis=-1), gate_w).astype(F32)
    g = g + gate_b.astype(F32)[None, :, :, None]

    def heads(t):
        return t.astype(F32).reshape(bsz, n, H_A, DH_A).transpose(0, 2, 1, 3)

    return xc, heads(q) * DH_A ** -0.5, heads(k), heads(v), g[:, :, :H_A], jax.nn.log_sigmoid(g[:, :, H_A:])


def mlstm_dir(feats, d, state, reverse):
    _, q, k, v, li, lf = feats
    seq = (q, k, v, li[:, d], lf[:, d])
    if reverse:
        seq = tuple(jnp.flip(t, axis=2) for t in seq)
    state, h = mlstm_chunked(*seq, state)
    if reverse:
        h = jnp.flip(h, axis=2)
    return state, h


def mlstm_readout(h, xc, zm, norm_g, skip):
    bsz, n, _ = xc.shape
    h = h.transpose(0, 2, 1, 3)
    mu = jnp.mean(h, axis=-1, keepdims=True)
    var = jnp.mean(jnp.square(h - mu), axis=-1, keepdims=True)
    h = ((h - mu) * lax.rsqrt(var + EPS)).reshape(bsz, n, W_A).astype(xc.dtype)
    return (h * norm_g + skip * xc) * jax.nn.silu(zm)


def mlstm_mixer(xm_c, zm_c, xm_l, zm_l, conv_w, conv_b, wq, wk, wv, gate_w, gate_b, norm_g, skip):
    fc = mlstm_features(xm_c, conv_w, conv_b, wq, wk, wv, gate_w, gate_b)
    fl = mlstm_features(xm_l, conv_w, conv_b, wq, wk, wv, gate_w, gate_b)
    bsz = xm_c.shape[0]
    hs_c, hs_l = [], []
    for d in range(2):
        zero = (jnp.zeros((bsz, H_A, DH_A, DH_A), F32), jnp.zeros((bsz, H_A, DH_A), F32), jnp.zeros((bsz, H_A), F32))
        st, hc = mlstm_dir(fc, d, zero, d == 1)
        _, hl = mlstm_dir(fl, d, st, d == 1)
        hs_c.append(hc)
        hs_l.append(hl)
    y_c = mlstm_readout(hs_c[0] + hs_c[1], fc[0], zm_c, norm_g, skip)
    y_l = mlstm_readout(hs_l[0] + hs_l[1], fl[0], zm_l, norm_g, skip)
    return y_c, y_l


def s5_scan(u, lam_bar, b_bar, h0, reverse):
    bu = jnp.einsum('gpc,blgc->blgp', b_bar, u.astype(jnp.complex64))
    edge = -1 if reverse else 0
    bu = bu.at[:, edge].add(lam_bar * h0)
    a = jnp.broadcast_to(lam_bar, bu.shape)

    def combine(e1, e2):
        return e2[0] * e1[0], e2[0] * e1[1] + e2[1]

    _, h = lax.associative_scan(combine, (a, bu), reverse=reverse, axis=1)
    return h


def s5_readout(c_mat, h):
    bsz, n = h.shape[:2]
    return jnp.real(jnp.einsum('gcp,blgp->blgc', c_mat, h)).reshape(bsz, n, W_B)


def s5_glu(y, glu_w, glu_b, dtype):
    y = jax.nn.gelu(y)
    return (y * jax.nn.sigmoid(y @ glu_w.astype(F32) + glu_b.astype(F32))).astype(dtype)


def s5_mixer(u_c, u_l, lam_re, lam_im, log_dt, b_re, b_im, c_re, c_im, d_skip, glu_w, glu_b):
    def groups(u):
        return u.astype(F32).reshape(u.shape[0], u.shape[1], G_B, S5_GROUP)

    gc, gl = groups(u_c), groups(u_l)
    y_c = d_skip.astype(F32) * u_c.astype(F32)
    y_l = d_skip.astype(F32) * u_l.astype(F32)
    for d in range(2):
        lam = lax.complex(lam_re[d].astype(F32), lam_im[d].astype(F32))
        lam_bar = jnp.exp(lam * jnp.exp(log_dt[d].astype(F32))[:, None])
        b_bar = ((lam_bar - 1) / lam)[..., None] * lax.complex(b_re[d].astype(F32), b_im[d].astype(F32))
        c_mat = lax.complex(c_re[d].astype(F32), c_im[d].astype(F32))
        rev = d == 1
        h_c = s5_scan(gc, lam_bar, b_bar, jnp.zeros((gc.shape[0], G_B, P_B), jnp.complex64), rev)
        h_l = s5_scan(gl, lam_bar, b_bar, h_c[:, 0] if rev else h_c[:, -1], rev)
        y_c = y_c + s5_readout(c_mat, h_c)
        y_l = y_l + s5_readout(c_mat, h_l)
    return s5_glu(y_c, glu_w, glu_b, u_c.dtype), s5_glu(y_l, glu_w, glu_b, u_l.dtype)


def segsum(a):
    t_len = a.shape[-1]
    cs = jnp.cumsum(a, axis=-1)
    mask = jnp.tril(jnp.ones((t_len, t_len), dtype=bool))
    return jnp.where(mask, cs[..., :, None] - cs[..., None, :], -jnp.inf)


def ssd_chunked(xv, a, bh, ch, h0):
    bsz, n, nh, hp = xv.shape
    nc = n // SSD_CHUNK
    xv = xv.reshape(bsz, nc, SSD_CHUNK, nh, hp)
    bh = bh.reshape(bsz, nc, SSD_CHUNK, nh, N_C)
    ch = ch.reshape(bsz, nc, SSD_CHUNK, nh, N_C)
    a = a.reshape(bsz, nc, SSD_CHUNK, nh).transpose(0, 3, 1, 2)
    a_cs = jnp.cumsum(a, axis=-1)
    scores = jnp.einsum('bclhn,bcshn->bhcls', ch, bh) * jnp.exp(segsum(a))
    y_diag = jnp.einsum('bhcls,bcshp->bclhp', scores, xv)
    decay_states = jnp.exp(a_cs[..., -1:] - a_cs).transpose(0, 2, 3, 1)
    states = jnp.einsum('bclhn,bclhp->bchpn', bh * decay_states[..., None], xv)
    states = jnp.concatenate([h0[:, None], states], axis=1)
    decay_chunk = jnp.exp(segsum(jnp.pad(a_cs[..., -1], ((0, 0), (0, 0), (1, 0)))))
    new_states = jnp.einsum('bhzc,bchpn->bzhpn', decay_chunk, states)
    states, final = new_states[:, :-1], new_states[:, -1]
    y_off = jnp.einsum('bclhn,bchpn->bclhp', ch, states) * jnp.exp(a_cs).transpose(0, 2, 3, 1)[..., None]
    return (y_diag + y_off).reshape(bsz, n, nh, hp), final


def ssd_dir(xbc, dt_raw, a_log, dt_bias, h0, reverse):
    if reverse:
        xbc, dt_raw = jnp.flip(xbc, axis=1), jnp.flip(dt_raw, axis=1)
    bsz, n, _ = xbc.shape
    xs, bm, cm = jnp.split(xbc.astype(F32), (W_C, W_C + NG_C * N_C), axis=-1)
    rep = H_C // NG_C
    bh = jnp.repeat(bm.reshape(bsz, n, NG_C, N_C), rep, axis=2)
    ch = jnp.repeat(cm.reshape(bsz, n, NG_C, N_C), rep, axis=2)
    dt = jax.nn.softplus(dt_raw.astype(F32) + dt_bias.astype(F32))
    a = -jnp.exp(a_log.astype(F32))
    y, final = ssd_chunked(xs.reshape(bsz, n, H_C, HD_C) * dt[..., None], a * dt, bh, ch, h0)
    if reverse:
        y = jnp.flip(y, axis=1)
    return y, final


def ssd_mixer(zs_c, xbc_c, dt_c, zs_l, xbc_l, dt_l, rows, conv_w, conv_b, a_log, dt_bias, d_skip, norm_g):
    xbc_c = jax.nn.silu(dwconv(xbc_c, conv_w, conv_b))
    xbc_l = jax.nn.silu(dwconv(to_colmajor(xbc_l, rows), conv_w, conv_b))
    dt_l = to_colmajor(dt_l, rows)
    bsz = xbc_c.shape[0]

    def skip(xbc):
        return xbc[..., :W_C].astype(F32).reshape(bsz, xbc.shape[1], H_C, HD_C) * d_skip.astype(F32)[:, None]

    y_c, y_l = skip(xbc_c), skip(xbc_l)
    for d in range(2):
        rev = d == 1
        h0 = jnp.zeros((bsz, H_C, HD_C, N_C), F32)
        yc, st = ssd_dir(xbc_c, dt_c[..., d * H_C:(d + 1) * H_C], a_log[d], dt_bias[d], h0, rev)
        yl, _ = ssd_dir(xbc_l, dt_l[..., d * H_C:(d + 1) * H_C], a_log[d], dt_bias[d], st, rev)
        y_c = y_c + yc
        y_l = y_l + yl
    y_c = y_c.reshape(bsz, -1, W_C)
    y_l = from_colmajor(y_l.reshape(bsz, -1, W_C), rows)
    out_c = rmsnorm(y_c.astype(zs_c.dtype) * jax.nn.silu(zs_c), norm_g)
    out_l = rmsnorm(y_l.astype(zs_l.dtype) * jax.nn.silu(zs_l), norm_g)
    return out_c, out_l


def merge_branches(ys, gd, gate_w, gate_b, branch_w, w_out):
    terms = [jax.nn.sigmoid(gd @ gate_w[i] + gate_b[i]) * (ys[i].astype(gd.dtype) @ branch_w[i]) for i in range(N_BRANCH)]
    return (terms[0] + terms[1] + terms[2]) @ w_out


def _mlp_kernel(x_ref, g_ref, mod_ref, w1_ref, w2_ref, o_ref, h_ref):
    f = pl.program_id(1)

    @pl.when(f == 0)
    def _():
        x = x_ref[...]
        xn = x * lax.rsqrt(jnp.mean(x * x, axis=-1, keepdims=True) + EPS) * g_ref[...]
        h = xn * (1.0 + mod_ref[0, 4:5, :]) + mod_ref[0, 3:4, :]
        h_ref[...] = h.astype(BF16)
        o_ref[...] = jnp.zeros_like(o_ref)

    a = jnp.dot(h_ref[...], w1_ref[...], preferred_element_type=F32)
    a = jnp.square(jnp.maximum(a, 0.0)).astype(BF16)
    o_ref[...] += jnp.dot(a, w2_ref[...], preferred_element_type=F32)

    @pl.when(f == pl.num_programs(1) - 1)
    def _():
        o_ref[...] = x_ref[...] + mod_ref[0, 5:6, :] * o_ref[...]


def mlp_block(x, g, mod, w1, w2, rows_per_mod, tm=512, tf=512):
    m, d = x.shape
    dff = w1.shape[1]
    tm = min(tm, m)
    assert m % tm == 0 and rows_per_mod % tm == 0 and dff % tf == 0
    tiles_per_mod = rows_per_mod // tm
    single = pl.Buffered(1)
    return pl.pallas_call(
        _mlp_kernel,
        grid=(m // tm, dff // tf),
        in_specs=[
            pl.BlockSpec((tm, d), lambda i, f: (i, 0), pipeline_mode=single),
            pl.BlockSpec((1, d), lambda i, f: (0, 0), pipeline_mode=single),
            pl.BlockSpec((1, MOD_ROWS, d), lambda i, f: (i // tiles_per_mod, 0, 0), pipeline_mode=single),
            pl.BlockSpec((d, tf), lambda i, f: (0, f)),
            pl.BlockSpec((tf, d), lambda i, f: (f, 0)),
        ],
        out_specs=pl.BlockSpec((tm, d), lambda i, f: (i, 0), pipeline_mode=single),
        out_shape=jax.ShapeDtypeStruct((m, d), F32),
        scratch_shapes=[pltpu.VMEM((tm, d), BF16)],
        compiler_params=pltpu.CompilerParams(
            dimension_semantics=("arbitrary", "arbitrary"),
            vmem_limit_bytes=V7X_VMEM_LIMIT_BYTES),
    )(x, g.reshape(1, d), mod, w1, w2)


def _pad_mod(mod):
    nb = mod.shape[0]
    mod = mod.reshape(nb, N_MOD, D_MODEL)
    return jnp.pad(mod, ((0, 0), (0, MOD_ROWS - N_MOD), (0, 0)))


def kernel(x, c, ctx, c_ctx, norm1_g, norm2_g, final_g, mod_down, mod_up, mod_b, w_in,
           m_conv_w, m_conv_b, m_wq, m_wk, m_wv, m_gate_w, m_gate_b, m_norm_g, m_skip,
           s5_lam_re, s5_lam_im, s5_log_dt, s5_b_re, s5_b_im, s5_c_re, s5_c_im, s5_d, s5_glu_w, s5_glu_b,
           ssd_conv_w, ssd_conv_b, ssd_a_log, ssd_dt_bias, ssd_d, ssd_norm_g,
           gate_w, gate_b, branch_w, w_out, mlp_w1, mlp_w2):
    rows = x.shape[1] // GRID_W
    bsz = x.shape[0]
    x_lat, x_ctx = x, ctx
    w1_bf = mlp_w1.astype(BF16)
    w2_bf = mlp_w2.astype(BF16)
    for l in range(DEPTH):
        last = l == DEPTH - 1
        mod_l = adaln(c, mod_down[l], mod_up[l], mod_b[l])
        mod_c = adaln(c_ctx, mod_down[l], mod_up[l], mod_b[l])
        sh1_l, sc1_l, g1_l, sh2_l, sc2_l, g2_l = jnp.split(mod_l[:, None, :], N_MOD, axis=-1)
        sh1_c, sc1_c, g1_c, sh2_c, sc2_c, g2_c = jnp.split(mod_c, N_MOD, axis=-1)
        p_l = modulate(x_lat, norm1_g[l], sh1_l, sc1_l) @ w_in[l]
        p_c = modulate(x_ctx, norm1_g[l], sh1_c, sc1_c) @ w_in[l]
        xm_l, zm_l, u_l, zs_l, xbc_l, dt_l, gd_l = jnp.split(p_l, SPLIT_IDX, axis=-1)
        xm_c, zm_c, u_c, zs_c, xbc_c, dt_c, gd_c = jnp.split(p_c, SPLIT_IDX, axis=-1)
        ya_c, ya_l = mlstm_mixer(xm_c, zm_c, xm_l, zm_l, m_conv_w[l], m_conv_b[l], m_wq[l], m_wk[l], m_wv[l],
                                 m_gate_w[l], m_gate_b[l], m_norm_g[l], m_skip[l])
        yb_c, yb_l = s5_mixer(u_c, u_l, s5_lam_re[l], s5_lam_im[l], s5_log_dt[l], s5_b_re[l], s5_b_im[l],
                              s5_c_re[l], s5_c_im[l], s5_d[l], s5_glu_w[l], s5_glu_b[l])
        yc_c, yc_l = ssd_mixer(zs_c, xbc_c, dt_c, zs_l, xbc_l, dt_l, rows, ssd_conv_w[l], ssd_conv_b[l],
                               ssd_a_log[l], ssd_dt_bias[l], ssd_d[l], ssd_norm_g[l])
        x_lat = x_lat + g1_l * merge_branches((ya_l, yb_l, yc_l), gd_l, gate_w[l], gate_b[l], branch_w[l], w_out[l])
        x_lat = mlp_block(x_lat.reshape(bsz * SEQ, D_MODEL), norm2_g[l], _pad_mod(mod_l), w1_bf[l], w2_bf[l],
                          rows_per_mod=SEQ).reshape(bsz, SEQ, D_MODEL)
        if not last:
            x_ctx = x_ctx + g1_c * merge_branches((ya_c, yb_c, yc_c), gd_c, gate_w[l], gate_b[l], branch_w[l], w_out[l])
            x_ctx = mlp_block(x_ctx.reshape(bsz * CTX_LEN, D_MODEL), norm2_g[l], _pad_mod(mod_c[None]), w1_bf[l], w2_bf[l],
                              rows_per_mod=bsz * CTX_LEN).reshape(bsz, CTX_LEN, D_MODEL)
    return rmsnorm(x_lat, final_g)
```

```python
import functools
import math
import jax
import jax.numpy as jnp
from jax import lax
import numpy as np
from jax.experimental import pallas as pl
from jax.experimental.pallas import tpu as pltpu

D_MODEL = 4096
BATCH = 2
SEQ = 8192
DEPTH = 4

CTX_LEN = 256
GRID_W = 64
W_MIX = D_MODEL // 4
W_A = W_MIX
H_A = 8
DH_A = W_A // H_A
QKV_BLOCK = 4
MLSTM_CHUNK = 64
W_B = W_MIX
S5_GROUP = 16
G_B = W_B // S5_GROUP
P_B = 64
W_C = W_MIX
HD_C = 64
H_C = W_C // HD_C
NG_C = 4
N_C = 128
SSD_CHUNK = 128
CONV_W = 5
XBC_W = W_C + 2 * NG_C * N_C
D_FF = 4 * D_MODEL
R_MOD = 512
R_GATE = 512
N_BRANCH = 3
N_MOD = 6
SPLIT_IDX = (W_A, 2 * W_A, 2 * W_A + W_B, 2 * W_A + W_B + W_C, 2 * W_A + W_B + W_C + XBC_W, 2 * W_A + W_B + W_C + XBC_W + 2 * H_C)
IN_COLS = 2 * W_A + W_B + W_C + XBC_W + 2 * H_C + R_GATE
EPS = 1e-6
F32 = jnp.float32
BF16 = jnp.bfloat16

V7X_VMEM_LIMIT_BYTES = 56 * 1024 * 1024
MOD_ROWS = 8


def rmsnorm(x, g):
    xf = x.astype(F32)
    xf = xf * lax.rsqrt(jnp.mean(xf * xf, axis=-1, keepdims=True) + EPS)
    return xf.astype(x.dtype) * g


def modulate(x, g, shift, scale):
    return rmsnorm(x, g) * (1 + scale) + shift


def adaln(cvec, down, up, b):
    return (jax.nn.silu(cvec) @ down) @ up + b


def dwconv(x, w, b):
    pad = CONV_W // 2
    y = lax.conv_general_dilated(x, w[:, None, :].astype(x.dtype), window_strides=(1,), padding=[(pad, pad)],
                                 dimension_numbers=('NWC', 'WIO', 'NWC'), feature_group_count=x.shape[-1])
    return y + b


def to_colmajor(t, rows):
    bsz, n, ch = t.shape
    return t.reshape(bsz, rows, GRID_W, ch).swapaxes(1, 2).reshape(bsz, n, ch)


def from_colmajor(t, rows):
    bsz, n, ch = t.shape
    return t.reshape(bsz, GRID_W, rows, ch).swapaxes(1, 2).reshape(bsz, n, ch)


def blockdiag(x, w):
    xb = x.reshape(*x.shape[:-1], -1, QKV_BLOCK)
    return jnp.einsum('blnc,ncd->blnd', xb, w).reshape(x.shape)


def mlstm_chunked(q, k, v, log_i, log_f, state):
    bsz, nh, n, dh = q.shape
    nc = n // MLSTM_CHUNK

    def chunks(t):
        return jnp.moveaxis(t.reshape(bsz, nh, nc, MLSTM_CHUNK, *t.shape[3:]), 2, 0)

    causal = jnp.tril(jnp.ones((MLSTM_CHUNK, MLSTM_CHUNK), dtype=bool))

    def step(carry, inp):
        c_st, n_st, m_st = carry
        qc, kc, vc, ic, fc = inp
        b = jnp.cumsum(fc, axis=-1)
        d_log = jnp.where(causal, b[..., :, None] - b[..., None, :] + ic[..., None, :], -jnp.inf)
        inter = b + m_st[..., None]
        m_t = jnp.maximum(inter, jnp.max(d_log, axis=-1))
        w_carry = jnp.exp(inter - m_t)
        s = jnp.einsum('bhtd,bhsd->bhts', qc, kc) * jnp.exp(d_log - m_t[..., None])
        num = jnp.einsum('bhts,bhsd->bhtd', s, vc) + w_carry[..., None] * jnp.einsum('bhtk,bhkv->bhtv', qc, c_st)
        den = jnp.sum(s, axis=-1) + w_carry * jnp.einsum('bhtk,bhk->bht', qc, n_st)
        h = num / jnp.maximum(jnp.abs(den), jnp.exp(-m_t))[..., None]
        b_end = b[..., -1]
        log_ws = b_end[..., None] - b + ic
        m_new = jnp.maximum(b_end + m_st, jnp.max(log_ws, axis=-1))
        ws = jnp.exp(log_ws - m_new[..., None])[..., None]
        w_prev = jnp.exp(b_end + m_st - m_new)
        c_new = w_prev[..., None, None] * c_st + jnp.einsum('bhsk,bhsv->bhkv', kc * ws, vc)
        n_new = w_prev[..., None] * n_st + jnp.sum(kc * ws, axis=2)
        return (c_new, n_new, m_new), h

    state, h = lax.scan(step, state, tuple(chunks(t) for t in (q, k, v, log_i, log_f)))
    return state, jnp.moveaxis(h, 0, 2).reshape(bsz, nh, n, dh)


def mlstm_features(xm, conv_w, conv_b, wq, wk, wv, gate_w, gate_b):
    bsz, n, _ = xm.shape
    xc = jax.nn.silu(dwconv(xm, conv_w, conv_b))
    q, k, v = blockdiag(xc, wq), blockdiag(xc, wk), blockdiag(xm, wv)
    g = jnp.einsum('blc,dcj->bdjl', jnp.concatenate([q, k, v], axis=-1), gate_w).astype(F32)
    g = g + gate_b.astype(F32)[None, :, :, None]

    def heads(t):
        return t.astype(F32).reshape(bsz, n, H_A, DH_A).transpose(0, 2, 1, 3)

    return xc, heads(q) * DH_A ** -0.5, heads(k), heads(v), g[:, :, :H_A], jax.nn.log_sigmoid(g[:, :, H_A:])


def mlstm_dir(feats, d, state, reverse):
    _, q, k, v, li, lf = feats
    seq = (q, k, v, li[:, d], lf[:, d])
    if reverse:
        seq = tuple(jnp.flip(t, axis=2) for t in seq)
    state, h = mlstm_chunked(*seq, state)
    if reverse:
        h = jnp.flip(h, axis=2)
    return state, h


def mlstm_readout(h, xc, zm, norm_g, skip):
    bsz, n, _ = xc.shape
    h = h.transpose(0, 2, 1, 3)
    mu = jnp.mean(h, axis=-1, keepdims=True)
    var = jnp.mean(jnp.square(h - mu), axis=-1, keepdims=True)
    h = ((h - mu) * lax.rsqrt(var + EPS)).reshape(bsz, n, W_A).astype(xc.dtype)
    return (h * norm_g + skip * xc) * jax.nn.silu(zm)


def mlstm_mixer(xm_c, zm_c, xm_l, zm_l, conv_w, conv_b, wq, wk, wv, gate_w, gate_b, norm_g, skip):
    fc = mlstm_features(xm_c, conv_w, conv_b, wq, wk, wv, gate_w, gate_b)
    fl = mlstm_features(xm_l, conv_w, conv_b, wq, wk, wv, gate_w, gate_b)
    bsz = xm_c.shape[0]
    hs_c, hs_l = [], []
    for d in range(2):
        zero = (jnp.zeros((bsz, H_A, DH_A, DH_A), F32), jnp.zeros((bsz, H_A, DH_A), F32), jnp.zeros((bsz, H_A), F32))
        st, hc = mlstm_dir(fc, d, zero, d == 1)
        _, hl = mlstm_dir(fl, d, st, d == 1)
        hs_c.append(hc)
        hs_l.append(hl)
    y_c = mlstm_readout(hs_c[0] + hs_c[1], fc[0], zm_c, norm_g, skip)
    y_l = mlstm_readout(hs_l[0] + hs_l[1], fl[0], zm_l, norm_g, skip)
    return y_c, y_l


def s5_glu(y, glu_w, glu_b, dtype):
    y = jax.nn.gelu(y)
    return (y * jax.nn.sigmoid(y @ glu_w.astype(F32) + glu_b.astype(F32))).astype(dtype)


S5_T = 8
LANES = 128
S5_GPB = LANES // S5_GROUP
S5_NB = W_B // LANES
S5_NS = S5_GPB * P_B
S5_CW = S5_T * LANES


def _s5_tables(lam_re, lam_im, log_dt, b_re, b_im, c_re, c_im, n_rows):
    lam = lax.complex(lam_re.astype(F32), lam_im.astype(F32))
    ldt = lam * jnp.exp(log_dt.astype(F32))[..., None]
    lam_bar = jnp.exp(ldt)
    b_bar = ((lam_bar - 1) / lam)[..., None] * lax.complex(b_re.astype(F32), b_im.astype(F32))
    c_mat = lax.complex(c_re.astype(F32), c_im.astype(F32))
    steps = jnp.arange(S5_T + 1, dtype=F32)
    pw = jnp.exp(steps[None, :, None, None] * ldt[:, None])
    eye = jnp.eye(S5_GPB, dtype=F32)

    def blocks(t):
        return t.reshape(S5_NB, S5_GPB, *t.shape[1:])

    kk = jnp.real(jnp.einsum('dgcp,djgp,dgpe->djgce', c_mat, pw[:, :S5_T], b_bar))
    tt = jnp.arange(S5_T)
    diff = tt[None, :] - tt[:, None]
    kf = kk[0][jnp.clip(diff, 0, S5_T - 1)] * (diff >= 0)[..., None, None, None].astype(F32)
    kb = kk[1][jnp.clip(-diff, 0, S5_T - 1)] * (diff <= 0)[..., None, None, None].astype(F32)
    kt = (kf + kb).transpose(2, 0, 4, 1, 3)
    kt = blocks(kt).transpose(0, 2, 1, 3, 4, 5)
    m_toep = kt[:, :, :, :, :, None, :] * eye[None, None, :, None, None, :, None]
    m_toep = m_toep.reshape(S5_NB, S5_CW, S5_CW)

    def w_in(d, powers):
        w = powers[:, :, :, None] * b_bar[d][None]
        w = jnp.stack([jnp.real(w), jnp.imag(w)], axis=0)
        w = w.transpose(2, 1, 4, 0, 3)
        w = blocks(w).transpose(0, 2, 1, 3, 4, 5)
        w = w[:, :, :, :, :, None, :] * eye[None, None, :, None, None, :, None]
        return w.reshape(S5_NB, S5_CW, 2 * S5_NS)

    def w_out(d, powers):
        w = c_mat[d][None] * powers[:, :, None, :]
        w = jnp.stack([jnp.real(w), -jnp.imag(w)], axis=0)
        w = w.transpose(2, 0, 4, 1, 3)
        w = blocks(w).transpose(0, 2, 1, 3, 4, 5)
        w = w[:, :, :, :, :, None, :] * eye[None, None, :, None, None, :, None]
        return w.reshape(S5_NB, 2 * S5_NS, S5_CW)

    we_f = w_in(0, pw[0, S5_T - 1::-1][:S5_T])
    we_b = w_in(1, pw[1, :S5_T])
    wo_f = w_out(0, pw[0, 1:])
    wo_b = w_out(1, pw[1, S5_T:0:-1])

    def rows(t):
        n = t.shape[0]
        t = t.reshape(n, S5_NB, S5_NS).transpose(1, 0, 2)
        return jnp.stack([jnp.real(t), jnp.imag(t)], axis=1)

    n_lv = max(1, int(math.log2(n_rows)))
    lv_steps = S5_T * (2.0 ** jnp.arange(n_lv, dtype=F32))
    jj = jnp.arange(n_rows, dtype=F32)
    tabs = []
    for d in range(2):
        lv = jnp.exp(lv_steps[:, None, None] * ldt[d][None])
        expo = jj if d == 0 else (n_rows - 1 - jj)
        pt = jnp.exp((S5_T * expo)[:, None, None] * ldt[d][None])
        tabs.append((rows(lv), rows(pt)))
    return dict(m_toep=m_toep.astype(BF16), we_f=we_f.astype(BF16), we_b=we_b.astype(BF16),
                wo_f=wo_f.astype(BF16), wo_b=wo_b.astype(BF16),
                lv_f=tabs[0][0], pt_f=tabs[0][1], lv_b=tabs[1][0], pt_b=tabs[1][1])


def _s5_state_scan(e, lv_ref, pt_ref, h0, reverse):
    n_rows = e.shape[0]
    row = lax.broadcasted_iota(jnp.int32, (n_rows, S5_NS), 0)

    def shift(x, s):
        if reverse:
            return jnp.where(row < n_rows - s, pltpu.roll(x, n_rows - s, 0), 0.0)
        return jnp.where(row >= s, pltpu.roll(x, s, 0), 0.0)

    e_re, e_im = e[:, :S5_NS], e[:, S5_NS:]
    z_re, z_im = shift(e_re, 1), shift(e_im, 1)
    k = 0
    while (1 << k) < n_rows:
        a_re, a_im = lv_ref[0, 0, k:k + 1, :], lv_ref[0, 1, k:k + 1, :]
        s_re, s_im = shift(z_re, 1 << k), shift(z_im, 1 << k)
        z_re, z_im = z_re + (a_re * s_re - a_im * s_im), z_im + (a_re * s_im + a_im * s_re)
        k += 1
    p_re, p_im = pt_ref[0, 0], pt_ref[0, 1]
    h_re = z_re + (p_re * h0[0:1, :] - p_im * h0[1:2, :])
    h_im = z_im + (p_re * h0[1:2, :] + p_im * h0[0:1, :])
    last = 0 if reverse else n_rows - 1
    a_re, a_im = lv_ref[0, 0, 0:1, :], lv_ref[0, 1, 0:1, :]
    l_re, l_im = h_re[last:last + 1, :], h_im[last:last + 1, :]
    o_re = a_re * l_re - a_im * l_im + e_re[last:last + 1, :]
    o_im = a_re * l_im + a_im * l_re + e_im[last:last + 1, :]
    return jnp.concatenate([h_re, h_im], axis=1), jnp.concatenate([o_re, o_im], axis=0)


def _s5_kernel(uf_ref, ub_ref, mt_ref, wef_ref, web_ref, wof_ref, wob_ref, lvf_ref, ptf_ref, lvb_ref, ptb_ref,
               d_ref, h0f_ref, h0b_ref, y1_ref, y2_ref, hff_ref, hfb_ref, sf_ref, sb_ref):
    i = pl.program_id(2)
    n_rows = uf_ref.shape[0] // S5_T

    @pl.when(i == 0)
    def _():
        sf_ref[...] = h0f_ref[...]
        sb_ref[...] = h0b_ref[...]

    def chunk_rows(ref):
        return jnp.concatenate([ref[pl.ds(t, n_rows, stride=S5_T), :] for t in range(S5_T)], axis=1)

    uf = chunk_rows(uf_ref)
    ub = chunk_rows(ub_ref)
    uf_bf = uf.astype(BF16)
    y_local = jnp.dot(uf_bf, mt_ref[0], preferred_element_type=F32)
    e_f = jnp.dot(uf_bf, wef_ref[0], preferred_element_type=F32)
    e_b = jnp.dot(ub.astype(BF16), web_ref[0], preferred_element_type=F32)
    h_f, out_f = _s5_state_scan(e_f, lvf_ref, ptf_ref, sf_ref[...], reverse=False)
    h_b, out_b = _s5_state_scan(e_b, lvb_ref, ptb_ref, sb_ref[...], reverse=True)
    sf_ref[...] = out_f
    sb_ref[...] = out_b
    y1 = y_local + jnp.dot(h_f.astype(BF16), wof_ref[0], preferred_element_type=F32)
    y2 = jnp.dot(h_b.astype(BF16), wob_ref[0], preferred_element_type=F32)
    d = d_ref[...]
    for t in range(S5_T):
        cols = slice(t * LANES, (t + 1) * LANES)
        y1_ref[pl.ds(t, n_rows, stride=S5_T), :] = y1[:, cols] + d * uf[:, cols]
        y2_ref[pl.ds(t, n_rows, stride=S5_T), :] = y2[:, cols]

    @pl.when(i == pl.num_programs(2) - 1)
    def _():
        hff_ref[...] = out_f
        hfb_ref[...] = out_b


def s5_scan(u, tabs, d_skip, h0_f, h0_b, tile):
    bsz, n, _ = u.shape
    assert n % tile == 0 and tile % S5_T == 0
    n_tiles = n // tile
    n_rows = tile // S5_T
    assert tabs['pt_f'].shape[2] == n_rows
    n_lv = tabs['lv_f'].shape[2]
    u_spec_f = pl.BlockSpec((None, tile, LANES), lambda b, n_, i: (n_, i, b))
    u_spec_b = pl.BlockSpec((None, tile, LANES), lambda b, n_, i: (n_, n_tiles - 1 - i, b))
    w_spec = pl.BlockSpec((1, S5_CW, S5_CW), lambda b, n_, i: (b, 0, 0))
    lv_spec = pl.BlockSpec((1, 2, n_lv, S5_NS), lambda b, n_, i: (b, 0, 0, 0))
    pt_spec = pl.BlockSpec((1, 2, n_rows, S5_NS), lambda b, n_, i: (b, 0, 0, 0))
    st_spec = pl.BlockSpec((None, None, 2, S5_NS), lambda b, n_, i: (n_, b, 0, 0))
    return pl.pallas_call(
        _s5_kernel,
        grid=(S5_NB, bsz, n_tiles),
        in_specs=[u_spec_f, u_spec_b, w_spec, w_spec, w_spec, w_spec, w_spec, lv_spec, pt_spec, lv_spec, pt_spec,
                  pl.BlockSpec((1, LANES), lambda b, n_, i: (0, b)), st_spec, st_spec],
        out_specs=[u_spec_f, u_spec_b, st_spec, st_spec],
        out_shape=[jax.ShapeDtypeStruct(u.shape, F32), jax.ShapeDtypeStruct(u.shape, F32),
                   jax.ShapeDtypeStruct((bsz, S5_NB, 2, S5_NS), F32), jax.ShapeDtypeStruct((bsz, S5_NB, 2, S5_NS), F32)],
        scratch_shapes=[pltpu.VMEM((2, S5_NS), F32), pltpu.VMEM((2, S5_NS), F32)],
        compiler_params=pltpu.CompilerParams(
            dimension_semantics=("arbitrary", "arbitrary", "arbitrary"),
            vmem_limit_bytes=V7X_VMEM_LIMIT_BYTES),
    )(u, u, tabs['m_toep'], tabs['we_f'], tabs['we_b'], tabs['wo_f'], tabs['wo_b'],
      tabs['lv_f'], tabs['pt_f'], tabs['lv_b'], tabs['pt_b'], d_skip.reshape(1, W_B).astype(F32), h0_f, h0_b)


def s5_mixer(u_c, u_l, lam_re, lam_im, log_dt, b_re, b_im, c_re, c_im, d_skip, glu_w, glu_b, tile_l=1024):
    bsz, n_c, _ = u_c.shape
    tile_l = min(tile_l, u_l.shape[1])
    prm = (lam_re, lam_im, log_dt, b_re, b_im, c_re, c_im)
    tabs_c = _s5_tables(*prm, n_rows=n_c // S5_T)
    tabs_l = _s5_tables(*prm, n_rows=tile_l // S5_T)
    zero = jnp.zeros((bsz, S5_NB, 2, S5_NS), F32)
    yc1, yc2, hf, hb = s5_scan(u_c.astype(F32), tabs_c, d_skip, zero, zero, n_c)
    yl1, yl2, _, _ = s5_scan(u_l.astype(F32), tabs_l, d_skip, hf, hb, tile_l)
    return s5_glu(yc1 + yc2, glu_w, glu_b, u_c.dtype), s5_glu(yl1 + yl2, glu_w, glu_b, u_l.dtype)


def segsum(a):
    t_len = a.shape[-1]
    cs = jnp.cumsum(a, axis=-1)
    mask = jnp.tril(jnp.ones((t_len, t_len), dtype=bool))
    return jnp.where(mask, cs[..., :, None] - cs[..., None, :], -jnp.inf)


def ssd_chunked(xv, a, bh, ch, h0):
    bsz, n, nh, hp = xv.shape
    nc = n // SSD_CHUNK
    xv = xv.reshape(bsz, nc, SSD_CHUNK, nh, hp)
    bh = bh.reshape(bsz, nc, SSD_CHUNK, nh, N_C)
    ch = ch.reshape(bsz, nc, SSD_CHUNK, nh, N_C)
    a = a.reshape(bsz, nc, SSD_CHUNK, nh).transpose(0, 3, 1, 2)
    a_cs = jnp.cumsum(a, axis=-1)
    scores = jnp.einsum('bclhn,bcshn->bhcls', ch, bh) * jnp.exp(segsum(a))
    y_diag = jnp.einsum('bhcls,bcshp->bclhp', scores, xv)
    decay_states = jnp.exp(a_cs[..., -1:] - a_cs).transpose(0, 2, 3, 1)
    states = jnp.einsum('bclhn,bclhp->bchpn', bh * decay_states[..., None], xv)
    states = jnp.concatenate([h0[:, None], states], axis=1)
    decay_chunk = jnp.exp(segsum(jnp.pad(a_cs[..., -1], ((0, 0), (0, 0), (1, 0)))))
    new_states = jnp.einsum('bhzc,bchpn->bzhpn', decay_chunk, states)
    states, final = new_states[:, :-1], new_states[:, -1]
    y_off = jnp.einsum('bclhn,bchpn->bclhp', ch, states) * jnp.exp(a_cs).transpose(0, 2, 3, 1)[..., None]
    return (y_diag + y_off).reshape(bsz, n, nh, hp), final


def ssd_dir(xbc, dt_raw, a_log, dt_bias, h0, reverse):
    if reverse:
        xbc, dt_raw = jnp.flip(xbc, axis=1), jnp.flip(dt_raw, axis=1)
    bsz, n, _ = xbc.shape
    xs, bm, cm = jnp.split(xbc.astype(F32), (W_C, W_C + NG_C * N_C), axis=-1)
    rep = H_C // NG_C
    bh = jnp.repeat(bm.reshape(bsz, n, NG_C, N_C), rep, axis=2)
    ch = jnp.repeat(cm.reshape(bsz, n, NG_C, N_C), rep, axis=2)
    dt = jax.nn.softplus(dt_raw.astype(F32) + dt_bias.astype(F32))
    a = -jnp.exp(a_log.astype(F32))
    y, final = ssd_chunked(xs.reshape(bsz, n, H_C, HD_C) * dt[..., None], a * dt, bh, ch, h0)
    if reverse:
        y = jnp.flip(y, axis=1)
    return y, final


def ssd_mixer(zs_c, xbc_c, dt_c, zs_l, xbc_l, dt_l, rows, conv_w, conv_b, a_log, dt_bias, d_skip, norm_g):
    xbc_c = jax.nn.silu(dwconv(xbc_c, conv_w, conv_b))
    xbc_l = jax.nn.silu(dwconv(to_colmajor(xbc_l, rows), conv_w, conv_b))
    dt_l = to_colmajor(dt_l, rows)
    bsz = xbc_c.shape[0]

    def skip(xbc):
        return xbc[..., :W_C].astype(F32).reshape(bsz, xbc.shape[1], H_C, HD_C) * d_skip.astype(F32)[:, None]

    y_c, y_l = skip(xbc_c), skip(xbc_l)
    for d in range(2):
        rev = d == 1
        h0 = jnp.zeros((bsz, H_C, HD_C, N_C), F32)
        yc, st = ssd_dir(xbc_c, dt_c[..., d * H_C:(d + 1) * H_C], a_log[d], dt_bias[d], h0, rev)
        yl, _ = ssd_dir(xbc_l, dt_l[..., d * H_C:(d + 1) * H_C], a_log[d], dt_bias[d], st, rev)
        y_c = y_c + yc
        y_l = y_l + yl
    y_c = y_c.reshape(bsz, -1, W_C)
    y_l = from_colmajor(y_l.reshape(bsz, -1, W_C), rows)
    out_c = rmsnorm(y_c.astype(zs_c.dtype) * jax.nn.silu(zs_c), norm_g)
    out_l = rmsnorm(y_l.astype(zs_l.dtype) * jax.nn.silu(zs_l), norm_g)
    return out_c, out_l


def merge_branches(ys, gd, gate_w, gate_b, branch_w, w_out):
    terms = [jax.nn.sigmoid(gd @ gate_w[i] + gate_b[i]) * (ys[i].astype(gd.dtype) @ branch_w[i]) for i in range(N_BRANCH)]
    return (terms[0] + terms[1] + terms[2]) @ w_out


def _mlp_kernel(x_ref, g_ref, mod_ref, w1_ref, w2_ref, o_ref, h_ref):
    f = pl.program_id(1)

    @pl.when(f == 0)
    def _():
        x = x_ref[...]
        xn = x * lax.rsqrt(jnp.mean(x * x, axis=-1, keepdims=True) + EPS) * g_ref[...]
        h = xn * (1.0 + mod_ref[0, 4:5, :]) + mod_ref[0, 3:4, :]
        h_ref[...] = h.astype(BF16)
        o_ref[...] = jnp.zeros_like(o_ref)

    a = jnp.dot(h_ref[...], w1_ref[...], preferred_element_type=F32)
    a = jnp.square(jnp.maximum(a, 0.0)).astype(BF16)
    o_ref[...] += jnp.dot(a, w2_ref[...], preferred_element_type=F32)

    @pl.when(f == pl.num_programs(1) - 1)
    def _():
        o_ref[...] = x_ref[...] + mod_ref[0, 5:6, :] * o_ref[...]


def mlp_block(x, g, mod, w1, w2, rows_per_mod, tm=512, tf=512):
    m, d = x.shape
    dff = w1.shape[1]
    tm = min(tm, m)
    assert m % tm == 0 and rows_per_mod % tm == 0 and dff % tf == 0
    tiles_per_mod = rows_per_mod // tm
    single = pl.Buffered(1)
    return pl.pallas_call(
        _mlp_kernel,
        grid=(m // tm, dff // tf),
        in_specs=[
            pl.BlockSpec((tm, d), lambda i, f: (i, 0), pipeline_mode=single),
            pl.BlockSpec((1, d), lambda i, f: (0, 0), pipeline_mode=single),
            pl.BlockSpec((1, MOD_ROWS, d), lambda i, f: (i // tiles_per_mod, 0, 0), pipeline_mode=single),
            pl.BlockSpec((d, tf), lambda i, f: (0, f)),
            pl.BlockSpec((tf, d), lambda i, f: (f, 0)),
        ],
        out_specs=pl.BlockSpec((tm, d), lambda i, f: (i, 0), pipeline_mode=single),
        out_shape=jax.ShapeDtypeStruct((m, d), F32),
        scratch_shapes=[pltpu.VMEM((tm, d), BF16)],
        compiler_params=pltpu.CompilerParams(
            dimension_semantics=("arbitrary", "arbitrary"),
            vmem_limit_bytes=V7X_VMEM_LIMIT_BYTES),
    )(x, g.reshape(1, d), mod, w1, w2)


def _pad_mod(mod):
    nb = mod.shape[0]
    mod = mod.reshape(nb, N_MOD, D_MODEL)
    return jnp.pad(mod, ((0, 0), (0, MOD_ROWS - N_MOD), (0, 0)))


def kernel(x, c, ctx, c_ctx, norm1_g, norm2_g, final_g, mod_down, mod_up, mod_b, w_in,
           m_conv_w, m_conv_b, m_wq, m_wk, m_wv, m_gate_w, m_gate_b, m_norm_g, m_skip,
           s5_lam_re, s5_lam_im, s5_log_dt, s5_b_re, s5_b_im, s5_c_re, s5_c_im, s5_d, s5_glu_w, s5_glu_b,
           ssd_conv_w, ssd_conv_b, ssd_a_log, ssd_dt_bias, ssd_d, ssd_norm_g,
           gate_w, gate_b, branch_w, w_out, mlp_w1, mlp_w2):
    rows = x.shape[1] // GRID_W
    bsz = x.shape[0]
    x_lat, x_ctx = x, ctx
    w1_bf = mlp_w1.astype(BF16)
    w2_bf = mlp_w2.astype(BF16)
    for l in range(DEPTH):
        last = l == DEPTH - 1
        mod_l = adaln(c, mod_down[l], mod_up[l], mod_b[l])
        mod_c = adaln(c_ctx, mod_down[l], mod_up[l], mod_b[l])
        sh1_l, sc1_l, g1_l, sh2_l, sc2_l, g2_l = jnp.split(mod_l[:, None, :], N_MOD, axis=-1)
        sh1_c, sc1_c, g1_c, sh2_c, sc2_c, g2_c = jnp.split(mod_c, N_MOD, axis=-1)
        p_l = modulate(x_lat, norm1_g[l], sh1_l, sc1_l) @ w_in[l]
        p_c = modulate(x_ctx, norm1_g[l], sh1_c, sc1_c) @ w_in[l]
        xm_l, zm_l, u_l, zs_l, xbc_l, dt_l, gd_l = jnp.split(p_l, SPLIT_IDX, axis=-1)
        xm_c, zm_c, u_c, zs_c, xbc_c, dt_c, gd_c = jnp.split(p_c, SPLIT_IDX, axis=-1)
        ya_c, ya_l = mlstm_mixer(xm_c, zm_c, xm_l, zm_l, m_conv_w[l], m_conv_b[l], m_wq[l], m_wk[l], m_wv[l],
                                 m_gate_w[l], m_gate_b[l], m_norm_g[l], m_skip[l])
        yb_c, yb_l = s5_mixer(u_c, u_l, s5_lam_re[l], s5_lam_im[l], s5_log_dt[l], s5_b_re[l], s5_b_im[l],
                              s5_c_re[l], s5_c_im[l], s5_d[l], s5_glu_w[l], s5_glu_b[l])
        yc_c, yc_l = ssd_mixer(zs_c, xbc_c, dt_c, zs_l, xbc_l, dt_l, rows, ssd_conv_w[l], ssd_conv_b[l],
                               ssd_a_log[l], ssd_dt_bias[l], ssd_d[l], ssd_norm_g[l])
        x_lat = x_lat + g1_l * merge_branches((ya_l, yb_l, yc_l), gd_l, gate_w[l], gate_b[l], branch_w[l], w_out[l])
        x_lat = mlp_block(x_lat.reshape(bsz * SEQ, D_MODEL), norm2_g[l], _pad_mod(mod_l), w1_bf[l], w2_bf[l],
                          rows_per_mod=SEQ).reshape(bsz, SEQ, D_MODEL)
        if not last:
            x_ctx = x_ctx + g1_c * merge_branches((ya_c, yb_c, yc_c), gd_c, gate_w[l], gate_b[l], branch_w[l], w_out[l])
            x_ctx = mlp_block(x_ctx.reshape(bsz * CTX_LEN, D_MODEL), norm2_g[l], _pad_mod(mod_c[None]), w1_bf[l], w2_bf[l],
                              rows_per_mod=bsz * CTX_LEN).reshape(bsz, CTX_LEN, D_MODEL)
    return rmsnorm(x_lat, final_g)
```

```python
import functools
import math
import jax
import jax.numpy as jnp
from jax import lax
import numpy as np
from jax.experimental import pallas as pl
from jax.experimental.pallas import tpu as pltpu

D_MODEL = 4096
BATCH = 2
SEQ = 8192
DEPTH = 4

CTX_LEN = 256
GRID_W = 64
W_MIX = D_MODEL // 4
W_A = W_MIX
H_A = 8
DH_A = W_A // H_A
QKV_BLOCK = 4
MLSTM_CHUNK = 64
W_B = W_MIX
S5_GROUP = 16
G_B = W_B // S5_GROUP
P_B = 64
W_C = W_MIX
HD_C = 64
H_C = W_C // HD_C
NG_C = 4
N_C = 128
SSD_CHUNK = 128
CONV_W = 5
XBC_W = W_C + 2 * NG_C * N_C
D_FF = 4 * D_MODEL
R_MOD = 512
R_GATE = 512
N_BRANCH = 3
N_MOD = 6
SPLIT_IDX = (W_A, 2 * W_A, 2 * W_A + W_B, 2 * W_A + W_B + W_C, 2 * W_A + W_B + W_C + XBC_W, 2 * W_A + W_B + W_C + XBC_W + 2 * H_C)
IN_COLS = 2 * W_A + W_B + W_C + XBC_W + 2 * H_C + R_GATE
EPS = 1e-6
F32 = jnp.float32
BF16 = jnp.bfloat16

V7X_VMEM_LIMIT_BYTES = 56 * 1024 * 1024
MOD_ROWS = 8


def rmsnorm(x, g):
    xf = x.astype(F32)
    xf = xf * lax.rsqrt(jnp.mean(xf * xf, axis=-1, keepdims=True) + EPS)
    return xf.astype(x.dtype) * g


def modulate(x, g, shift, scale):
    return rmsnorm(x, g) * (1 + scale) + shift


def adaln(cvec, down, up, b):
    return (jax.nn.silu(cvec) @ down) @ up + b


def dwconv(x, w, b):
    pad = CONV_W // 2
    y = lax.conv_general_dilated(x, w[:, None, :].astype(x.dtype), window_strides=(1,), padding=[(pad, pad)],
                                 dimension_numbers=('NWC', 'WIO', 'NWC'), feature_group_count=x.shape[-1])
    return y + b


def to_colmajor(t, rows):
    bsz, n, ch = t.shape
    return t.reshape(bsz, rows, GRID_W, ch).swapaxes(1, 2).reshape(bsz, n, ch)


def from_colmajor(t, rows):
    bsz, n, ch = t.shape
    return t.reshape(bsz, GRID_W, rows, ch).swapaxes(1, 2).reshape(bsz, n, ch)


def blockdiag(x, w):
    xb = x.reshape(*x.shape[:-1], -1, QKV_BLOCK)
    return jnp.einsum('blnc,ncd->blnd', xb, w).reshape(x.shape)


SCAN_CHUNK = 128


def _cumsum_incl(x, axis, reverse):
    n = x.shape[axis]
    idx = lax.broadcasted_iota(jnp.int32, x.shape, axis)
    s = 1
    while s < n:
        if reverse:
            x = x + jnp.where(idx < n - s, pltpu.roll(x, n - s, axis), 0.0)
        else:
            x = x + jnp.where(idx >= s, pltpu.roll(x, s, axis), 0.0)
        s *= 2
    return x


def _scan_chunk_index(s, n_ctx_chunks, n_chunks, reverse):
    if not reverse:
        return s
    return jnp.where(s < n_ctx_chunks, n_ctx_chunks - 1 - s, n_chunks - 1 - (s - n_ctx_chunks))


def _mlstm_kernel(*refs, reverse, readout):
    if readout:
        (q_ref, k_ref, v_ref, gr_ref, gc_ref, hb_ref, xc_ref, zm_ref, ng_ref, sk_ref, o_ref, c_scr, n_scr, m_scr) = refs
    else:
        (q_ref, k_ref, v_ref, gr_ref, gc_ref, o_ref, c_scr, n_scr, m_scr) = refs
    t_len = q_ref.shape[0]

    @pl.when(pl.program_id(1) == 0)
    def _():
        c_scr[...] = jnp.zeros_like(c_scr)
        n_scr[...] = jnp.zeros_like(n_scr)
        m_scr[...] = jnp.zeros_like(m_scr)

    b_rows = _cumsum_incl(gr_ref[H_A:, :], 1, reverse)
    b_cols = _cumsum_incl(gc_ref[:, H_A:], 0, reverse)
    i_rows = gr_ref[:H_A, :]
    i_cols = gc_ref[:, :H_A]
    row_id = lax.broadcasted_iota(jnp.int32, (t_len, t_len), 0)
    col_id = lax.broadcasted_iota(jnp.int32, (t_len, t_len), 1)
    allowed = (col_id >= row_id) if reverse else (col_id <= row_id)
    end = 0 if reverse else t_len - 1
    m_all = m_scr[...]
    n_all = n_scr[...]
    for h in range(H_A):
        sl = slice(h * DH_A, (h + 1) * DH_A)
        q = q_ref[:, sl]
        k = k_ref[:, sl]
        v = v_ref[:, sl]
        b_row, b_col = b_rows[h:h + 1, :], b_cols[:, h:h + 1]
        i_row, i_col = i_rows[h:h + 1, :], i_cols[:, h:h + 1]
        b_end = b_row[:, end:end + 1]
        m_st = m_all[h:h + 1, 0:1]
        n_st = n_all[h:h + 1, :]
        c_st = c_scr[h]
        d_log = jnp.where(allowed, b_col - b_row + i_row, -jnp.inf)
        inter = b_col + m_st
        m_t = jnp.maximum(inter, jnp.max(d_log, axis=-1, keepdims=True))
        w_carry = jnp.exp(inter - m_t)
        qb, kb, vb = q.astype(BF16), k.astype(BF16), v.astype(BF16)
        qk = lax.dot_general(qb, kb, (((1,), (1,)), ((), ())), preferred_element_type=F32)
        s = qk * jnp.exp(d_log - m_t)
        num = (jnp.dot(s.astype(BF16), vb, preferred_element_type=F32)
               + w_carry * jnp.dot(qb, c_st.astype(BF16), preferred_element_type=F32))
        den = jnp.sum(s, axis=-1, keepdims=True) + w_carry * jnp.sum(q * n_st, axis=-1, keepdims=True)
        hh = num / jnp.maximum(jnp.abs(den), jnp.exp(-m_t))
        log_ws = b_end - b_col + i_col
        m_new = jnp.maximum(b_end + m_st, jnp.max(log_ws, axis=0, keepdims=True))
        kw = k * jnp.exp(log_ws - m_new)
        w_prev = jnp.exp(b_end + m_st - m_new)
        c_scr[h] = w_prev * c_st + lax.dot_general(kw.astype(BF16), vb, (((0,), (0,)), ((), ())),
                                                   preferred_element_type=F32)
        n_scr[h:h + 1, :] = w_prev * n_st + jnp.sum(kw, axis=0, keepdims=True)
        m_scr[h:h + 1, :] = jnp.broadcast_to(m_new, (1, DH_A))
        if readout:
            hh = hh + hb_ref[:, sl]
            mu = jnp.mean(hh, axis=-1, keepdims=True)
            var = jnp.mean(jnp.square(hh - mu), axis=-1, keepdims=True)
            hn = (hh - mu) * lax.rsqrt(var + EPS)
            zm = zm_ref[:, sl]
            o_ref[:, sl] = (hn * ng_ref[:, sl] + sk_ref[:, sl] * xc_ref[:, sl]) * (zm * jax.nn.sigmoid(zm))
        else:
            o_ref[:, sl] = hh


def mlstm_scan(q, k, v, g_rows, g_cols, n_ctx, reverse, readout_args=None):
    bsz, n, w = q.shape
    t_len = SCAN_CHUNK
    assert n % t_len == 0 and n_ctx % t_len == 0
    n_chunks, n_ctx_chunks = n // t_len, n_ctx // t_len
    order = functools.partial(_scan_chunk_index, n_ctx_chunks=n_ctx_chunks, n_chunks=n_chunks, reverse=reverse)
    tok = pl.BlockSpec((None, t_len, w), lambda b, s: (b, order(s), 0))
    in_specs = [tok, tok, tok,
                pl.BlockSpec((None, 2 * H_A, t_len), lambda b, s: (b, 0, order(s))),
                pl.BlockSpec((None, t_len, 2 * H_A), lambda b, s: (b, order(s), 0))]
    args = [q, k, v, g_rows, g_cols]
    if readout_args is not None:
        h_other, xc, zm, norm_g, skip = readout_args
        vec = pl.BlockSpec((1, w), lambda b, s: (0, 0))
        in_specs += [tok, tok, tok, vec, vec]
        args += [h_other, xc, zm, norm_g.reshape(1, w).astype(F32), skip.reshape(1, w).astype(F32)]
    return pl.pallas_call(
        functools.partial(_mlstm_kernel, reverse=reverse, readout=readout_args is not None),
        grid=(bsz, n_chunks),
        in_specs=in_specs,
        out_specs=tok,
        out_shape=jax.ShapeDtypeStruct((bsz, n, w), F32),
        scratch_shapes=[pltpu.VMEM((H_A, DH_A, DH_A), F32), pltpu.VMEM((H_A, DH_A), F32), pltpu.VMEM((H_A, DH_A), F32)],
        compiler_params=pltpu.CompilerParams(
            dimension_semantics=("arbitrary", "arbitrary"),
            vmem_limit_bytes=V7X_VMEM_LIMIT_BYTES),
    )(*args)


def mlstm_features(xm, conv_w, conv_b, wq, wk, wv, gate_w, gate_b):
    xc = jax.nn.silu(dwconv(xm, conv_w, conv_b))
    q, k, v = blockdiag(xc, wq), blockdiag(xc, wk), blockdiag(xm, wv)
    qkv = jnp.concatenate([q, k, v], axis=-1)
    g_rows = jnp.einsum('blc,dcj->bdjl', qkv, gate_w).astype(F32) + gate_b.astype(F32)[None, :, :, None]
    g_cols = jnp.einsum('blc,dcj->bdlj', qkv, gate_w).astype(F32) + gate_b.astype(F32)[None, :, None, :]
    g_rows = jnp.concatenate([g_rows[:, :, :H_A], jax.nn.log_sigmoid(g_rows[:, :, H_A:])], axis=2)
    g_cols = jnp.concatenate([g_cols[..., :H_A], jax.nn.log_sigmoid(g_cols[..., H_A:])], axis=-1)
    return xc, q.astype(F32) * DH_A ** -0.5, k.astype(F32), v.astype(F32), g_rows, g_cols


def mlstm_mixer(xm_c, zm_c, xm_l, zm_l, conv_w, conv_b, wq, wk, wv, gate_w, gate_b, norm_g, skip):
    n_ctx = xm_c.shape[1]
    fc = mlstm_features(xm_c, conv_w, conv_b, wq, wk, wv, gate_w, gate_b)
    fl = mlstm_features(xm_l, conv_w, conv_b, wq, wk, wv, gate_w, gate_b)
    xc, q, k, v = (jnp.concatenate([a, b], axis=1) for a, b in zip(fc[:4], fl[:4]))
    g_rows = jnp.concatenate([fc[4], fl[4]], axis=3)
    g_cols = jnp.concatenate([fc[5], fl[5]], axis=2)
    zm = jnp.concatenate([zm_c, zm_l], axis=1)
    h_bwd = mlstm_scan(q, k, v, g_rows[:, 1], g_cols[:, 1], n_ctx, reverse=True)
    y = mlstm_scan(q, k, v, g_rows[:, 0], g_cols[:, 0], n_ctx, reverse=False,
                   readout_args=(h_bwd, xc, zm, norm_g, skip))
    return y[:, :n_ctx], y[:, n_ctx:]


def s5_glu(y, glu_w, glu_b, dtype):
    y = jax.nn.gelu(y)
    return (y * jax.nn.sigmoid(y @ glu_w.astype(F32) + glu_b.astype(F32))).astype(dtype)


S5_T = 8
LANES = 128
S5_GPB = LANES // S5_GROUP
S5_NB = W_B // LANES
S5_NS = S5_GPB * P_B
S5_CW = S5_T * LANES


def _s5_tables(lam_re, lam_im, log_dt, b_re, b_im, c_re, c_im, n_rows):
    lam = lax.complex(lam_re.astype(F32), lam_im.astype(F32))
    ldt = lam * jnp.exp(log_dt.astype(F32))[..., None]
    lam_bar = jnp.exp(ldt)
    b_bar = ((lam_bar - 1) / lam)[..., None] * lax.complex(b_re.astype(F32), b_im.astype(F32))
    c_mat = lax.complex(c_re.astype(F32), c_im.astype(F32))
    steps = jnp.arange(S5_T + 1, dtype=F32)
    pw = jnp.exp(steps[None, :, None, None] * ldt[:, None])
    eye = jnp.eye(S5_GPB, dtype=F32)

    def blocks(t):
        return t.reshape(S5_NB, S5_GPB, *t.shape[1:])

    kk = jnp.real(jnp.einsum('dgcp,djgp,dgpe->djgce', c_mat, pw[:, :S5_T], b_bar))
    tt = jnp.arange(S5_T)
    diff = tt[None, :] - tt[:, None]
    kf = kk[0][jnp.clip(diff, 0, S5_T - 1)] * (diff >= 0)[..., None, None, None].astype(F32)
    kb = kk[1][jnp.clip(-diff, 0, S5_T - 1)] * (diff <= 0)[..., None, None, None].astype(F32)
    kt = (kf + kb).transpose(2, 0, 4, 1, 3)
    kt = blocks(kt).transpose(0, 2, 1, 3, 4, 5)
    m_toep = kt[:, :, :, :, :, None, :] * eye[None, None, :, None, None, :, None]
    m_toep = m_toep.reshape(S5_NB, S5_CW, S5_CW)

    def w_in(d, powers):
        w = powers[:, :, :, None] * b_bar[d][None]
        w = jnp.stack([jnp.real(w), jnp.imag(w)], axis=0)
        w = w.transpose(2, 1, 4, 0, 3)
        w = blocks(w).transpose(0, 2, 1, 3, 4, 5)
        w = w[:, :, :, :, :, None, :] * eye[None, None, :, None, None, :, None]
        return w.reshape(S5_NB, S5_CW, 2 * S5_NS)

    def w_out(d, powers):
        w = c_mat[d][None] * powers[:, :, None, :]
        w = jnp.stack([jnp.real(w), -jnp.imag(w)], axis=0)
        w = w.transpose(2, 0, 4, 1, 3)
        w = blocks(w).transpose(0, 2, 1, 3, 4, 5)
        w = w[:, :, :, :, :, None, :] * eye[None, None, :, None, None, :, None]
        return w.reshape(S5_NB, 2 * S5_NS, S5_CW)

    we_f = w_in(0, pw[0, S5_T - 1::-1][:S5_T])
    we_b = w_in(1, pw[1, :S5_T])
    wo_f = w_out(0, pw[0, 1:])
    wo_b = w_out(1, pw[1, S5_T:0:-1])

    def rows(t):
        n = t.shape[0]
        t = t.reshape(n, S5_NB, S5_NS).transpose(1, 0, 2)
        return jnp.stack([jnp.real(t), jnp.imag(t)], axis=1)

    n_lv = max(1, int(math.log2(n_rows)))
    lv_steps = S5_T * (2.0 ** jnp.arange(n_lv, dtype=F32))
    jj = jnp.arange(n_rows, dtype=F32)
    tabs = []
    for d in range(2):
        lv = jnp.exp(lv_steps[:, None, None] * ldt[d][None])
        expo = jj if d == 0 else (n_rows - 1 - jj)
        pt = jnp.exp((S5_T * expo)[:, None, None] * ldt[d][None])
        tabs.append((rows(lv), rows(pt)))
    return dict(m_toep=m_toep.astype(BF16), we_f=we_f.astype(BF16), we_b=we_b.astype(BF16),
                wo_f=wo_f.astype(BF16), wo_b=wo_b.astype(BF16),
                lv_f=tabs[0][0], pt_f=tabs[0][1], lv_b=tabs[1][0], pt_b=tabs[1][1])


def _s5_state_scan(e, lv_ref, pt_ref, h0, reverse):
    n_rows = e.shape[0]
    row = lax.broadcasted_iota(jnp.int32, (n_rows, S5_NS), 0)

    def shift(x, s):
        if reverse:
            return jnp.where(row < n_rows - s, pltpu.roll(x, n_rows - s, 0), 0.0)
        return jnp.where(row >= s, pltpu.roll(x, s, 0), 0.0)

    e_re, e_im = e[:, :S5_NS], e[:, S5_NS:]
    z_re, z_im = shift(e_re, 1), shift(e_im, 1)
    k = 0
    while (1 << k) < n_rows:
        a_re, a_im = lv_ref[0, 0, k:k + 1, :], lv_ref[0, 1, k:k + 1, :]
        s_re, s_im = shift(z_re, 1 << k), shift(z_im, 1 << k)
        z_re, z_im = z_re + (a_re * s_re - a_im * s_im), z_im + (a_re * s_im + a_im * s_re)
        k += 1
    p_re, p_im = pt_ref[0, 0], pt_ref[0, 1]
    h_re = z_re + (p_re * h0[0:1, :] - p_im * h0[1:2, :])
    h_im = z_im + (p_re * h0[1:2, :] + p_im * h0[0:1, :])
    last = 0 if reverse else n_rows - 1
    a_re, a_im = lv_ref[0, 0, 0:1, :], lv_ref[0, 1, 0:1, :]
    l_re, l_im = h_re[last:last + 1, :], h_im[last:last + 1, :]
    o_re = a_re * l_re - a_im * l_im + e_re[last:last + 1, :]
    o_im = a_re * l_im + a_im * l_re + e_im[last:last + 1, :]
    return jnp.concatenate([h_re, h_im], axis=1), jnp.concatenate([o_re, o_im], axis=0)


def _s5_kernel(uf_ref, ub_ref, mt_ref, wef_ref, web_ref, wof_ref, wob_ref, lvf_ref, ptf_ref, lvb_ref, ptb_ref,
               d_ref, h0f_ref, h0b_ref, y1_ref, y2_ref, hff_ref, hfb_ref, sf_ref, sb_ref):
    i = pl.program_id(2)
    n_rows = uf_ref.shape[0] // S5_T

    @pl.when(i == 0)
    def _():
        sf_ref[...] = h0f_ref[...]
        sb_ref[...] = h0b_ref[...]

    def chunk_rows(ref):
        return jnp.concatenate([ref[pl.ds(t, n_rows, stride=S5_T), :] for t in range(S5_T)], axis=1)

    uf = chunk_rows(uf_ref)
    ub = chunk_rows(ub_ref)
    uf_bf = uf.astype(BF16)
    y_local = jnp.dot(uf_bf, mt_ref[0], preferred_element_type=F32)
    e_f = jnp.dot(uf_bf, wef_ref[0], preferred_element_type=F32)
    e_b = jnp.dot(ub.astype(BF16), web_ref[0], preferred_element_type=F32)
    h_f, out_f = _s5_state_scan(e_f, lvf_ref, ptf_ref, sf_ref[...], reverse=False)
    h_b, out_b = _s5_state_scan(e_b, lvb_ref, ptb_ref, sb_ref[...], reverse=True)
    sf_ref[...] = out_f
    sb_ref[...] = out_b
    y1 = y_local + jnp.dot(h_f.astype(BF16), wof_ref[0], preferred_element_type=F32)
    y2 = jnp.dot(h_b.astype(BF16), wob_ref[0], preferred_element_type=F32)
    d = d_ref[...]
    for t in range(S5_T):
        cols = slice(t * LANES, (t + 1) * LANES)
        y1_ref[pl.ds(t, n_rows, stride=S5_T), :] = y1[:, cols] + d * uf[:, cols]
        y2_ref[pl.ds(t, n_rows, stride=S5_T), :] = y2[:, cols]

    @pl.when(i == pl.num_programs(2) - 1)
    def _():
        hff_ref[...] = out_f
        hfb_ref[...] = out_b


def s5_scan(u, tabs, d_skip, h0_f, h0_b, tile):
    bsz, n, _ = u.shape
    assert n % tile == 0 and tile % S5_T == 0
    n_tiles = n // tile
    n_rows = tile // S5_T
    assert tabs['pt_f'].shape[2] == n_rows
    n_lv = tabs['lv_f'].shape[2]
    u_spec_f = pl.BlockSpec((None, tile, LANES), lambda b, n_, i: (n_, i, b))
    u_spec_b = pl.BlockSpec((None, tile, LANES), lambda b, n_, i: (n_, n_tiles - 1 - i, b))
    w_spec = pl.BlockSpec((1, S5_CW, S5_CW), lambda b, n_, i: (b, 0, 0))
    lv_spec = pl.BlockSpec((1, 2, n_lv, S5_NS), lambda b, n_, i: (b, 0, 0, 0))
    pt_spec = pl.BlockSpec((1, 2, n_rows, S5_NS), lambda b, n_, i: (b, 0, 0, 0))
    st_spec = pl.BlockSpec((None, None, 2, S5_NS), lambda b, n_, i: (n_, b, 0, 0))
    return pl.pallas_call(
        _s5_kernel,
        grid=(S5_NB, bsz, n_tiles),
        in_specs=[u_spec_f, u_spec_b, w_spec, w_spec, w_spec, w_spec, w_spec, lv_spec, pt_spec, lv_spec, pt_spec,
                  pl.BlockSpec((1, LANES), lambda b, n_, i: (0, b)), st_spec, st_spec],
        out_specs=[u_spec_f, u_spec_b, st_spec, st_spec],
        out_shape=[jax.ShapeDtypeStruct(u.shape, F32), jax.ShapeDtypeStruct(u.shape, F32),
                   jax.ShapeDtypeStruct((bsz, S5_NB, 2, S5_NS), F32), jax.ShapeDtypeStruct((bsz, S5_NB, 2, S5_NS), F32)],
        scratch_shapes=[pltpu.VMEM((2, S5_NS), F32), pltpu.VMEM((2, S5_NS), F32)],
        compiler_params=pltpu.CompilerParams(
            dimension_semantics=("arbitrary", "arbitrary", "arbitrary"),
            vmem_limit_bytes=V7X_VMEM_LIMIT_BYTES),
    )(u, u, tabs['m_toep'], tabs['we_f'], tabs['we_b'], tabs['wo_f'], tabs['wo_b'],
      tabs['lv_f'], tabs['pt_f'], tabs['lv_b'], tabs['pt_b'], d_skip.reshape(1, W_B).astype(F32), h0_f, h0_b)


def s5_mixer(u_c, u_l, lam_re, lam_im, log_dt, b_re, b_im, c_re, c_im, d_skip, glu_w, glu_b, tile_l=1024):
    bsz, n_c, _ = u_c.shape
    tile_l = min(tile_l, u_l.shape[1])
    prm = (lam_re, lam_im, log_dt, b_re, b_im, c_re, c_im)
    tabs_c = _s5_tables(*prm, n_rows=n_c // S5_T)
    tabs_l = _s5_tables(*prm, n_rows=tile_l // S5_T)
    zero = jnp.zeros((bsz, S5_NB, 2, S5_NS), F32)
    yc1, yc2, hf, hb = s5_scan(u_c.astype(F32), tabs_c, d_skip, zero, zero, n_c)
    yl1, yl2, _, _ = s5_scan(u_l.astype(F32), tabs_l, d_skip, hf, hb, tile_l)
    return s5_glu(yc1 + yc2, glu_w, glu_b, u_c.dtype), s5_glu(yl1 + yl2, glu_w, glu_b, u_l.dtype)


HPG_C = H_C // NG_C
GW_C = HPG_C * HD_C


def _softplus(x):
    return jnp.maximum(x, 0.0) + jnp.log1p(jnp.exp(-jnp.abs(x)))


def _ssd_kernel(*refs, reverse, combine):
    if combine:
        xbc_ref, dtr_ref, dtc_ref, pc_ref, pr_ref, yo_ref, dsk_ref, o_ref, s_scr = refs
    else:
        xbc_ref, dtr_ref, dtc_ref, pc_ref, pr_ref, o_ref, s_scr = refs
    t_len = xbc_ref.shape[0]

    @pl.when(pl.program_id(1) == 0)
    def _():
        s_scr[...] = jnp.zeros_like(s_scr)

    dt_rows = _softplus(dtr_ref[...] + pc_ref[:, 1:2])
    cs_rows = _cumsum_incl(-jnp.exp(pc_ref[:, 0:1]) * dt_rows, 1, reverse)
    dt_cols = _softplus(dtc_ref[...] + pr_ref[1:2, :])
    cs_cols = _cumsum_incl(-jnp.exp(pr_ref[0:1, :]) * dt_cols, 0, reverse)
    end = 0 if reverse else t_len - 1
    cs_end = cs_cols[end:end + 1, :]
    row_id = lax.broadcasted_iota(jnp.int32, (t_len, t_len), 0)
    col_id = lax.broadcasted_iota(jnp.int32, (t_len, t_len), 1)
    allowed = (col_id >= row_id) if reverse else (col_id <= row_id)
    for g in range(NG_C):
        bg = xbc_ref[:, W_C + g * N_C:W_C + (g + 1) * N_C].astype(BF16)
        cg = xbc_ref[:, W_C + NG_C * N_C + g * N_C:W_C + NG_C * N_C + (g + 1) * N_C].astype(BF16)
        cb = lax.dot_general(cg, bg, (((1,), (1,)), ((), ())), preferred_element_type=F32)
        s_old = s_scr[g]
        y_off = jnp.dot(cg, s_old.astype(BF16), preferred_element_type=F32)
        ys, xds, chunk_decay = [], [], []
        for hh in range(HPG_C):
            h = g * HPG_C + hh
            cs_c, cs_r = cs_cols[:, h:h + 1], cs_rows[h:h + 1, :]
            seg = jnp.exp(jnp.where(allowed, cs_c - cs_r, -jnp.inf))
            xv = xbc_ref[:, h * HD_C:(h + 1) * HD_C] * dt_cols[:, h:h + 1]
            y_diag = jnp.dot((cb * seg).astype(BF16), xv.astype(BF16), preferred_element_type=F32)
            ys.append(y_diag + y_off[:, hh * HD_C:(hh + 1) * HD_C] * jnp.exp(cs_c))
            xds.append(xv * jnp.exp(cs_end[:, h:h + 1] - cs_c))
            chunk_decay.append(jnp.broadcast_to(jnp.exp(cs_end[:, h:h + 1]), (1, HD_C)))
        xd = jnp.concatenate(xds, axis=1).astype(BF16)
        s_loc = lax.dot_general(bg, xd, (((0,), (0,)), ((), ())), preferred_element_type=F32)
        s_scr[g] = s_old * jnp.concatenate(chunk_decay, axis=1) + s_loc
        y = jnp.concatenate(ys, axis=1)
        sl = slice(g * GW_C, (g + 1) * GW_C)
        if combine:
            y = y + yo_ref[:, sl] + xbc_ref[:, sl] * dsk_ref[:, sl]
        o_ref[:, sl] = y


def ssd_scan(xbc, dt_rows, dt_cols, a_log, dt_bias, n_ctx, reverse, combine_args=None):
    bsz, n, _ = xbc.shape
    t_len = SCAN_CHUNK
    assert n % t_len == 0 and n_ctx % t_len == 0
    n_chunks, n_ctx_chunks = n // t_len, n_ctx // t_len
    order = functools.partial(_scan_chunk_index, n_ctx_chunks=n_ctx_chunks, n_chunks=n_chunks, reverse=reverse)
    prm_c = jnp.stack([a_log, dt_bias], axis=1).astype(F32)
    prm_r = jnp.stack([a_log, dt_bias], axis=0).astype(F32)
    y_spec = pl.BlockSpec((None, t_len, W_C), lambda b, s: (b, order(s), 0))
    in_specs = [pl.BlockSpec((None, t_len, XBC_W), lambda b, s: (b, order(s), 0)),
                pl.BlockSpec((None, H_C, t_len), lambda b, s: (b, 0, order(s))),
                pl.BlockSpec((None, t_len, H_C), lambda b, s: (b, order(s), 0)),
                pl.BlockSpec((H_C, 2), lambda b, s: (0, 0)),
                pl.BlockSpec((2, H_C), lambda b, s: (0, 0))]
    args = [xbc, dt_rows, dt_cols, prm_c, prm_r]
    if combine_args is not None:
        y_other, d_skip = combine_args
        in_specs += [y_spec, pl.BlockSpec((1, W_C), lambda b, s: (0, 0))]
        args += [y_other, jnp.repeat(d_skip.astype(F32), HD_C).reshape(1, W_C)]
    return pl.pallas_call(
        functools.partial(_ssd_kernel, reverse=reverse, combine=combine_args is not None),
        grid=(bsz, n_chunks),
        in_specs=in_specs,
        out_specs=y_spec,
        out_shape=jax.ShapeDtypeStruct((bsz, n, W_C), F32),
        scratch_shapes=[pltpu.VMEM((NG_C, N_C, GW_C), F32)],
        compiler_params=pltpu.CompilerParams(
            dimension_semantics=("arbitrary", "arbitrary"),
            vmem_limit_bytes=V7X_VMEM_LIMIT_BYTES),
    )(*args)


def ssd_mixer(zs_c, xbc_c, dt_c, zs_l, xbc_l, dt_l, rows, conv_w, conv_b, a_log, dt_bias, d_skip, norm_g):
    n_ctx = xbc_c.shape[1]
    xbc = jnp.concatenate([jax.nn.silu(dwconv(xbc_c, conv_w, conv_b)),
                           jax.nn.silu(dwconv(to_colmajor(xbc_l, rows), conv_w, conv_b))], axis=1).astype(F32)
    dt = jnp.concatenate([dt_c, to_colmajor(dt_l, rows)], axis=1).astype(F32)
    bsz, n, _ = dt.shape
    dt_cols = dt.reshape(bsz, n, 2, H_C).transpose(0, 2, 1, 3)
    dt_rows = dt.reshape(bsz, n, 2, H_C).transpose(0, 2, 3, 1)
    y_bwd = ssd_scan(xbc, dt_rows[:, 1], dt_cols[:, 1], a_log[1], dt_bias[1], n_ctx, reverse=True)
    y = ssd_scan(xbc, dt_rows[:, 0], dt_cols[:, 0], a_log[0], dt_bias[0], n_ctx, reverse=False,
                 combine_args=(y_bwd, d_skip))
    y_c, y_l = y[:, :n_ctx], from_colmajor(y[:, n_ctx:], rows)
    out_c = rmsnorm(y_c.astype(zs_c.dtype) * jax.nn.silu(zs_c), norm_g)
    out_l = rmsnorm(y_l.astype(zs_l.dtype) * jax.nn.silu(zs_l), norm_g)
    return out_c, out_l


def merge_branches(ys, gd, gate_w, gate_b, branch_w, w_out):
    terms = [jax.nn.sigmoid(gd @ gate_w[i] + gate_b[i]) * (ys[i].astype(gd.dtype) @ branch_w[i]) for i in range(N_BRANCH)]
    return (terms[0] + terms[1] + terms[2]) @ w_out


def _mlp_kernel(x_ref, g_ref, mod_ref, w1_ref, w2_ref, o_ref, h_ref):
    f = pl.program_id(1)

    @pl.when(f == 0)
    def _():
        x = x_ref[...]
        xn = x * lax.rsqrt(jnp.mean(x * x, axis=-1, keepdims=True) + EPS) * g_ref[...]
        h = xn * (1.0 + mod_ref[0, 4:5, :]) + mod_ref[0, 3:4, :]
        h_ref[...] = h.astype(BF16)
        o_ref[...] = jnp.zeros_like(o_ref)

    a = jnp.dot(h_ref[...], w1_ref[...], preferred_element_type=F32)
    a = jnp.square(jnp.maximum(a, 0.0)).astype(BF16)
    o_ref[...] += jnp.dot(a, w2_ref[...], preferred_element_type=F32)

    @pl.when(f == pl.num_programs(1) - 1)
    def _():
        o_ref[...] = x_ref[...] + mod_ref[0, 5:6, :] * o_ref[...]


def mlp_block(x, g, mod, w1, w2, rows_per_mod, tm=512, tf=512):
    m, d = x.shape
    dff = w1.shape[1]
    tm = min(tm, m)
    assert m % tm == 0 and rows_per_mod % tm == 0 and dff % tf == 0
    tiles_per_mod = rows_per_mod // tm
    single = pl.Buffered(1)
    return pl.pallas_call(
        _mlp_kernel,
        grid=(m // tm, dff // tf),
        in_specs=[
            pl.BlockSpec((tm, d), lambda i, f: (i, 0), pipeline_mode=single),
            pl.BlockSpec((1, d), lambda i, f: (0, 0), pipeline_mode=single),
            pl.BlockSpec((1, MOD_ROWS, d), lambda i, f: (i // tiles_per_mod, 0, 0), pipeline_mode=single),
            pl.BlockSpec((d, tf), lambda i, f: (0, f)),
            pl.BlockSpec((tf, d), lambda i, f: (f, 0)),
        ],
        out_specs=pl.BlockSpec((tm, d), lambda i, f: (i, 0), pipeline_mode=single),
        out_shape=jax.ShapeDtypeStruct((m, d), F32),
        scratch_shapes=[pltpu.VMEM((tm, d), BF16)],
        compiler_params=pltpu.CompilerParams(
            dimension_semantics=("arbitrary", "arbitrary"),
            vmem_limit_bytes=V7X_VMEM_LIMIT_BYTES),
    )(x, g.reshape(1, d), mod, w1, w2)


def _pad_mod(mod):
    nb = mod.shape[0]
    mod = mod.reshape(nb, N_MOD, D_MODEL)
    return jnp.pad(mod, ((0, 0), (0, MOD_ROWS - N_MOD), (0, 0)))


def kernel(x, c, ctx, c_ctx, norm1_g, norm2_g, final_g, mod_down, mod_up, mod_b, w_in,
           m_conv_w, m_conv_b, m_wq, m_wk, m_wv, m_gate_w, m_gate_b, m_norm_g, m_skip,
           s5_lam_re, s5_lam_im, s5_log_dt, s5_b_re, s5_b_im, s5_c_re, s5_c_im, s5_d, s5_glu_w, s5_glu_b,
           ssd_conv_w, ssd_conv_b, ssd_a_log, ssd_dt_bias, ssd_d, ssd_norm_g,
           gate_w, gate_b, branch_w, w_out, mlp_w1, mlp_w2):
    rows = x.shape[1] // GRID_W
    bsz = x.shape[0]
    x_lat, x_ctx = x, ctx
    w1_bf = mlp_w1.astype(BF16)
    w2_bf = mlp_w2.astype(BF16)
    for l in range(DEPTH):
        last = l == DEPTH - 1
        mod_l = adaln(c, mod_down[l], mod_up[l], mod_b[l])
        mod_c = adaln(c_ctx, mod_down[l], mod_up[l], mod_b[l])
        sh1_l, sc1_l, g1_l, sh2_l, sc2_l, g2_l = jnp.split(mod_l[:, None, :], N_MOD, axis=-1)
        sh1_c, sc1_c, g1_c, sh2_c, sc2_c, g2_c = jnp.split(mod_c, N_MOD, axis=-1)
        p_l = modulate(x_lat, norm1_g[l], sh1_l, sc1_l) @ w_in[l]
        p_c = modulate(x_ctx, norm1_g[l], sh1_c, sc1_c) @ w_in[l]
        xm_l, zm_l, u_l, zs_l, xbc_l, dt_l, gd_l = jnp.split(p_l, SPLIT_IDX, axis=-1)
        xm_c, zm_c, u_c, zs_c, xbc_c, dt_c, gd_c = jnp.split(p_c, SPLIT_IDX, axis=-1)
        ya_c, ya_l = mlstm_mixer(xm_c, zm_c, xm_l, zm_l, m_conv_w[l], m_conv_b[l], m_wq[l], m_wk[l], m_wv[l],
                                 m_gate_w[l], m_gate_b[l], m_norm_g[l], m_skip[l])
        yb_c, yb_l = s5_mixer(u_c, u_l, s5_lam_re[l], s5_lam_im[l], s5_log_dt[l], s5_b_re[l], s5_b_im[l],
                              s5_c_re[l], s5_c_im[l], s5_d[l], s5_glu_w[l], s5_glu_b[l])
        yc_c, yc_l = ssd_mixer(zs_c, xbc_c, dt_c, zs_l, xbc_l, dt_l, rows, ssd_conv_w[l], ssd_conv_b[l],
                               ssd_a_log[l], ssd_dt_bias[l], ssd_d[l], ssd_norm_g[l])
        x_lat = x_lat + g1_l * merge_branches((ya_l, yb_l, yc_l), gd_l, gate_w[l], gate_b[l], branch_w[l], w_out[l])
        x_lat = mlp_block(x_lat.reshape(bsz * SEQ, D_MODEL), norm2_g[l], _pad_mod(mod_l), w1_bf[l], w2_bf[l],
                          rows_per_mod=SEQ).reshape(bsz, SEQ, D_MODEL)
        if not last:
            x_ctx = x_ctx + g1_c * merge_branches((ya_c, yb_c, yc_c), gd_c, gate_w[l], gate_b[l], branch_w[l], w_out[l])
            x_ctx = mlp_block(x_ctx.reshape(bsz * CTX_LEN, D_MODEL), norm2_g[l], _pad_mod(mod_c[None]), w1_bf[l], w2_bf[l],
                              rows_per_mod=bsz * CTX_LEN).reshape(bsz, CTX_LEN, D_MODEL)
    return rmsnorm(x_lat, final_g)
```

```python
import functools
import math
import jax
import jax.numpy as jnp
from jax import lax
import numpy as np
from jax.experimental import pallas as pl
from jax.experimental.pallas import tpu as pltpu

D_MODEL = 4096
BATCH = 2
SEQ = 8192
DEPTH = 4

CTX_LEN = 256
GRID_W = 64
W_MIX = D_MODEL // 4
W_A = W_MIX
H_A = 8
DH_A = W_A // H_A
QKV_BLOCK = 4
MLSTM_CHUNK = 64
W_B = W_MIX
S5_GROUP = 16
G_B = W_B // S5_GROUP
P_B = 64
W_C = W_MIX
HD_C = 64
H_C = W_C // HD_C
NG_C = 4
N_C = 128
SSD_CHUNK = 128
CONV_W = 5
XBC_W = W_C + 2 * NG_C * N_C
D_FF = 4 * D_MODEL
R_MOD = 512
R_GATE = 512
N_BRANCH = 3
N_MOD = 6
SPLIT_IDX = (W_A, 2 * W_A, 2 * W_A + W_B, 2 * W_A + W_B + W_C, 2 * W_A + W_B + W_C + XBC_W, 2 * W_A + W_B + W_C + XBC_W + 2 * H_C)
IN_COLS = 2 * W_A + W_B + W_C + XBC_W + 2 * H_C + R_GATE
EPS = 1e-6
F32 = jnp.float32
BF16 = jnp.bfloat16

V7X_VMEM_LIMIT_BYTES = 56 * 1024 * 1024
MOD_ROWS = 8
LANES = 128


def rmsnorm(x, g):
    xf = x.astype(F32)
    xf = xf * lax.rsqrt(jnp.mean(xf * xf, axis=-1, keepdims=True) + EPS)
    return xf.astype(x.dtype) * g


def modulate(x, g, shift, scale):
    return rmsnorm(x, g) * (1 + scale) + shift


def adaln(cvec, down, up, b):
    return (jax.nn.silu(cvec) @ down) @ up + b


def dwconv(x, w, b):
    pad = CONV_W // 2
    y = lax.conv_general_dilated(x, w[:, None, :].astype(x.dtype), window_strides=(1,), padding=[(pad, pad)],
                                 dimension_numbers=('NWC', 'WIO', 'NWC'), feature_group_count=x.shape[-1])
    return y + b


def to_colmajor(t, rows):
    bsz, n, ch = t.shape
    return t.reshape(bsz, rows, GRID_W, ch).swapaxes(1, 2).reshape(bsz, n, ch)


def from_colmajor(t, rows):
    bsz, n, ch = t.shape
    return t.reshape(bsz, GRID_W, rows, ch).swapaxes(1, 2).reshape(bsz, n, ch)


def blockdiag(x, w):
    xb = x.reshape(*x.shape[:-1], -1, QKV_BLOCK)
    return jnp.einsum('blnc,ncd->blnd', xb, w).reshape(x.shape)


SCAN_CHUNK = 128


def _cumsum_incl(x, axis, reverse):
    n = x.shape[axis]
    idx = lax.broadcasted_iota(jnp.int32, x.shape, axis)
    s = 1
    while s < n:
        if reverse:
            x = x + jnp.where(idx < n - s, pltpu.roll(x, n - s, axis), 0.0)
        else:
            x = x + jnp.where(idx >= s, pltpu.roll(x, s, axis), 0.0)
        s *= 2
    return x


def _scan_chunk_index(s, n_ctx_chunks, n_chunks, reverse):
    if not reverse:
        return s
    return jnp.where(s < n_ctx_chunks, n_ctx_chunks - 1 - s, n_chunks - 1 - (s - n_ctx_chunks))


def _mlstm_kernel(*refs, reverse, readout, gate_col0):
    if readout:
        (q_ref, k_ref, v_ref, gr_ref, gc_ref, hb_ref, xc_ref, zm_ref, ng_ref, sk_ref, o_ref, c_scr, n_scr, m_scr) = refs
    else:
        (q_ref, k_ref, v_ref, gr_ref, gc_ref, o_ref, c_scr, n_scr, m_scr) = refs
    t_len = q_ref.shape[0]

    @pl.when(pl.program_id(1) == 0)
    def _():
        c_scr[...] = jnp.zeros_like(c_scr)
        n_scr[...] = jnp.zeros_like(n_scr)
        m_scr[...] = jnp.zeros_like(m_scr)

    b_rows = _cumsum_incl(gr_ref[H_A:, :], 1, reverse)
    b_cols = _cumsum_incl(gc_ref[:, gate_col0 + H_A:gate_col0 + 2 * H_A], 0, reverse)
    i_rows = gr_ref[:H_A, :]
    i_cols = gc_ref[:, gate_col0:gate_col0 + H_A]
    row_id = lax.broadcasted_iota(jnp.int32, (t_len, t_len), 0)
    col_id = lax.broadcasted_iota(jnp.int32, (t_len, t_len), 1)
    allowed = (col_id >= row_id) if reverse else (col_id <= row_id)
    end = 0 if reverse else t_len - 1
    m_all = m_scr[...]
    n_all = n_scr[...]
    for h in range(H_A):
        sl = slice(h * DH_A, (h + 1) * DH_A)
        q = q_ref[:, sl]
        k = k_ref[:, sl]
        v = v_ref[:, sl]
        b_row, b_col = b_rows[h:h + 1, :], b_cols[:, h:h + 1]
        i_row, i_col = i_rows[h:h + 1, :], i_cols[:, h:h + 1]
        b_end = b_row[:, end:end + 1]
        m_st = m_all[h:h + 1, 0:1]
        n_st = n_all[h:h + 1, :]
        c_st = c_scr[h]
        d_log = jnp.where(allowed, b_col - b_row + i_row, -jnp.inf)
        inter = b_col + m_st
        m_t = jnp.maximum(inter, jnp.max(d_log, axis=-1, keepdims=True))
        w_carry = jnp.exp(inter - m_t)
        qb, kb, vb = q.astype(BF16), k.astype(BF16), v.astype(BF16)
        qk = lax.dot_general(qb, kb, (((1,), (1,)), ((), ())), preferred_element_type=F32)
        s = qk * jnp.exp(d_log - m_t)
        num = (jnp.dot(s.astype(BF16), vb, preferred_element_type=F32)
               + w_carry * jnp.dot(qb, c_st.astype(BF16), preferred_element_type=F32))
        den = jnp.sum(s, axis=-1, keepdims=True) + w_carry * jnp.sum(q * n_st, axis=-1, keepdims=True)
        hh = num / jnp.maximum(jnp.abs(den), jnp.exp(-m_t))
        log_ws = b_end - b_col + i_col
        m_new = jnp.maximum(b_end + m_st, jnp.max(log_ws, axis=0, keepdims=True))
        kw = k * jnp.exp(log_ws - m_new)
        w_prev = jnp.exp(b_end + m_st - m_new)
        c_scr[h] = w_prev * c_st + lax.dot_general(kw.astype(BF16), vb, (((0,), (0,)), ((), ())),
                                                   preferred_element_type=F32)
        n_scr[h:h + 1, :] = w_prev * n_st + jnp.sum(kw, axis=0, keepdims=True)
        m_scr[h:h + 1, :] = jnp.broadcast_to(m_new, (1, DH_A))
        if readout:
            hh = hh + hb_ref[:, sl]
            mu = jnp.mean(hh, axis=-1, keepdims=True)
            var = jnp.mean(jnp.square(hh - mu), axis=-1, keepdims=True)
            hn = (hh - mu) * lax.rsqrt(var + EPS)
            zm = zm_ref[:, sl]
            o_ref[:, sl] = (hn * ng_ref[:, sl] + sk_ref[:, sl] * xc_ref[:, sl]) * (zm * jax.nn.sigmoid(zm))
        else:
            o_ref[:, sl] = hh


def mlstm_scan(q, k, v, g_rows, g_cols, lay, direction, readout_args=None):
    bsz, n_lat, n_ctx = lay
    w = q.shape[1]
    reverse = direction == 1
    t_len = SCAN_CHUNK
    assert n_lat % t_len == 0 and n_ctx % t_len == 0
    n_chunks, n_ctx_chunks = (n_lat + n_ctx) // t_len, n_ctx // t_len
    order = functools.partial(_scan_chunk_index, n_ctx_chunks=n_ctx_chunks, n_chunks=n_chunks, reverse=reverse)
    blk = lambda b, s: _seq_block(lay, b, order(s), t_len)
    tok = pl.BlockSpec((t_len, w), lambda b, s: (blk(b, s), 0))
    in_specs = [tok, tok, tok,
                pl.BlockSpec((2 * H_A, t_len), lambda b, s: (direction, blk(b, s))),
                pl.BlockSpec((t_len, 4 * H_A), lambda b, s: (blk(b, s), 0))]
    args = [q, k, v, g_rows, g_cols]
    if readout_args is not None:
        h_other, xc, p_all, norm_g, skip = readout_args
        vec = pl.BlockSpec((1, w), lambda b, s: (0, 0))
        in_specs += [tok, tok, pl.BlockSpec((t_len, w), lambda b, s: (blk(b, s), P_ZM // w)), vec, vec]
        args += [h_other, xc, p_all, norm_g.reshape(1, w).astype(F32), skip.reshape(1, w).astype(F32)]
    return pl.pallas_call(
        functools.partial(_mlstm_kernel, reverse=reverse, readout=readout_args is not None,
                          gate_col0=2 * H_A * direction),
        grid=(bsz, n_chunks),
        in_specs=in_specs,
        out_specs=tok,
        out_shape=jax.ShapeDtypeStruct(q.shape, F32),
        scratch_shapes=[pltpu.VMEM((H_A, DH_A, DH_A), F32), pltpu.VMEM((H_A, DH_A), F32), pltpu.VMEM((H_A, DH_A), F32)],
        compiler_params=pltpu.CompilerParams(
            dimension_semantics=("arbitrary", "arbitrary"),
            vmem_limit_bytes=V7X_VMEM_LIMIT_BYTES),
    )(*args)


FEAT_TILE = 256
HALO = 8
N_LB = W_A // LANES


def _seq_edges(lay, i, tile):
    bsz, n_lat, n_ctx = lay
    lat_tiles, ctx_tiles, base = n_lat // tile, n_ctx // tile, (bsz * n_lat) // tile
    pos = jnp.where(i < base, i % lat_tiles, (i - base) % ctx_tiles)
    n = jnp.where(i < base, lat_tiles, ctx_tiles)
    return pos == 0, pos == n - 1


def _dwconv_silu(prev_ref, cur_ref, next_ref, w_ref, b_ref, xp_ref, first, last):
    tile = cur_ref.shape[0]
    xp_ref[0:HALO, :] = jnp.where(first, 0.0, prev_ref[...])
    xp_ref[HALO:HALO + tile, :] = cur_ref[...]
    xp_ref[HALO + tile:, :] = jnp.where(last, 0.0, next_ref[...])
    acc = b_ref[...] + w_ref[0:1, :] * xp_ref[pl.ds(HALO - CONV_W // 2, tile), :]
    for j in range(1, CONV_W):
        acc = acc + w_ref[j:j + 1, :] * xp_ref[pl.ds(HALO - CONV_W // 2 + j, tile), :]
    return acc * jax.nn.sigmoid(acc)


def _split_dot(x, w_hi, w_lo):
    x_hi = x.astype(BF16)
    x_lo = (x - x_hi.astype(F32)).astype(BF16)
    return (jnp.dot(x_hi, w_hi, preferred_element_type=F32) + jnp.dot(x_lo, w_hi, preferred_element_type=F32)
            + jnp.dot(x_hi, w_lo, preferred_element_type=F32))


def _mlstm_feat_kernel(prev_ref, cur_ref, next_ref, cw_ref, cb_ref, wqk_hi_ref, wqk_lo_ref, wv_hi_ref, wv_lo_ref,
                       gw_ref, gwt_ref, gbc_ref, gbr_ref, xc_ref, q_ref, k_ref, v_ref, gc_ref, gr_ref,
                       xp_ref, qkv_ref, *, lay):
    tile = cur_ref.shape[0]
    first, last = _seq_edges(lay, pl.program_id(0), tile)
    xc = _dwconv_silu(prev_ref, cur_ref, next_ref, cw_ref, cb_ref, xp_ref, first, last)
    xc_ref[...] = xc
    for lb in range(N_LB):
        sl = slice(lb * LANES, (lb + 1) * LANES)
        qk = _split_dot(xc[:, sl], wqk_hi_ref[lb], wqk_lo_ref[lb])
        v = _split_dot(cur_ref[:, sl], wv_hi_ref[lb], wv_lo_ref[lb])
        q_ref[:, sl] = qk[:, :LANES] * DH_A ** -0.5
        k_ref[:, sl] = qk[:, LANES:]
        v_ref[:, sl] = v
        qkv_ref[:, sl] = qk[:, :LANES].astype(BF16)
        qkv_ref[:, W_A + lb * LANES:W_A + (lb + 1) * LANES] = qk[:, LANES:].astype(BF16)
        qkv_ref[:, 2 * W_A + lb * LANES:2 * W_A + (lb + 1) * LANES] = v.astype(BF16)
    qkv = qkv_ref[...]
    g_cols = jnp.dot(qkv, gw_ref[...], preferred_element_type=F32) + gbc_ref[...]
    g_rows = lax.dot_general(gwt_ref[...], qkv, (((1,), (1,)), ((), ())), preferred_element_type=F32) + gbr_ref[...]
    is_f_c = (lax.broadcasted_iota(jnp.int32, g_cols.shape, 1) % (2 * H_A)) >= H_A
    is_f_r = (lax.broadcasted_iota(jnp.int32, g_rows.shape, 0) % (2 * H_A)) >= H_A
    gc_ref[...] = jnp.where(is_f_c, -_softplus(-g_cols), g_cols)
    gr_ref[...] = jnp.where(is_f_r, -_softplus(-g_rows), g_rows)


def _blockdiag_dense(w):
    per = LANES // QKV_BLOCK
    w = w.astype(F32).reshape(N_LB, per, QKV_BLOCK, QKV_BLOCK)
    eye = jnp.eye(per, dtype=F32)
    return (w[:, :, :, None, :] * eye[None, :, None, :, None]).reshape(N_LB, LANES, LANES)


def _hi_lo(w):
    hi = w.astype(BF16)
    return hi, (w - hi.astype(F32)).astype(BF16)


def mlstm_features(p_all, lay, conv_w, conv_b, wq, wk, wv, gate_w, gate_b):
    rows = p_all.shape[0]
    tile = FEAT_TILE
    hb = tile // HALO
    n_halo_blocks = rows // HALO
    wqk = jnp.concatenate([_blockdiag_dense(wq), _blockdiag_dense(wk)], axis=2)
    wqk_hi, wqk_lo = _hi_lo(wqk)
    wv_hi, wv_lo = _hi_lo(_blockdiag_dense(wv))
    gw = gate_w.astype(F32).transpose(1, 0, 2).reshape(3 * W_A, 4 * H_A)
    gb = gate_b.astype(F32).reshape(1, 4 * H_A)
    cw = jnp.pad(conv_w.astype(F32), ((0, HALO - CONV_W), (0, 0)))
    tok = pl.BlockSpec((tile, W_A), lambda i: (i, 0))
    full = lambda *shape: pl.BlockSpec(shape, lambda i: (0,) * len(shape))
    return pl.pallas_call(
        functools.partial(_mlstm_feat_kernel, lay=lay),
        grid=(rows // tile,),
        in_specs=[pl.BlockSpec((HALO, W_A), lambda i: (jnp.maximum(i * hb - 1, 0), P_XM // W_A)),
                  pl.BlockSpec((tile, W_A), lambda i: (i, P_XM // W_A)),
                  pl.BlockSpec((HALO, W_A), lambda i: (jnp.minimum((i + 1) * hb, n_halo_blocks - 1), P_XM // W_A)),
                  full(HALO, W_A), full(1, W_A),
                  full(N_LB, LANES, 2 * LANES), full(N_LB, LANES, 2 * LANES),
                  full(N_LB, LANES, LANES), full(N_LB, LANES, LANES),
                  full(3 * W_A, 4 * H_A), full(4 * H_A, 3 * W_A), full(1, 4 * H_A), full(4 * H_A, 1)],
        out_specs=[tok, tok, tok, tok,
                   pl.BlockSpec((tile, 4 * H_A), lambda i: (i, 0)),
                   pl.BlockSpec((4 * H_A, tile), lambda i: (0, i))],
        out_shape=[jax.ShapeDtypeStruct((rows, W_A), F32)] * 4
        + [jax.ShapeDtypeStruct((rows, 4 * H_A), F32), jax.ShapeDtypeStruct((4 * H_A, rows), F32)],
        scratch_shapes=[pltpu.VMEM((tile + 2 * HALO, W_A), F32), pltpu.VMEM((tile, 3 * W_A), BF16)],
        compiler_params=pltpu.CompilerParams(dimension_semantics=("arbitrary",),
                                             vmem_limit_bytes=V7X_VMEM_LIMIT_BYTES),
    )(p_all, p_all, p_all, cw, conv_b.reshape(1, W_A).astype(F32), wqk_hi, wqk_lo, wv_hi, wv_lo,
      gw.astype(BF16), gw.T.astype(BF16), gb, gb.reshape(4 * H_A, 1))


def mlstm_mixer(p_all, lay, conv_w, conv_b, wq, wk, wv, gate_w, gate_b, norm_g, skip):
    xc, q, k, v, g_cols, g_rows = mlstm_features(p_all, lay, conv_w, conv_b, wq, wk, wv, gate_w, gate_b)
    h_bwd = mlstm_scan(q, k, v, g_rows, g_cols, lay, 1)
    return mlstm_scan(q, k, v, g_rows, g_cols, lay, 0, readout_args=(h_bwd, xc, p_all, norm_g, skip))


def s5_glu(y, glu_w, glu_b, dtype):
    y = jax.nn.gelu(y)
    return (y * jax.nn.sigmoid(y @ glu_w.astype(F32) + glu_b.astype(F32))).astype(dtype)


S5_T = 8
S5_GPB = LANES // S5_GROUP
S5_NB = W_B // LANES
S5_NS = S5_GPB * P_B
S5_CW = S5_T * LANES


def _s5_tables(lam_re, lam_im, log_dt, b_re, b_im, c_re, c_im, n_rows):
    lam = lax.complex(lam_re.astype(F32), lam_im.astype(F32))
    ldt = lam * jnp.exp(log_dt.astype(F32))[..., None]
    lam_bar = jnp.exp(ldt)
    b_bar = ((lam_bar - 1) / lam)[..., None] * lax.complex(b_re.astype(F32), b_im.astype(F32))
    c_mat = lax.complex(c_re.astype(F32), c_im.astype(F32))
    steps = jnp.arange(S5_T + 1, dtype=F32)
    pw = jnp.exp(steps[None, :, None, None] * ldt[:, None])
    eye = jnp.eye(S5_GPB, dtype=F32)

    def blocks(t):
        return t.reshape(S5_NB, S5_GPB, *t.shape[1:])

    kk = jnp.real(jnp.einsum('dgcp,djgp,dgpe->djgce', c_mat, pw[:, :S5_T], b_bar))
    tt = jnp.arange(S5_T)
    diff = tt[None, :] - tt[:, None]
    kf = kk[0][jnp.clip(diff, 0, S5_T - 1)] * (diff >= 0)[..., None, None, None].astype(F32)
    kb = kk[1][jnp.clip(-diff, 0, S5_T - 1)] * (diff <= 0)[..., None, None, None].astype(F32)
    kt = (kf + kb).transpose(2, 0, 4, 1, 3)
    kt = blocks(kt).transpose(0, 2, 1, 3, 4, 5)
    m_toep = kt[:, :, :, :, :, None, :] * eye[None, None, :, None, None, :, None]
    m_toep = m_toep.reshape(S5_NB, S5_CW, S5_CW)

    def w_in(d, powers):
        w = powers[:, :, :, None] * b_bar[d][None]
        w = jnp.stack([jnp.real(w), jnp.imag(w)], axis=0)
        w = w.transpose(2, 1, 4, 0, 3)
        w = blocks(w).transpose(0, 2, 1, 3, 4, 5)
        w = w[:, :, :, :, :, None, :] * eye[None, None, :, None, None, :, None]
        return w.reshape(S5_NB, S5_CW, 2 * S5_NS)

    def w_out(d, powers):
        w = c_mat[d][None] * powers[:, :, None, :]
        w = jnp.stack([jnp.real(w), -jnp.imag(w)], axis=0)
        w = w.transpose(2, 0, 4, 1, 3)
        w = blocks(w).transpose(0, 2, 1, 3, 4, 5)
        w = w[:, :, :, :, :, None, :] * eye[None, None, :, None, None, :, None]
        return w.reshape(S5_NB, 2 * S5_NS, S5_CW)

    we_f = w_in(0, pw[0, S5_T - 1::-1][:S5_T])
    we_b = w_in(1, pw[1, :S5_T])
    wo_f = w_out(0, pw[0, 1:])
    wo_b = w_out(1, pw[1, S5_T:0:-1])

    def rows(t):
        n = t.shape[0]
        t = t.reshape(n, S5_NB, S5_NS).transpose(1, 0, 2)
        return jnp.stack([jnp.real(t), jnp.imag(t)], axis=1)

    n_lv = max(1, int(math.log2(n_rows)))
    lv_steps = S5_T * (2.0 ** jnp.arange(n_lv, dtype=F32))
    jj = jnp.arange(n_rows, dtype=F32)
    tabs = []
    for d in range(2):
        lv = jnp.exp(lv_steps[:, None, None] * ldt[d][None])
        expo = jj if d == 0 else (n_rows - 1 - jj)
        pt = jnp.exp((S5_T * expo)[:, None, None] * ldt[d][None])
        tabs.append((rows(lv), rows(pt)))
    return dict(m_toep=m_toep.astype(BF16), we_f=we_f.astype(BF16), we_b=we_b.astype(BF16),
                wo_f=wo_f.astype(BF16), wo_b=wo_b.astype(BF16),
                lv_f=tabs[0][0], pt_f=tabs[0][1], lv_b=tabs[1][0], pt_b=tabs[1][1])


def _s5_state_scan(e, lv_ref, pt_ref, h0, reverse):
    n_rows = e.shape[0]
    row = lax.broadcasted_iota(jnp.int32, (n_rows, S5_NS), 0)

    def shift(x, s):
        if reverse:
            return jnp.where(row < n_rows - s, pltpu.roll(x, n_rows - s, 0), 0.0)
        return jnp.where(row >= s, pltpu.roll(x, s, 0), 0.0)

    e_re, e_im = e[:, :S5_NS], e[:, S5_NS:]
    z_re, z_im = shift(e_re, 1), shift(e_im, 1)
    k = 0
    while (1 << k) < n_rows:
        a_re, a_im = lv_ref[0, 0, k:k + 1, :], lv_ref[0, 1, k:k + 1, :]
        s_re, s_im = shift(z_re, 1 << k), shift(z_im, 1 << k)
        z_re, z_im = z_re + (a_re * s_re - a_im * s_im), z_im + (a_re * s_im + a_im * s_re)
        k += 1
    p_re, p_im = pt_ref[0, 0], pt_ref[0, 1]
    h_re = z_re + (p_re * h0[0:1, :] - p_im * h0[1:2, :])
    h_im = z_im + (p_re * h0[1:2, :] + p_im * h0[0:1, :])
    last = 0 if reverse else n_rows - 1
    a_re, a_im = lv_ref[0, 0, 0:1, :], lv_ref[0, 1, 0:1, :]
    l_re, l_im = h_re[last:last + 1, :], h_im[last:last + 1, :]
    o_re = a_re * l_re - a_im * l_im + e_re[last:last + 1, :]
    o_im = a_re * l_im + a_im * l_re + e_im[last:last + 1, :]
    return jnp.concatenate([h_re, h_im], axis=1), jnp.concatenate([o_re, o_im], axis=0)


def _s5_kernel(*refs, n_pass_through):
    (uf_ref, ub_ref, mt_ref, wef_ref, web_ref, wof_ref, wob_ref, lvf_ref, ptf_ref, lvb_ref, ptb_ref,
     d_ref, h0f_ref, h0b_ref) = refs[:14]
    y1_ref, y2_ref, hff_ref, hfb_ref, sf_ref, sb_ref = refs[14 + n_pass_through:]
    i = pl.program_id(2)
    n_rows = uf_ref.shape[0] // S5_T

    @pl.when(i == 0)
    def _():
        sf_ref[...] = h0f_ref[...]
        sb_ref[...] = h0b_ref[...]

    def chunk_rows(ref):
        return jnp.concatenate([ref[pl.ds(t, n_rows, stride=S5_T), :] for t in range(S5_T)], axis=1)

    uf = chunk_rows(uf_ref)
    ub = chunk_rows(ub_ref)
    uf_bf = uf.astype(BF16)
    y_local = jnp.dot(uf_bf, mt_ref[0], preferred_element_type=F32)
    e_f = jnp.dot(uf_bf, wef_ref[0], preferred_element_type=F32)
    e_b = jnp.dot(ub.astype(BF16), web_ref[0], preferred_element_type=F32)
    h_f, out_f = _s5_state_scan(e_f, lvf_ref, ptf_ref, sf_ref[...], reverse=False)
    h_b, out_b = _s5_state_scan(e_b, lvb_ref, ptb_ref, sb_ref[...], reverse=True)
    sf_ref[...] = out_f
    sb_ref[...] = out_b
    y1 = y_local + jnp.dot(h_f.astype(BF16), wof_ref[0], preferred_element_type=F32)
    y2 = jnp.dot(h_b.astype(BF16), wob_ref[0], preferred_element_type=F32)
    d = d_ref[...]
    for t in range(S5_T):
        cols = slice(t * LANES, (t + 1) * LANES)
        y1_ref[pl.ds(t, n_rows, stride=S5_T), :] = y1[:, cols] + d * uf[:, cols]
        y2_ref[pl.ds(t, n_rows, stride=S5_T), :] = y2[:, cols]

    @pl.when(i == pl.num_programs(2) - 1)
    def _():
        hff_ref[...] = out_f
        hfb_ref[...] = out_b


def s5_scan(u, col0, row0, bsz, n_tiles, tile, tabs, d_skip, h0_f, h0_b, y_prev=None):
    n_rows = tile // S5_T
    assert tile % S5_T == 0 and tabs['pt_f'].shape[2] == n_rows
    n_lv = tabs['lv_f'].shape[2]
    u_spec_f = pl.BlockSpec((tile, LANES), lambda b, n_, i: (row0(n_) + i, col0 + b))
    u_spec_b = pl.BlockSpec((tile, LANES), lambda b, n_, i: (row0(n_) + n_tiles - 1 - i, col0 + b))
    y_spec_f = pl.BlockSpec((tile, LANES), lambda b, n_, i: (row0(n_) + i, b))
    y_spec_b = pl.BlockSpec((tile, LANES), lambda b, n_, i: (row0(n_) + n_tiles - 1 - i, b))
    w_spec = pl.BlockSpec((1, S5_CW, S5_CW), lambda b, n_, i: (b, 0, 0))
    lv_spec = pl.BlockSpec((1, 2, n_lv, S5_NS), lambda b, n_, i: (b, 0, 0, 0))
    pt_spec = pl.BlockSpec((1, 2, n_rows, S5_NS), lambda b, n_, i: (b, 0, 0, 0))
    st_spec = pl.BlockSpec((None, None, 2, S5_NS), lambda b, n_, i: (n_, b, 0, 0))
    in_specs = [u_spec_f, u_spec_b, w_spec, w_spec, w_spec, w_spec, w_spec, lv_spec, pt_spec, lv_spec, pt_spec,
                pl.BlockSpec((1, LANES), lambda b, n_, i: (0, b)), st_spec, st_spec]
    args = [u, u, tabs['m_toep'], tabs['we_f'], tabs['we_b'], tabs['wo_f'], tabs['wo_b'],
            tabs['lv_f'], tabs['pt_f'], tabs['lv_b'], tabs['pt_b'], d_skip.reshape(1, W_B).astype(F32), h0_f, h0_b]
    aliases = {}
    if y_prev is not None:
        aliases = {len(args): 0, len(args) + 1: 1}
        in_specs += [pl.BlockSpec(memory_space=pl.ANY), pl.BlockSpec(memory_space=pl.ANY)]
        args += list(y_prev)
    y_shape = jax.ShapeDtypeStruct((u.shape[0], W_B), F32)
    st_shape = jax.ShapeDtypeStruct((bsz, S5_NB, 2, S5_NS), F32)
    return pl.pallas_call(
        functools.partial(_s5_kernel, n_pass_through=len(aliases)),
        grid=(S5_NB, bsz, n_tiles),
        in_specs=in_specs,
        out_specs=[y_spec_f, y_spec_b, st_spec, st_spec],
        out_shape=[y_shape, y_shape, st_shape, st_shape],
        input_output_aliases=aliases,
        scratch_shapes=[pltpu.VMEM((2, S5_NS), F32), pltpu.VMEM((2, S5_NS), F32)],
        compiler_params=pltpu.CompilerParams(
            dimension_semantics=("arbitrary", "arbitrary", "arbitrary"),
            vmem_limit_bytes=V7X_VMEM_LIMIT_BYTES),
    )(*args)


def s5_mixer(u, col0, lay, lam_re, lam_im, log_dt, b_re, b_im, c_re, c_im, d_skip, tile_l=1024):
    bsz, n_lat, n_ctx = lay
    tile_l = min(tile_l, n_lat)
    assert n_lat % tile_l == 0 and (bsz * n_lat) % n_ctx == 0
    prm = (lam_re, lam_im, log_dt, b_re, b_im, c_re, c_im)
    tabs_c = _s5_tables(*prm, n_rows=n_ctx // S5_T)
    tabs_l = _s5_tables(*prm, n_rows=tile_l // S5_T)
    zero = jnp.zeros((bsz, S5_NB, 2, S5_NS), F32)
    y1, y2, hf, hb = s5_scan(u, col0, lambda n_: _ctx_row(lay, n_) // n_ctx, bsz, 1, n_ctx, tabs_c, d_skip, zero, zero)
    y1, y2, _, _ = s5_scan(u, col0, lambda n_: _lat_row(lay, n_) // tile_l, bsz, n_lat // tile_l, tile_l, tabs_l,
                           d_skip, hf, hb, y_prev=(y1, y2))
    return y1, y2


HPG_C = H_C // NG_C
GW_C = HPG_C * HD_C


def _softplus(x):
    return jnp.maximum(x, 0.0) + jnp.log1p(jnp.exp(-jnp.abs(x)))


def _ssd_kernel(*refs, reverse, combine):
    if combine:
        xbc_ref, dtr_ref, dtc_ref, pc_ref, pr_ref, yo_ref, dsk_ref, o_ref, s_scr = refs
    else:
        xbc_ref, dtr_ref, dtc_ref, pc_ref, pr_ref, o_ref, s_scr = refs
    t_len = xbc_ref.shape[0]

    @pl.when(pl.program_id(1) == 0)
    def _():
        s_scr[...] = jnp.zeros_like(s_scr)

    dt_rows = _softplus(dtr_ref[...] + pc_ref[:, 1:2])
    cs_rows = _cumsum_incl(-jnp.exp(pc_ref[:, 0:1]) * dt_rows, 1, reverse)
    dt_cols = _softplus(dtc_ref[...] + pr_ref[1:2, :])
    cs_cols = _cumsum_incl(-jnp.exp(pr_ref[0:1, :]) * dt_cols, 0, reverse)
    end = 0 if reverse else t_len - 1
    cs_end = cs_cols[end:end + 1, :]
    row_id = lax.broadcasted_iota(jnp.int32, (t_len, t_len), 0)
    col_id = lax.broadcasted_iota(jnp.int32, (t_len, t_len), 1)
    allowed = (col_id >= row_id) if reverse else (col_id <= row_id)
    for g in range(NG_C):
        bg = xbc_ref[:, W_C + g * N_C:W_C + (g + 1) * N_C].astype(BF16)
        cg = xbc_ref[:, W_C + NG_C * N_C + g * N_C:W_C + NG_C * N_C + (g + 1) * N_C].astype(BF16)
        cb = lax.dot_general(cg, bg, (((1,), (1,)), ((), ())), preferred_element_type=F32)
        s_old = s_scr[g]
        y_off = jnp.dot(cg, s_old.astype(BF16), preferred_element_type=F32)
        ys, xds, chunk_decay = [], [], []
        for hh in range(HPG_C):
            h = g * HPG_C + hh
            cs_c, cs_r = cs_cols[:, h:h + 1], cs_rows[h:h + 1, :]
            seg = jnp.exp(jnp.where(allowed, cs_c - cs_r, -jnp.inf))
            xv = xbc_ref[:, h * HD_C:(h + 1) * HD_C] * dt_cols[:, h:h + 1]
            y_diag = jnp.dot((cb * seg).astype(BF16), xv.astype(BF16), preferred_element_type=F32)
            ys.append(y_diag + y_off[:, hh * HD_C:(hh + 1) * HD_C] * jnp.exp(cs_c))
            xds.append(xv * jnp.exp(cs_end[:, h:h + 1] - cs_c))
            chunk_decay.append(jnp.broadcast_to(jnp.exp(cs_end[:, h:h + 1]), (1, HD_C)))
        xd = jnp.concatenate(xds, axis=1).astype(BF16)
        s_loc = lax.dot_general(bg, xd, (((0,), (0,)), ((), ())), preferred_element_type=F32)
        s_scr[g] = s_old * jnp.concatenate(chunk_decay, axis=1) + s_loc
        y = jnp.concatenate(ys, axis=1)
        sl = slice(g * GW_C, (g + 1) * GW_C)
        if combine:
            y = y + yo_ref[:, sl] + xbc_ref[:, sl] * dsk_ref[:, sl]
        o_ref[:, sl] = y


def ssd_scan(xbc, dt_rows, dt_cols, a_log, dt_bias, n_ctx, reverse, combine_args=None):
    bsz, n, _ = xbc.shape
    t_len = SCAN_CHUNK
    assert n % t_len == 0 and n_ctx % t_len == 0
    n_chunks, n_ctx_chunks = n // t_len, n_ctx // t_len
    order = functools.partial(_scan_chunk_index, n_ctx_chunks=n_ctx_chunks, n_chunks=n_chunks, reverse=reverse)
    prm_c = jnp.stack([a_log, dt_bias], axis=1).astype(F32)
    prm_r = jnp.stack([a_log, dt_bias], axis=0).astype(F32)
    y_spec = pl.BlockSpec((None, t_len, W_C), lambda b, s: (b, order(s), 0))
    in_specs = [pl.BlockSpec((None, t_len, XBC_W), lambda b, s: (b, order(s), 0)),
                pl.BlockSpec((None, H_C, t_len), lambda b, s: (b, 0, order(s))),
                pl.BlockSpec((None, t_len, H_C), lambda b, s: (b, order(s), 0)),
                pl.BlockSpec((H_C, 2), lambda b, s: (0, 0)),
                pl.BlockSpec((2, H_C), lambda b, s: (0, 0))]
    args = [xbc, dt_rows, dt_cols, prm_c, prm_r]
    if combine_args is not None:
        y_other, d_skip = combine_args
        in_specs += [y_spec, pl.BlockSpec((1, W_C), lambda b, s: (0, 0))]
        args += [y_other, jnp.repeat(d_skip.astype(F32), HD_C).reshape(1, W_C)]
    return pl.pallas_call(
        functools.partial(_ssd_kernel, reverse=reverse, combine=combine_args is not None),
        grid=(bsz, n_chunks),
        in_specs=in_specs,
        out_specs=y_spec,
        out_shape=jax.ShapeDtypeStruct((bsz, n, W_C), F32),
        scratch_shapes=[pltpu.VMEM((NG_C, N_C, GW_C), F32)],
        compiler_params=pltpu.CompilerParams(
            dimension_semantics=("arbitrary", "arbitrary"),
            vmem_limit_bytes=V7X_VMEM_LIMIT_BYTES),
    )(*args)


def ssd_mixer_rows(p_all, lay, rows, conv_w, conv_b, a_log, dt_bias, d_skip, norm_g):
    bsz, n_lat, n_ctx = lay
    lat, ctx = p_all[:bsz * n_lat], p_all[bsz * n_lat:]

    def cols(t, n, c0, w):
        return t[:, c0:c0 + w].reshape(bsz, n, w)

    out_c, out_l = ssd_mixer(cols(ctx, n_ctx, P_ZS, W_C), cols(ctx, n_ctx, P_XBC, XBC_W), cols(ctx, n_ctx, P_DT, 2 * H_C),
                             cols(lat, n_lat, P_ZS, W_C), cols(lat, n_lat, P_XBC, XBC_W), cols(lat, n_lat, P_DT, 2 * H_C),
                             rows, conv_w, conv_b, a_log, dt_bias, d_skip, norm_g)
    return jnp.concatenate([out_l.reshape(bsz * n_lat, W_C), out_c.reshape(bsz * n_ctx, W_C)], axis=0)


def ssd_mixer(zs_c, xbc_c, dt_c, zs_l, xbc_l, dt_l, rows, conv_w, conv_b, a_log, dt_bias, d_skip, norm_g):
    n_ctx = xbc_c.shape[1]
    xbc = jnp.concatenate([jax.nn.silu(dwconv(xbc_c, conv_w, conv_b)),
                           jax.nn.silu(dwconv(to_colmajor(xbc_l, rows), conv_w, conv_b))], axis=1).astype(F32)
    dt = jnp.concatenate([dt_c, to_colmajor(dt_l, rows)], axis=1).astype(F32)
    bsz, n, _ = dt.shape
    dt_cols = dt.reshape(bsz, n, 2, H_C).transpose(0, 2, 1, 3)
    dt_rows = dt.reshape(bsz, n, 2, H_C).transpose(0, 2, 3, 1)
    y_bwd = ssd_scan(xbc, dt_rows[:, 1], dt_cols[:, 1], a_log[1], dt_bias[1], n_ctx, reverse=True)
    y = ssd_scan(xbc, dt_rows[:, 0], dt_cols[:, 0], a_log[0], dt_bias[0], n_ctx, reverse=False,
                 combine_args=(y_bwd, d_skip))
    y_c, y_l = y[:, :n_ctx], from_colmajor(y[:, n_ctx:], rows)
    out_c = rmsnorm(y_c.astype(zs_c.dtype) * jax.nn.silu(zs_c), norm_g)
    out_l = rmsnorm(y_l.astype(zs_l.dtype) * jax.nn.silu(zs_l), norm_g)
    return out_c, out_l


TRUNK_TM = 512
P_COLS = 7168
P_XM, P_ZM, P_U, P_ZS, P_XBC, P_GD, P_DT = 0, W_A, 2 * W_A, 2 * W_A + W_B, 2 * W_A + W_B + W_C, 6144, 6656


def _lat_row(lay, b):
    bsz, n_lat, n_ctx = lay
    return b * n_lat


def _ctx_row(lay, b):
    bsz, n_lat, n_ctx = lay
    return bsz * n_lat + b * n_ctx


def _seq_block(lay, b, j, t_len):
    bsz, n_lat, n_ctx = lay
    n_cb = n_ctx // t_len
    return jnp.where(j < n_cb, (bsz * n_lat + b * n_ctx) // t_len + j, (b * n_lat) // t_len + (j - n_cb))


def _mod_index(lay, tm):
    bsz, n_lat, n_ctx = lay
    assert n_lat % tm == 0 and (bsz * n_ctx) % tm == 0
    return lambda i: jnp.where(i < (bsz * n_lat) // tm, 1 + i // (n_lat // tm), 0)


def _permute_w_in(w):
    xm_zm_u_zs_xbc, dt, gd = w[:, :SPLIT_IDX[4]], w[:, SPLIT_IDX[4]:SPLIT_IDX[5]], w[:, SPLIT_IDX[5]:]
    pad = jnp.zeros((w.shape[0], P_COLS - IN_COLS), w.dtype)
    return jnp.concatenate([xm_zm_u_zs_xbc, gd, dt, pad], axis=1).astype(BF16)


def _modulated_norm(x, g, shift, scale):
    xn = x * lax.rsqrt(jnp.mean(x * x, axis=-1, keepdims=True) + EPS) * g
    return xn * (1.0 + scale) + shift


def _win_kernel(x_ref, g_ref, mod_ref, w_ref, o_ref, h_ref):
    @pl.when(pl.program_id(1) == 0)
    def _():
        h_ref[...] = _modulated_norm(x_ref[...], g_ref[...], mod_ref[0, 0:1, :], mod_ref[0, 1:2, :]).astype(BF16)

    o_ref[...] = jnp.dot(h_ref[...], w_ref[...], preferred_element_type=F32)


def in_projection(x, g, mods, w_p, lay, tn=512):
    m, d = x.shape
    tm = TRUNK_TM
    mod_idx = _mod_index(lay, tm)
    single = pl.Buffered(1)
    return pl.pallas_call(
        _win_kernel,
        grid=(m // tm, P_COLS // tn),
        in_specs=[pl.BlockSpec((tm, d), lambda i, j: (i, 0), pipeline_mode=single),
                  pl.BlockSpec((1, d), lambda i, j: (0, 0), pipeline_mode=single),
                  pl.BlockSpec((1, MOD_ROWS, d), lambda i, j: (mod_idx(i), 0, 0), pipeline_mode=single),
                  pl.BlockSpec((d, tn), lambda i, j: (0, j))],
        out_specs=pl.BlockSpec((tm, tn), lambda i, j: (i, j)),
        out_shape=jax.ShapeDtypeStruct((m, P_COLS), F32),
        scratch_shapes=[pltpu.VMEM((tm, d), BF16)],
        compiler_params=pltpu.CompilerParams(dimension_semantics=("arbitrary", "arbitrary"),
                                             vmem_limit_bytes=V7X_VMEM_LIMIT_BYTES),
    )(x, g.reshape(1, d).astype(F32), mods, w_p)


def _merge_kernel(gd_ref, ya_ref, y1_ref, y2_ref, yc_ref, gluw_ref, glub_ref, gw_ref, gb_ref, bw_ref, o_ref,
                  gds_ref, ys_ref):
    @pl.when(pl.program_id(1) == 0)
    def _():
        gds_ref[...] = gd_ref[...].astype(BF16)
        ys_ref[0] = ya_ref[...].astype(BF16)
        yb = jax.nn.gelu(y1_ref[...] + y2_ref[...])
        gate = jnp.dot(yb.astype(BF16), gluw_ref[...], preferred_element_type=F32) + glub_ref[...]
        ys_ref[1] = (yb * jax.nn.sigmoid(gate)).astype(BF16)
        ys_ref[2] = yc_ref[...].astype(BF16)

    acc = None
    for i in range(N_BRANCH):
        gate = jax.nn.sigmoid(jnp.dot(gds_ref[...], gw_ref[i], preferred_element_type=F32) + gb_ref[i])
        term = gate * jnp.dot(ys_ref[i], bw_ref[i], preferred_element_type=F32)
        acc = term if acc is None else acc + term
    o_ref[...] = acc.astype(o_ref.dtype)


def merge_projection(p_all, ya, y1, y2, yc, glu_w, glu_b, gate_w, gate_b, branch_w, tn=512):
    m = p_all.shape[0]
    d = gate_w.shape[-1]
    tm = TRUNK_TM
    single = pl.Buffered(1)
    row = lambda w: pl.BlockSpec((tm, w), lambda i, j: (i, 0), pipeline_mode=single)
    return pl.pallas_call(
        _merge_kernel,
        grid=(m // tm, d // tn),
        in_specs=[pl.BlockSpec((tm, R_GATE), lambda i, j: (i, P_GD // R_GATE), pipeline_mode=single),
                  row(W_A), row(W_B), row(W_B), row(W_C),
                  pl.BlockSpec((W_B, W_B), lambda i, j: (0, 0), pipeline_mode=single),
                  pl.BlockSpec((1, W_B), lambda i, j: (0, 0), pipeline_mode=single),
                  pl.BlockSpec((N_BRANCH, R_GATE, tn), lambda i, j: (0, 0, j)),
                  pl.BlockSpec((N_BRANCH, 1, tn), lambda i, j: (0, 0, j)),
                  pl.BlockSpec((N_BRANCH, W_MIX, tn), lambda i, j: (0, 0, j))],
        out_specs=pl.BlockSpec((tm, tn), lambda i, j: (i, j)),
        out_shape=jax.ShapeDtypeStruct((m, d), BF16),
        scratch_shapes=[pltpu.VMEM((tm, R_GATE), BF16), pltpu.VMEM((N_BRANCH, tm, W_MIX), BF16)],
        compiler_params=pltpu.CompilerParams(dimension_semantics=("arbitrary", "arbitrary"),
                                             vmem_limit_bytes=V7X_VMEM_LIMIT_BYTES),
    )(p_all, ya, y1, y2, yc, glu_w.astype(BF16), glu_b.reshape(1, W_B).astype(F32),
      gate_w.astype(BF16), gate_b.reshape(N_BRANCH, 1, d).astype(F32), branch_w.astype(BF16))


def _wout_kernel(t_ref, w_ref, x_ref, mod_ref, o_ref):
    y = jnp.dot(t_ref[...], w_ref[...], preferred_element_type=F32)
    o_ref[...] = x_ref[...] + mod_ref[0, 2:3, :] * y


def out_projection(t, w_out_bf, x, mods, lay, tn=512):
    m, d = x.shape
    tm = TRUNK_TM
    mod_idx = _mod_index(lay, tm)
    single = pl.Buffered(1)
    return pl.pallas_call(
        _wout_kernel,
        grid=(m // tm, d // tn),
        in_specs=[pl.BlockSpec((tm, d), lambda i, j: (i, 0), pipeline_mode=single),
                  pl.BlockSpec((d, tn), lambda i, j: (0, j)),
                  pl.BlockSpec((tm, tn), lambda i, j: (i, j)),
                  pl.BlockSpec((1, MOD_ROWS, tn), lambda i, j: (mod_idx(i), 0, j))],
        out_specs=pl.BlockSpec((tm, tn), lambda i, j: (i, j)),
        out_shape=jax.ShapeDtypeStruct((m, d), F32),
        compiler_params=pltpu.CompilerParams(dimension_semantics=("arbitrary", "arbitrary"),
                                             vmem_limit_bytes=V7X_VMEM_LIMIT_BYTES),
    )(t, w_out_bf, x, mods)


def _mlp_kernel(x_ref, g_ref, mod_ref, w1_ref, w2_ref, o_ref, h_ref):
    f = pl.program_id(1)

    @pl.when(f == 0)
    def _():
        x = x_ref[...]
        xn = x * lax.rsqrt(jnp.mean(x * x, axis=-1, keepdims=True) + EPS) * g_ref[...]
        h = xn * (1.0 + mod_ref[0, 4:5, :]) + mod_ref[0, 3:4, :]
        h_ref[...] = h.astype(BF16)
        o_ref[...] = jnp.zeros_like(o_ref)

    a = jnp.dot(h_ref[...], w1_ref[...], preferred_element_type=F32)
    a = jnp.square(jnp.maximum(a, 0.0)).astype(BF16)
    o_ref[...] += jnp.dot(a, w2_ref[...], preferred_element_type=F32)

    @pl.when(f == pl.num_programs(1) - 1)
    def _():
        o_ref[...] = x_ref[...] + mod_ref[0, 5:6, :] * o_ref[...]


def mlp_block(x, g, mod, w1, w2, lay, tf=512):
    m, d = x.shape
    dff = w1.shape[1]
    tm = TRUNK_TM
    assert m % tm == 0 and dff % tf == 0
    mod_idx = _mod_index(lay, tm)
    single = pl.Buffered(1)
    return pl.pallas_call(
        _mlp_kernel,
        grid=(m // tm, dff // tf),
        in_specs=[
            pl.BlockSpec((tm, d), lambda i, f: (i, 0), pipeline_mode=single),
            pl.BlockSpec((1, d), lambda i, f: (0, 0), pipeline_mode=single),
            pl.BlockSpec((1, MOD_ROWS, d), lambda i, f: (mod_idx(i), 0, 0), pipeline_mode=single),
            pl.BlockSpec((d, tf), lambda i, f: (0, f)),
            pl.BlockSpec((tf, d), lambda i, f: (f, 0)),
        ],
        out_specs=pl.BlockSpec((tm, d), lambda i, f: (i, 0), pipeline_mode=single),
        out_shape=jax.ShapeDtypeStruct((m, d), F32),
        scratch_shapes=[pltpu.VMEM((tm, d), BF16)],
        compiler_params=pltpu.CompilerParams(
            dimension_semantics=("arbitrary", "arbitrary"),
            vmem_limit_bytes=V7X_VMEM_LIMIT_BYTES),
    )(x, g.reshape(1, d), mod, w1, w2)


def _pad_mod(mod):
    nb = mod.shape[0]
    mod = mod.reshape(nb, N_MOD, D_MODEL)
    return jnp.pad(mod, ((0, 0), (0, MOD_ROWS - N_MOD), (0, 0)))


def kernel(x, c, ctx, c_ctx, norm1_g, norm2_g, final_g, mod_down, mod_up, mod_b, w_in,
           m_conv_w, m_conv_b, m_wq, m_wk, m_wv, m_gate_w, m_gate_b, m_norm_g, m_skip,
           s5_lam_re, s5_lam_im, s5_log_dt, s5_b_re, s5_b_im, s5_c_re, s5_c_im, s5_d, s5_glu_w, s5_glu_b,
           ssd_conv_w, ssd_conv_b, ssd_a_log, ssd_dt_bias, ssd_d, ssd_norm_g,
           gate_w, gate_b, branch_w, w_out, mlp_w1, mlp_w2):
    bsz, n_lat, d = x.shape
    n_ctx = ctx.shape[1]
    lay = (bsz, n_lat, n_ctx)
    grid_rows = n_lat // GRID_W
    xs = jnp.concatenate([x.reshape(bsz * n_lat, d), ctx.reshape(bsz * n_ctx, d)], axis=0)
    for l in range(DEPTH):
        mods = _pad_mod(adaln(jnp.concatenate([c_ctx[None], c], axis=0), mod_down[l], mod_up[l], mod_b[l]))
        p_all = in_projection(xs, norm1_g[l], mods, _permute_w_in(w_in[l]), lay)
        ya = mlstm_mixer(p_all, lay, m_conv_w[l], m_conv_b[l], m_wq[l], m_wk[l], m_wv[l],
                         m_gate_w[l], m_gate_b[l], m_norm_g[l], m_skip[l])
        yb1, yb2 = s5_mixer(p_all, P_U // LANES, lay, s5_lam_re[l], s5_lam_im[l], s5_log_dt[l], s5_b_re[l], s5_b_im[l],
                            s5_c_re[l], s5_c_im[l], s5_d[l])
        yc = ssd_mixer_rows(p_all, lay, grid_rows, ssd_conv_w[l], ssd_conv_b[l], ssd_a_log[l], ssd_dt_bias[l],
                            ssd_d[l], ssd_norm_g[l])
        t = merge_projection(p_all, ya, yb1, yb2, yc, s5_glu_w[l], s5_glu_b[l], gate_w[l], gate_b[l], branch_w[l])
        xs = out_projection(t, w_out[l].astype(BF16), xs, mods, lay)
        xs = mlp_block(xs, norm2_g[l], mods, mlp_w1[l].astype(BF16), mlp_w2[l].astype(BF16), lay)
    return rmsnorm(xs[:bsz * n_lat].reshape(bsz, n_lat, d), final_g)
```

```python
import functools
import math
import jax
import jax.numpy as jnp
from jax import lax
import numpy as np
from jax.experimental import pallas as pl
from jax.experimental.pallas import tpu as pltpu

D_MODEL = 4096
BATCH = 2
SEQ = 8192
DEPTH = 4

CTX_LEN = 256
GRID_W = 64
W_MIX = D_MODEL // 4
W_A = W_MIX
H_A = 8
DH_A = W_A // H_A
QKV_BLOCK = 4
MLSTM_CHUNK = 64
W_B = W_MIX
S5_GROUP = 16
G_B = W_B // S5_GROUP
P_B = 64
W_C = W_MIX
HD_C = 64
H_C = W_C // HD_C
NG_C = 4
N_C = 128
SSD_CHUNK = 128
CONV_W = 5
XBC_W = W_C + 2 * NG_C * N_C
D_FF = 4 * D_MODEL
R_MOD = 512
R_GATE = 512
N_BRANCH = 3
N_MOD = 6
SPLIT_IDX = (W_A, 2 * W_A, 2 * W_A + W_B, 2 * W_A + W_B + W_C, 2 * W_A + W_B + W_C + XBC_W, 2 * W_A + W_B + W_C + XBC_W + 2 * H_C)
IN_COLS = 2 * W_A + W_B + W_C + XBC_W + 2 * H_C + R_GATE
EPS = 1e-6
F32 = jnp.float32
BF16 = jnp.bfloat16

V7X_VMEM_LIMIT_BYTES = 56 * 1024 * 1024
MOD_ROWS = 8
LANES = 128


def rmsnorm(x, g):
    xf = x.astype(F32)
    xf = xf * lax.rsqrt(jnp.mean(xf * xf, axis=-1, keepdims=True) + EPS)
    return xf.astype(x.dtype) * g


def modulate(x, g, shift, scale):
    return rmsnorm(x, g) * (1 + scale) + shift


def adaln(cvec, down, up, b):
    return (jax.nn.silu(cvec) @ down) @ up + b


def dwconv(x, w, b):
    pad = CONV_W // 2
    y = lax.conv_general_dilated(x, w[:, None, :].astype(x.dtype), window_strides=(1,), padding=[(pad, pad)],
                                 dimension_numbers=('NWC', 'WIO', 'NWC'), feature_group_count=x.shape[-1])
    return y + b


def to_colmajor(t, rows):
    bsz, n, ch = t.shape
    return t.reshape(bsz, rows, GRID_W, ch).swapaxes(1, 2).reshape(bsz, n, ch)


def from_colmajor(t, rows):
    bsz, n, ch = t.shape
    return t.reshape(bsz, GRID_W, rows, ch).swapaxes(1, 2).reshape(bsz, n, ch)


def blockdiag(x, w):
    xb = x.reshape(*x.shape[:-1], -1, QKV_BLOCK)
    return jnp.einsum('blnc,ncd->blnd', xb, w).reshape(x.shape)


SCAN_CHUNK = 128


def _cumsum_incl(x, axis, reverse):
    n = x.shape[axis]
    idx = lax.broadcasted_iota(jnp.int32, x.shape, axis)
    s = 1
    while s < n:
        if reverse:
            x = x + jnp.where(idx < n - s, pltpu.roll(x, n - s, axis), 0.0)
        else:
            x = x + jnp.where(idx >= s, pltpu.roll(x, s, axis), 0.0)
        s *= 2
    return x


def _scan_chunk_index(s, n_ctx_chunks, n_chunks, reverse):
    if not reverse:
        return s
    return jnp.where(s < n_ctx_chunks, n_ctx_chunks - 1 - s, n_chunks - 1 - (s - n_ctx_chunks))


def _mlstm_kernel(*refs, reverse, readout, gate_col0):
    if readout:
        (q_ref, k_ref, v_ref, gr_ref, gc_ref, hb_ref, xc_ref, zm_ref, ng_ref, sk_ref, o_ref, c_scr, n_scr, m_scr) = refs
    else:
        (q_ref, k_ref, v_ref, gr_ref, gc_ref, o_ref, c_scr, n_scr, m_scr) = refs
    t_len = q_ref.shape[0]

    @pl.when(pl.program_id(1) == 0)
    def _():
        c_scr[...] = jnp.zeros_like(c_scr)
        n_scr[...] = jnp.zeros_like(n_scr)
        m_scr[...] = jnp.zeros_like(m_scr)

    b_rows = _cumsum_incl(gr_ref[H_A:, :], 1, reverse)
    b_cols = _cumsum_incl(gc_ref[:, gate_col0 + H_A:gate_col0 + 2 * H_A], 0, reverse)
    i_rows = gr_ref[:H_A, :]
    i_cols = gc_ref[:, gate_col0:gate_col0 + H_A]
    row_id = lax.broadcasted_iota(jnp.int32, (t_len, t_len), 0)
    col_id = lax.broadcasted_iota(jnp.int32, (t_len, t_len), 1)
    allowed = (col_id >= row_id) if reverse else (col_id <= row_id)
    end = 0 if reverse else t_len - 1
    m_all = m_scr[...]
    n_all = n_scr[...]
    for h in range(H_A):
        sl = slice(h * DH_A, (h + 1) * DH_A)
        q = q_ref[:, sl]
        k = k_ref[:, sl]
        v = v_ref[:, sl]
        b_row, b_col = b_rows[h:h + 1, :], b_cols[:, h:h + 1]
        i_row, i_col = i_rows[h:h + 1, :], i_cols[:, h:h + 1]
        b_end = b_row[:, end:end + 1]
        m_st = m_all[h:h + 1, 0:1]
        n_st = n_all[h:h + 1, :]
        c_st = c_scr[h]
        d_log = jnp.where(allowed, b_col - b_row + i_row, -jnp.inf)
        inter = b_col + m_st
        m_t = jnp.maximum(inter, jnp.max(d_log, axis=-1, keepdims=True))
        w_carry = jnp.exp(inter - m_t)
        qb, kb, vb = q.astype(BF16), k.astype(BF16), v.astype(BF16)
        qk = lax.dot_general(qb, kb, (((1,), (1,)), ((), ())), preferred_element_type=F32)
        s = qk * jnp.exp(d_log - m_t)
        num = (jnp.dot(s.astype(BF16), vb, preferred_element_type=F32)
               + w_carry * jnp.dot(qb, c_st.astype(BF16), preferred_element_type=F32))
        den = jnp.sum(s, axis=-1, keepdims=True) + w_carry * jnp.sum(q * n_st, axis=-1, keepdims=True)
        hh = num / jnp.maximum(jnp.abs(den), jnp.exp(-m_t))
        log_ws = b_end - b_col + i_col
        m_new = jnp.maximum(b_end + m_st, jnp.max(log_ws, axis=0, keepdims=True))
        kw = k * jnp.exp(log_ws - m_new)
        w_prev = jnp.exp(b_end + m_st - m_new)
        c_scr[h] = w_prev * c_st + lax.dot_general(kw.astype(BF16), vb, (((0,), (0,)), ((), ())),
                                                   preferred_element_type=F32)
        n_scr[h:h + 1, :] = w_prev * n_st + jnp.sum(kw, axis=0, keepdims=True)
        m_scr[h:h + 1, :] = jnp.broadcast_to(m_new, (1, DH_A))
        if readout:
            hh = hh + hb_ref[:, sl]
            mu = jnp.mean(hh, axis=-1, keepdims=True)
            var = jnp.mean(jnp.square(hh - mu), axis=-1, keepdims=True)
            hn = (hh - mu) * lax.rsqrt(var + EPS)
            zm = zm_ref[:, sl]
            o_ref[:, sl] = (hn * ng_ref[:, sl] + sk_ref[:, sl] * xc_ref[:, sl]) * (zm * jax.nn.sigmoid(zm))
        else:
            o_ref[:, sl] = hh


def mlstm_scan(q, k, v, g_rows, g_cols, lay, direction, readout_args=None):
    bsz, n_lat, n_ctx = lay
    w = q.shape[1]
    reverse = direction == 1
    t_len = SCAN_CHUNK
    assert n_lat % t_len == 0 and n_ctx % t_len == 0
    n_chunks, n_ctx_chunks = (n_lat + n_ctx) // t_len, n_ctx // t_len
    order = functools.partial(_scan_chunk_index, n_ctx_chunks=n_ctx_chunks, n_chunks=n_chunks, reverse=reverse)
    blk = lambda b, s: _seq_block(lay, b, order(s), t_len)
    tok = pl.BlockSpec((t_len, w), lambda b, s: (blk(b, s), 0))
    in_specs = [tok, tok, tok,
                pl.BlockSpec((2 * H_A, t_len), lambda b, s: (direction, blk(b, s))),
                pl.BlockSpec((t_len, 4 * H_A), lambda b, s: (blk(b, s), 0))]
    args = [q, k, v, g_rows, g_cols]
    if readout_args is not None:
        h_other, xc, p_all, norm_g, skip = readout_args
        vec = pl.BlockSpec((1, w), lambda b, s: (0, 0))
        in_specs += [tok, tok, pl.BlockSpec((t_len, w), lambda b, s: (blk(b, s), P_ZM // w)), vec, vec]
        args += [h_other, xc, p_all, norm_g.reshape(1, w).astype(F32), skip.reshape(1, w).astype(F32)]
    return pl.pallas_call(
        functools.partial(_mlstm_kernel, reverse=reverse, readout=readout_args is not None,
                          gate_col0=2 * H_A * direction),
        grid=(bsz, n_chunks),
        in_specs=in_specs,
        out_specs=tok,
        out_shape=jax.ShapeDtypeStruct(q.shape, F32),
        scratch_shapes=[pltpu.VMEM((H_A, DH_A, DH_A), F32), pltpu.VMEM((H_A, DH_A), F32), pltpu.VMEM((H_A, DH_A), F32)],
        compiler_params=pltpu.CompilerParams(
            dimension_semantics=("arbitrary", "arbitrary"),
            vmem_limit_bytes=V7X_VMEM_LIMIT_BYTES),
    )(*args)


FEAT_TILE = 256
HALO = 8
N_LB = W_A // LANES


def _seq_edges(lay, i, tile):
    bsz, n_lat, n_ctx = lay
    lat_tiles, ctx_tiles, base = n_lat // tile, n_ctx // tile, (bsz * n_lat) // tile
    pos = jnp.where(i < base, i % lat_tiles, (i - base) % ctx_tiles)
    n = jnp.where(i < base, lat_tiles, ctx_tiles)
    return pos == 0, pos == n - 1


def _dwconv_silu(prev_ref, cur_ref, next_ref, w_ref, b_ref, xp_ref, first, last):
    tile = cur_ref.shape[0]
    xp_ref[0:HALO, :] = jnp.where(first, 0.0, prev_ref[...])
    xp_ref[HALO:HALO + tile, :] = cur_ref[...]
    xp_ref[HALO + tile:, :] = jnp.where(last, 0.0, next_ref[...])
    acc = b_ref[...] + w_ref[0:1, :] * xp_ref[pl.ds(HALO - CONV_W // 2, tile), :]
    for j in range(1, CONV_W):
        acc = acc + w_ref[j:j + 1, :] * xp_ref[pl.ds(HALO - CONV_W // 2 + j, tile), :]
    return acc * jax.nn.sigmoid(acc)


def _split_dot(x, w_hi, w_lo):
    x_hi = x.astype(BF16)
    x_lo = (x - x_hi.astype(F32)).astype(BF16)
    return (jnp.dot(x_hi, w_hi, preferred_element_type=F32) + jnp.dot(x_lo, w_hi, preferred_element_type=F32)
            + jnp.dot(x_hi, w_lo, preferred_element_type=F32))


def _mlstm_feat_kernel(prev_ref, cur_ref, next_ref, cw_ref, cb_ref, wqk_hi_ref, wqk_lo_ref, wv_hi_ref, wv_lo_ref,
                       gw_ref, gwt_ref, gbc_ref, gbr_ref, xc_ref, q_ref, k_ref, v_ref, gc_ref, gr_ref,
                       xp_ref, qkv_ref, *, lay):
    tile = cur_ref.shape[0]
    first, last = _seq_edges(lay, pl.program_id(0), tile)
    xc = _dwconv_silu(prev_ref, cur_ref, next_ref, cw_ref, cb_ref, xp_ref, first, last)
    xc_ref[...] = xc
    for lb in range(N_LB):
        sl = slice(lb * LANES, (lb + 1) * LANES)
        qk = _split_dot(xc[:, sl], wqk_hi_ref[lb], wqk_lo_ref[lb])
        v = _split_dot(cur_ref[:, sl], wv_hi_ref[lb], wv_lo_ref[lb])
        q_ref[:, sl] = qk[:, :LANES] * DH_A ** -0.5
        k_ref[:, sl] = qk[:, LANES:]
        v_ref[:, sl] = v
        qkv_ref[:, sl] = qk[:, :LANES].astype(BF16)
        qkv_ref[:, W_A + lb * LANES:W_A + (lb + 1) * LANES] = qk[:, LANES:].astype(BF16)
        qkv_ref[:, 2 * W_A + lb * LANES:2 * W_A + (lb + 1) * LANES] = v.astype(BF16)
    qkv = qkv_ref[...]
    g_cols = jnp.dot(qkv, gw_ref[...], preferred_element_type=F32) + gbc_ref[...]
    g_rows = lax.dot_general(gwt_ref[...], qkv, (((1,), (1,)), ((), ())), preferred_element_type=F32) + gbr_ref[...]
    is_f_c = (lax.broadcasted_iota(jnp.int32, g_cols.shape, 1) % (2 * H_A)) >= H_A
    is_f_r = (lax.broadcasted_iota(jnp.int32, g_rows.shape, 0) % (2 * H_A)) >= H_A
    gc_ref[...] = jnp.where(is_f_c, -_softplus(-g_cols), g_cols)
    gr_ref[...] = jnp.where(is_f_r, -_softplus(-g_rows), g_rows)


def _blockdiag_dense(w):
    per = LANES // QKV_BLOCK
    w = w.astype(F32).reshape(N_LB, per, QKV_BLOCK, QKV_BLOCK)
    eye = jnp.eye(per, dtype=F32)
    return (w[:, :, :, None, :] * eye[None, :, None, :, None]).reshape(N_LB, LANES, LANES)


def _hi_lo(w):
    hi = w.astype(BF16)
    return hi, (w - hi.astype(F32)).astype(BF16)


def mlstm_features(p_all, lay, conv_w, conv_b, wq, wk, wv, gate_w, gate_b):
    rows = p_all.shape[0]
    tile = FEAT_TILE
    hb = tile // HALO
    n_halo_blocks = rows // HALO
    wqk = jnp.concatenate([_blockdiag_dense(wq), _blockdiag_dense(wk)], axis=2)
    wqk_hi, wqk_lo = _hi_lo(wqk)
    wv_hi, wv_lo = _hi_lo(_blockdiag_dense(wv))
    gw = gate_w.astype(F32).transpose(1, 0, 2).reshape(3 * W_A, 4 * H_A)
    gb = gate_b.astype(F32).reshape(1, 4 * H_A)
    cw = jnp.pad(conv_w.astype(F32), ((0, HALO - CONV_W), (0, 0)))
    tok = pl.BlockSpec((tile, W_A), lambda i: (i, 0))
    full = lambda *shape: pl.BlockSpec(shape, lambda i: (0,) * len(shape))
    return pl.pallas_call(
        functools.partial(_mlstm_feat_kernel, lay=lay),
        grid=(rows // tile,),
        in_specs=[pl.BlockSpec((HALO, W_A), lambda i: (jnp.maximum(i * hb - 1, 0), P_XM // W_A)),
                  pl.BlockSpec((tile, W_A), lambda i: (i, P_XM // W_A)),
                  pl.BlockSpec((HALO, W_A), lambda i: (jnp.minimum((i + 1) * hb, n_halo_blocks - 1), P_XM // W_A)),
                  full(HALO, W_A), full(1, W_A),
                  full(N_LB, LANES, 2 * LANES), full(N_LB, LANES, 2 * LANES),
                  full(N_LB, LANES, LANES), full(N_LB, LANES, LANES),
                  full(3 * W_A, 4 * H_A), full(4 * H_A, 3 * W_A), full(1, 4 * H_A), full(4 * H_A, 1)],
        out_specs=[tok, tok, tok, tok,
                   pl.BlockSpec((tile, 4 * H_A), lambda i: (i, 0)),
                   pl.BlockSpec((4 * H_A, tile), lambda i: (0, i))],
        out_shape=[jax.ShapeDtypeStruct((rows, W_A), F32)] * 4
        + [jax.ShapeDtypeStruct((rows, 4 * H_A), F32), jax.ShapeDtypeStruct((4 * H_A, rows), F32)],
        scratch_shapes=[pltpu.VMEM((tile + 2 * HALO, W_A), F32), pltpu.VMEM((tile, 3 * W_A), BF16)],
        compiler_params=pltpu.CompilerParams(dimension_semantics=("arbitrary",),
                                             vmem_limit_bytes=V7X_VMEM_LIMIT_BYTES),
    )(p_all, p_all, p_all, cw, conv_b.reshape(1, W_A).astype(F32), wqk_hi, wqk_lo, wv_hi, wv_lo,
      gw.astype(BF16), gw.T.astype(BF16), gb, gb.reshape(4 * H_A, 1))


def mlstm_mixer(p_all, lay, conv_w, conv_b, wq, wk, wv, gate_w, gate_b, norm_g, skip):
    xc, q, k, v, g_cols, g_rows = mlstm_features(p_all, lay, conv_w, conv_b, wq, wk, wv, gate_w, gate_b)
    h_bwd = mlstm_scan(q, k, v, g_rows, g_cols, lay, 1)
    return mlstm_scan(q, k, v, g_rows, g_cols, lay, 0, readout_args=(h_bwd, xc, p_all, norm_g, skip))


def s5_glu(y, glu_w, glu_b, dtype):
    y = jax.nn.gelu(y)
    return (y * jax.nn.sigmoid(y @ glu_w.astype(F32) + glu_b.astype(F32))).astype(dtype)


S5_T = 8
S5_GPB = LANES // S5_GROUP
S5_NB = W_B // LANES
S5_NS = S5_GPB * P_B
S5_CW = S5_T * LANES


def _s5_tables(lam_re, lam_im, log_dt, b_re, b_im, c_re, c_im, n_rows_list):
    lam = lax.complex(lam_re.astype(F32), lam_im.astype(F32))
    ldt = lam * jnp.exp(log_dt.astype(F32))[..., None]
    lam_bar = jnp.exp(ldt)
    b_bar = ((lam_bar - 1) / lam)[..., None] * lax.complex(b_re.astype(F32), b_im.astype(F32))
    c_mat = lax.complex(c_re.astype(F32), c_im.astype(F32))
    steps = jnp.arange(S5_T + 1, dtype=F32)
    pw = jnp.exp(steps[None, :, None, None] * ldt[:, None])
    eye = jnp.eye(S5_GPB, dtype=F32)

    def blocks(t):
        return t.reshape(S5_NB, S5_GPB, *t.shape[1:])

    kk = jnp.real(jnp.einsum('dgcp,djgp,dgpe->djgce', c_mat, pw[:, :S5_T], b_bar))
    tt = jnp.arange(S5_T)
    diff = tt[None, :] - tt[:, None]
    kf = kk[0][jnp.clip(diff, 0, S5_T - 1)] * (diff >= 0)[..., None, None, None].astype(F32)
    kb = kk[1][jnp.clip(-diff, 0, S5_T - 1)] * (diff <= 0)[..., None, None, None].astype(F32)
    kt = (kf + kb).transpose(2, 0, 4, 1, 3)
    kt = blocks(kt).transpose(0, 2, 1, 3, 4, 5)
    m_toep = kt[:, :, :, :, :, None, :] * eye[None, None, :, None, None, :, None]
    m_toep = m_toep.reshape(S5_NB, S5_CW, S5_CW)

    def w_in(d, powers):
        w = powers[:, :, :, None] * b_bar[d][None]
        w = jnp.stack([jnp.real(w), jnp.imag(w)], axis=0)
        w = w.transpose(2, 1, 4, 0, 3)
        w = blocks(w).transpose(0, 2, 1, 3, 4, 5)
        w = w[:, :, :, :, :, None, :] * eye[None, None, :, None, None, :, None]
        return w.reshape(S5_NB, S5_CW, 2 * S5_NS)

    def w_out(d, powers):
        w = c_mat[d][None] * powers[:, :, None, :]
        w = jnp.stack([jnp.real(w), -jnp.imag(w)], axis=0)
        w = w.transpose(2, 0, 4, 1, 3)
        w = blocks(w).transpose(0, 2, 1, 3, 4, 5)
        w = w[:, :, :, :, :, None, :] * eye[None, None, :, None, None, :, None]
        return w.reshape(S5_NB, 2 * S5_NS, S5_CW)

    we_f = w_in(0, pw[0, S5_T - 1::-1][:S5_T])
    we_b = w_in(1, pw[1, :S5_T])
    wo_f = w_out(0, pw[0, 1:])
    wo_b = w_out(1, pw[1, S5_T:0:-1])

    def rows(t):
        n = t.shape[0]
        t = t.reshape(n, S5_NB, S5_NS).transpose(1, 0, 2)
        return jnp.stack([jnp.real(t), jnp.imag(t)], axis=1)

    weights = dict(m_toep=m_toep.astype(BF16), we_f=we_f.astype(BF16), we_b=we_b.astype(BF16),
                   wo_f=wo_f.astype(BF16), wo_b=wo_b.astype(BF16))

    def scan_tables(n_rows):
        n_lv = max(1, int(math.log2(n_rows)))
        lv_steps = S5_T * (2.0 ** jnp.arange(n_lv, dtype=F32))
        jj = jnp.arange(n_rows, dtype=F32)
        tabs = []
        for d in range(2):
            lv = jnp.exp(lv_steps[:, None, None] * ldt[d][None])
            expo = jj if d == 0 else (n_rows - 1 - jj)
            pt = jnp.exp((S5_T * expo)[:, None, None] * ldt[d][None])
            tabs.append((rows(lv), rows(pt)))
        return dict(weights, lv_f=tabs[0][0], pt_f=tabs[0][1], lv_b=tabs[1][0], pt_b=tabs[1][1])

    return [scan_tables(n) for n in n_rows_list]


def _s5_state_scan(e, lv_ref, pt_ref, h0, reverse):
    n_rows = e.shape[0]
    row = lax.broadcasted_iota(jnp.int32, (n_rows, S5_NS), 0)

    def shift(x, s):
        if reverse:
            return jnp.where(row < n_rows - s, pltpu.roll(x, n_rows - s, 0), 0.0)
        return jnp.where(row >= s, pltpu.roll(x, s, 0), 0.0)

    e_re, e_im = e[:, :S5_NS], e[:, S5_NS:]
    z_re, z_im = shift(e_re, 1), shift(e_im, 1)
    k = 0
    while (1 << k) < n_rows:
        a_re, a_im = lv_ref[0, 0, k:k + 1, :], lv_ref[0, 1, k:k + 1, :]
        s_re, s_im = shift(z_re, 1 << k), shift(z_im, 1 << k)
        z_re, z_im = z_re + (a_re * s_re - a_im * s_im), z_im + (a_re * s_im + a_im * s_re)
        k += 1
    p_re, p_im = pt_ref[0, 0], pt_ref[0, 1]
    h_re = z_re + (p_re * h0[0:1, :] - p_im * h0[1:2, :])
    h_im = z_im + (p_re * h0[1:2, :] + p_im * h0[0:1, :])
    last = 0 if reverse else n_rows - 1
    a_re, a_im = lv_ref[0, 0, 0:1, :], lv_ref[0, 1, 0:1, :]
    l_re, l_im = h_re[last:last + 1, :], h_im[last:last + 1, :]
    o_re = a_re * l_re - a_im * l_im + e_re[last:last + 1, :]
    o_im = a_re * l_im + a_im * l_re + e_im[last:last + 1, :]
    return jnp.concatenate([h_re, h_im], axis=1), jnp.concatenate([o_re, o_im], axis=0)


def _s5_kernel(*refs, n_pass_through):
    (uf_ref, ub_ref, mt_ref, wef_ref, web_ref, wof_ref, wob_ref, lvf_ref, ptf_ref, lvb_ref, ptb_ref,
     d_ref, h0f_ref, h0b_ref) = refs[:14]
    y1_ref, y2_ref, hff_ref, hfb_ref, sf_ref, sb_ref = refs[14 + n_pass_through:]
    i = pl.program_id(2)
    n_rows = uf_ref.shape[0] // S5_T

    @pl.when(i == 0)
    def _():
        sf_ref[...] = h0f_ref[...]
        sb_ref[...] = h0b_ref[...]

    def chunk_rows(ref):
        return jnp.concatenate([ref[pl.ds(t, n_rows, stride=S5_T), :] for t in range(S5_T)], axis=1)

    uf = chunk_rows(uf_ref)
    ub = chunk_rows(ub_ref)
    uf_bf = uf.astype(BF16)
    y_local = jnp.dot(uf_bf, mt_ref[0], preferred_element_type=F32)
    e_f = jnp.dot(uf_bf, wef_ref[0], preferred_element_type=F32)
    e_b = jnp.dot(ub.astype(BF16), web_ref[0], preferred_element_type=F32)
    h_f, out_f = _s5_state_scan(e_f, lvf_ref, ptf_ref, sf_ref[...], reverse=False)
    h_b, out_b = _s5_state_scan(e_b, lvb_ref, ptb_ref, sb_ref[...], reverse=True)
    sf_ref[...] = out_f
    sb_ref[...] = out_b
    y1 = y_local + jnp.dot(h_f.astype(BF16), wof_ref[0], preferred_element_type=F32)
    y2 = jnp.dot(h_b.astype(BF16), wob_ref[0], preferred_element_type=F32)
    d = d_ref[...]
    for t in range(S5_T):
        cols = slice(t * LANES, (t + 1) * LANES)
        y1_ref[pl.ds(t, n_rows, stride=S5_T), :] = y1[:, cols] + d * uf[:, cols]
        y2_ref[pl.ds(t, n_rows, stride=S5_T), :] = y2[:, cols]

    @pl.when(i == pl.num_programs(2) - 1)
    def _():
        hff_ref[...] = out_f
        hfb_ref[...] = out_b


def s5_scan(u, col0, row0, bsz, n_tiles, tile, tabs, d_skip, h0_f, h0_b, y_prev=None):
    n_rows = tile // S5_T
    assert tile % S5_T == 0 and tabs['pt_f'].shape[2] == n_rows
    n_lv = tabs['lv_f'].shape[2]
    u_spec_f = pl.BlockSpec((tile, LANES), lambda b, n_, i: (row0(n_) + i, col0 + b))
    u_spec_b = pl.BlockSpec((tile, LANES), lambda b, n_, i: (row0(n_) + n_tiles - 1 - i, col0 + b))
    y_spec_f = pl.BlockSpec((tile, LANES), lambda b, n_, i: (row0(n_) + i, b))
    y_spec_b = pl.BlockSpec((tile, LANES), lambda b, n_, i: (row0(n_) + n_tiles - 1 - i, b))
    w_spec = pl.BlockSpec((1, S5_CW, S5_CW), lambda b, n_, i: (b, 0, 0))
    lv_spec = pl.BlockSpec((1, 2, n_lv, S5_NS), lambda b, n_, i: (b, 0, 0, 0))
    pt_spec = pl.BlockSpec((1, 2, n_rows, S5_NS), lambda b, n_, i: (b, 0, 0, 0))
    st_spec = pl.BlockSpec((None, None, 2, S5_NS), lambda b, n_, i: (n_, b, 0, 0))
    in_specs = [u_spec_f, u_spec_b, w_spec, w_spec, w_spec, w_spec, w_spec, lv_spec, pt_spec, lv_spec, pt_spec,
                pl.BlockSpec((1, LANES), lambda b, n_, i: (0, b)), st_spec, st_spec]
    args = [u, u, tabs['m_toep'], tabs['we_f'], tabs['we_b'], tabs['wo_f'], tabs['wo_b'],
            tabs['lv_f'], tabs['pt_f'], tabs['lv_b'], tabs['pt_b'], d_skip.reshape(1, W_B).astype(F32), h0_f, h0_b]
    aliases = {}
    if y_prev is not None:
        aliases = {len(args): 0, len(args) + 1: 1}
        in_specs += [pl.BlockSpec(memory_space=pl.ANY), pl.BlockSpec(memory_space=pl.ANY)]
        args += list(y_prev)
    y_shape = jax.ShapeDtypeStruct((u.shape[0], W_B), F32)
    st_shape = jax.ShapeDtypeStruct((bsz, S5_NB, 2, S5_NS), F32)
    return pl.pallas_call(
        functools.partial(_s5_kernel, n_pass_through=len(aliases)),
        grid=(S5_NB, bsz, n_tiles),
        in_specs=in_specs,
        out_specs=[y_spec_f, y_spec_b, st_spec, st_spec],
        out_shape=[y_shape, y_shape, st_shape, st_shape],
        input_output_aliases=aliases,
        scratch_shapes=[pltpu.VMEM((2, S5_NS), F32), pltpu.VMEM((2, S5_NS), F32)],
        compiler_params=pltpu.CompilerParams(
            dimension_semantics=("arbitrary", "arbitrary", "arbitrary"),
            vmem_limit_bytes=V7X_VMEM_LIMIT_BYTES),
    )(*args)


def s5_mixer(u, col0, lay, lam_re, lam_im, log_dt, b_re, b_im, c_re, c_im, d_skip, tile_l=1024):
    bsz, n_lat, n_ctx = lay
    tile_l = min(tile_l, n_lat)
    assert n_lat % tile_l == 0 and (bsz * n_lat) % n_ctx == 0
    prm = (lam_re, lam_im, log_dt, b_re, b_im, c_re, c_im)
    tabs_c, tabs_l = _s5_tables(*prm, n_rows_list=(n_ctx // S5_T, tile_l // S5_T))
    zero = jnp.zeros((bsz, S5_NB, 2, S5_NS), F32)
    y1, y2, hf, hb = s5_scan(u, col0, lambda n_: _ctx_row(lay, n_) // n_ctx, bsz, 1, n_ctx, tabs_c, d_skip, zero, zero)
    y1, y2, _, _ = s5_scan(u, col0, lambda n_: _lat_row(lay, n_) // tile_l, bsz, n_lat // tile_l, tile_l, tabs_l,
                           d_skip, hf, hb, y_prev=(y1, y2))
    return y1, y2


HPG_C = H_C // NG_C
GW_C = HPG_C * HD_C


def _softplus(x):
    return jnp.maximum(x, 0.0) + jnp.log1p(jnp.exp(-jnp.abs(x)))


def _ssd_kernel(*refs, reverse, combine, n_ctx_chunks, n_chunks):
    if combine:
        (prev_ref, raw_ref, next_ref, cw_ref, cb_ref, dtr_ref, dtc_ref, pc_ref, pr_ref, yo_ref, dsk_ref, o_ref,
         s_scr, xp_ref, xbc_ref) = refs
    else:
        (prev_ref, raw_ref, next_ref, cw_ref, cb_ref, dtr_ref, dtc_ref, pc_ref, pr_ref, o_ref,
         s_scr, xp_ref, xbc_ref) = refs
    t_len = raw_ref.shape[0]

    @pl.when(pl.program_id(1) == 0)
    def _():
        s_scr[...] = jnp.zeros_like(s_scr)

    j = _scan_chunk_index(pl.program_id(1), n_ctx_chunks, n_chunks, reverse)
    first = jnp.logical_or(j == 0, j == n_ctx_chunks)
    last = jnp.logical_or(j == n_ctx_chunks - 1, j == n_chunks - 1)
    xbc_ref[...] = _dwconv_silu(prev_ref, raw_ref, next_ref, cw_ref, cb_ref, xp_ref, first, last)

    dt_rows = _softplus(dtr_ref[...] + pc_ref[:, 1:2])
    cs_rows = _cumsum_incl(-jnp.exp(pc_ref[:, 0:1]) * dt_rows, 1, reverse)
    dt_cols = _softplus(dtc_ref[...] + pr_ref[1:2, :])
    cs_cols = _cumsum_incl(-jnp.exp(pr_ref[0:1, :]) * dt_cols, 0, reverse)
    end = 0 if reverse else t_len - 1
    cs_end = cs_cols[end:end + 1, :]
    row_id = lax.broadcasted_iota(jnp.int32, (t_len, t_len), 0)
    col_id = lax.broadcasted_iota(jnp.int32, (t_len, t_len), 1)
    allowed = (col_id >= row_id) if reverse else (col_id <= row_id)
    for g in range(NG_C):
        bg = xbc_ref[:, W_C + g * N_C:W_C + (g + 1) * N_C].astype(BF16)
        cg = xbc_ref[:, W_C + NG_C * N_C + g * N_C:W_C + NG_C * N_C + (g + 1) * N_C].astype(BF16)
        cb = lax.dot_general(cg, bg, (((1,), (1,)), ((), ())), preferred_element_type=F32)
        s_old = s_scr[g]
        y_off = jnp.dot(cg, s_old.astype(BF16), preferred_element_type=F32)
        ys, xds, chunk_decay = [], [], []
        for hh in range(HPG_C):
            h = g * HPG_C + hh
            cs_c, cs_r = cs_cols[:, h:h + 1], cs_rows[h:h + 1, :]
            seg = jnp.exp(jnp.where(allowed, cs_c - cs_r, -jnp.inf))
            xv = xbc_ref[:, h * HD_C:(h + 1) * HD_C] * dt_cols[:, h:h + 1]
            y_diag = jnp.dot((cb * seg).astype(BF16), xv.astype(BF16), preferred_element_type=F32)
            ys.append(y_diag + y_off[:, hh * HD_C:(hh + 1) * HD_C] * jnp.exp(cs_c))
            xds.append(xv * jnp.exp(cs_end[:, h:h + 1] - cs_c))
            chunk_decay.append(jnp.broadcast_to(jnp.exp(cs_end[:, h:h + 1]), (1, HD_C)))
        xd = jnp.concatenate(xds, axis=1).astype(BF16)
        s_loc = lax.dot_general(bg, xd, (((0,), (0,)), ((), ())), preferred_element_type=F32)
        s_scr[g] = s_old * jnp.concatenate(chunk_decay, axis=1) + s_loc
        y = jnp.concatenate(ys, axis=1)
        sl = slice(g * GW_C, (g + 1) * GW_C)
        if combine:
            y = y + yo_ref[:, sl] + xbc_ref[:, sl] * dsk_ref[:, sl]
        o_ref[:, sl] = y


def ssd_scan(xbc, conv_w, conv_b, dt_rows, dt_cols, a_log, dt_bias, n_ctx, reverse, combine_args=None):
    bsz, n, _ = xbc.shape
    t_len = SCAN_CHUNK
    assert n % t_len == 0 and n_ctx % t_len == 0
    n_chunks, n_ctx_chunks = n // t_len, n_ctx // t_len
    order = functools.partial(_scan_chunk_index, n_ctx_chunks=n_ctx_chunks, n_chunks=n_chunks, reverse=reverse)
    prm_c = jnp.stack([a_log, dt_bias], axis=1).astype(F32)
    prm_r = jnp.stack([a_log, dt_bias], axis=0).astype(F32)
    hb = t_len // HALO
    y_spec = pl.BlockSpec((None, t_len, W_C), lambda b, s: (b, order(s), 0))
    in_specs = [pl.BlockSpec((None, HALO, XBC_W), lambda b, s: (b, jnp.maximum(order(s) * hb - 1, 0), 0)),
                pl.BlockSpec((None, t_len, XBC_W), lambda b, s: (b, order(s), 0)),
                pl.BlockSpec((None, HALO, XBC_W), lambda b, s: (b, jnp.minimum((order(s) + 1) * hb, n // HALO - 1), 0)),
                pl.BlockSpec((HALO, XBC_W), lambda b, s: (0, 0)),
                pl.BlockSpec((1, XBC_W), lambda b, s: (0, 0)),
                pl.BlockSpec((None, H_C, t_len), lambda b, s: (b, 0, order(s))),
                pl.BlockSpec((None, t_len, H_C), lambda b, s: (b, order(s), 0)),
                pl.BlockSpec((H_C, 2), lambda b, s: (0, 0)),
                pl.BlockSpec((2, H_C), lambda b, s: (0, 0))]
    args = [xbc, xbc, xbc, jnp.pad(conv_w.astype(F32), ((0, HALO - CONV_W), (0, 0))),
            conv_b.reshape(1, XBC_W).astype(F32), dt_rows, dt_cols, prm_c, prm_r]
    if combine_args is not None:
        y_other, d_skip = combine_args
        in_specs += [y_spec, pl.BlockSpec((1, W_C), lambda b, s: (0, 0))]
        args += [y_other, jnp.repeat(d_skip.astype(F32), HD_C).reshape(1, W_C)]
    return pl.pallas_call(
        functools.partial(_ssd_kernel, reverse=reverse, combine=combine_args is not None,
                          n_ctx_chunks=n_ctx_chunks, n_chunks=n_chunks),
        grid=(bsz, n_chunks),
        in_specs=in_specs,
        out_specs=y_spec,
        out_shape=jax.ShapeDtypeStruct((bsz, n, W_C), F32),
        scratch_shapes=[pltpu.VMEM((NG_C, N_C, GW_C), F32), pltpu.VMEM((t_len + 2 * HALO, XBC_W), F32),
                        pltpu.VMEM((t_len, XBC_W), F32)],
        compiler_params=pltpu.CompilerParams(
            dimension_semantics=("arbitrary", "arbitrary"),
            vmem_limit_bytes=V7X_VMEM_LIMIT_BYTES),
    )(*args)


def ssd_mixer_rows(p_all, lay, rows, conv_w, conv_b, a_log, dt_bias, d_skip):
    bsz, n_lat, n_ctx = lay
    lat, ctx = p_all[:bsz * n_lat], p_all[bsz * n_lat:]

    def cols(t, n, c0, w):
        return t[:, c0:c0 + w].reshape(bsz, n, w)

    y_c, y_l = ssd_mixer(cols(ctx, n_ctx, P_XBC, XBC_W), cols(ctx, n_ctx, P_DT, 2 * H_C),
                         cols(lat, n_lat, P_XBC, XBC_W), cols(lat, n_lat, P_DT, 2 * H_C),
                         rows, conv_w, conv_b, a_log, dt_bias, d_skip)
    return jnp.concatenate([y_l.reshape(bsz * n_lat, W_C), y_c.reshape(bsz * n_ctx, W_C)], axis=0)


def ssd_mixer(xbc_c, dt_c, xbc_l, dt_l, rows, conv_w, conv_b, a_log, dt_bias, d_skip):
    n_ctx = xbc_c.shape[1]
    xbc = jnp.concatenate([xbc_c, to_colmajor(xbc_l, rows)], axis=1).astype(F32)
    dt = jnp.concatenate([dt_c, to_colmajor(dt_l, rows)], axis=1).astype(F32)
    bsz, n, _ = dt.shape
    dt_cols = dt.reshape(bsz, n, 2, H_C).transpose(0, 2, 1, 3)
    dt_rows = dt.reshape(bsz, n, 2, H_C).transpose(0, 2, 3, 1)
    y_bwd = ssd_scan(xbc, conv_w, conv_b, dt_rows[:, 1], dt_cols[:, 1], a_log[1], dt_bias[1], n_ctx, reverse=True)
    y = ssd_scan(xbc, conv_w, conv_b, dt_rows[:, 0], dt_cols[:, 0], a_log[0], dt_bias[0], n_ctx, reverse=False,
                 combine_args=(y_bwd, d_skip))
    return y[:, :n_ctx], from_colmajor(y[:, n_ctx:], rows)


TRUNK_TM = 512
P_COLS = 7168
P_XM, P_ZM, P_U, P_ZS, P_XBC, P_GD, P_DT = 0, W_A, 2 * W_A, 2 * W_A + W_B, 2 * W_A + W_B + W_C, 6144, 6656


def _lat_row(lay, b):
    bsz, n_lat, n_ctx = lay
    return b * n_lat


def _ctx_row(lay, b):
    bsz, n_lat, n_ctx = lay
    return bsz * n_lat + b * n_ctx


def _seq_block(lay, b, j, t_len):
    bsz, n_lat, n_ctx = lay
    n_cb = n_ctx // t_len
    return jnp.where(j < n_cb, (bsz * n_lat + b * n_ctx) // t_len + j, (b * n_lat) // t_len + (j - n_cb))


def _mod_index(lay, tm):
    bsz, n_lat, n_ctx = lay
    assert n_lat % tm == 0 and (bsz * n_ctx) % tm == 0
    return lambda i: jnp.where(i < (bsz * n_lat) // tm, 1 + i // (n_lat // tm), 0)


def _permute_w_in(w):
    xm_zm_u_zs_xbc, dt, gd = w[:, :SPLIT_IDX[4]], w[:, SPLIT_IDX[4]:SPLIT_IDX[5]], w[:, SPLIT_IDX[5]:]
    pad = jnp.zeros((w.shape[0], P_COLS - IN_COLS), w.dtype)
    return jnp.concatenate([xm_zm_u_zs_xbc, gd, dt, pad], axis=1).astype(BF16)


def _modulated_norm(x, g, shift, scale):
    xn = x * lax.rsqrt(jnp.mean(x * x, axis=-1, keepdims=True) + EPS) * g
    return xn * (1.0 + scale) + shift


def _win_kernel(x_ref, g_ref, mod_ref, w_ref, o_ref, h_ref):
    @pl.when(pl.program_id(1) == 0)
    def _():
        h_ref[...] = _modulated_norm(x_ref[...], g_ref[...], mod_ref[0, 0:1, :], mod_ref[0, 1:2, :]).astype(BF16)

    o_ref[...] = jnp.dot(h_ref[...], w_ref[...], preferred_element_type=F32)


def in_projection(x, g, mods, w_p, lay, tn=1024):
    m, d = x.shape
    tm = TRUNK_TM
    mod_idx = _mod_index(lay, tm)
    single = pl.Buffered(1)
    return pl.pallas_call(
        _win_kernel,
        grid=(m // tm, P_COLS // tn),
        in_specs=[pl.BlockSpec((tm, d), lambda i, j: (i, 0), pipeline_mode=single),
                  pl.BlockSpec((1, d), lambda i, j: (0, 0), pipeline_mode=single),
                  pl.BlockSpec((1, MOD_ROWS, d), lambda i, j: (mod_idx(i), 0, 0), pipeline_mode=single),
                  pl.BlockSpec((d, tn), lambda i, j: (0, j))],
        out_specs=pl.BlockSpec((tm, tn), lambda i, j: (i, j)),
        out_shape=jax.ShapeDtypeStruct((m, P_COLS), F32),
        scratch_shapes=[pltpu.VMEM((tm, d), BF16)],
        compiler_params=pltpu.CompilerParams(dimension_semantics=("arbitrary", "arbitrary"),
                                             vmem_limit_bytes=V7X_VMEM_LIMIT_BYTES),
    )(x, g.reshape(1, d).astype(F32), mods, w_p)


def _merge_kernel(gd_ref, ya_ref, y1_ref, y2_ref, yc_ref, zs_ref, ng_ref, gluw_ref, glub_ref, gw_ref, gb_ref, bw_ref,
                  o_ref, gds_ref, ys_ref):
    @pl.when(pl.program_id(1) == 0)
    def _():
        gds_ref[...] = gd_ref[...].astype(BF16)
        ys_ref[0] = ya_ref[...].astype(BF16)
        yb = jax.nn.gelu(y1_ref[...] + y2_ref[...])
        gate = jnp.dot(yb.astype(BF16), gluw_ref[...], preferred_element_type=F32) + glub_ref[...]
        ys_ref[1] = (yb * jax.nn.sigmoid(gate)).astype(BF16)
        zs = zs_ref[...]
        yg = yc_ref[...] * (zs * jax.nn.sigmoid(zs))
        yg = yg * lax.rsqrt(jnp.mean(yg * yg, axis=-1, keepdims=True) + EPS) * ng_ref[...]
        ys_ref[2] = yg.astype(BF16)

    acc = None
    for i in range(N_BRANCH):
        gate = jax.nn.sigmoid(jnp.dot(gds_ref[...], gw_ref[i], preferred_element_type=F32) + gb_ref[i])
        term = gate * jnp.dot(ys_ref[i], bw_ref[i], preferred_element_type=F32)
        acc = term if acc is None else acc + term
    o_ref[...] = acc.astype(o_ref.dtype)


def merge_projection(p_all, ya, y1, y2, yc, ssd_norm_g, glu_w, glu_b, gate_w, gate_b, branch_w, tn=1024):
    m = p_all.shape[0]
    d = gate_w.shape[-1]
    tm = TRUNK_TM
    single = pl.Buffered(1)
    row = lambda w: pl.BlockSpec((tm, w), lambda i, j: (i, 0), pipeline_mode=single)
    return pl.pallas_call(
        _merge_kernel,
        grid=(m // tm, d // tn),
        in_specs=[pl.BlockSpec((tm, R_GATE), lambda i, j: (i, P_GD // R_GATE), pipeline_mode=single),
                  row(W_A), row(W_B), row(W_B), row(W_C),
                  pl.BlockSpec((tm, W_C), lambda i, j: (i, P_ZS // W_C), pipeline_mode=single),
                  pl.BlockSpec((1, W_C), lambda i, j: (0, 0), pipeline_mode=single),
                  pl.BlockSpec((W_B, W_B), lambda i, j: (0, 0), pipeline_mode=single),
                  pl.BlockSpec((1, W_B), lambda i, j: (0, 0), pipeline_mode=single),
                  pl.BlockSpec((N_BRANCH, R_GATE, tn), lambda i, j: (0, 0, j)),
                  pl.BlockSpec((N_BRANCH, 1, tn), lambda i, j: (0, 0, j)),
                  pl.BlockSpec((N_BRANCH, W_MIX, tn), lambda i, j: (0, 0, j))],
        out_specs=pl.BlockSpec((tm, tn), lambda i, j: (i, j)),
        out_shape=jax.ShapeDtypeStruct((m, d), BF16),
        scratch_shapes=[pltpu.VMEM((tm, R_GATE), BF16), pltpu.VMEM((N_BRANCH, tm, W_MIX), BF16)],
        compiler_params=pltpu.CompilerParams(dimension_semantics=("arbitrary", "arbitrary"),
                                             vmem_limit_bytes=V7X_VMEM_LIMIT_BYTES),
    )(p_all, ya, y1, y2, yc, p_all, ssd_norm_g.reshape(1, W_C).astype(F32), glu_w.astype(BF16),
      glu_b.reshape(1, W_B).astype(F32),
      gate_w.astype(BF16), gate_b.reshape(N_BRANCH, 1, d).astype(F32), branch_w.astype(BF16))


def _wout_kernel(t_ref, w_ref, x_ref, mod_ref, o_ref):
    y = jnp.dot(t_ref[...], w_ref[...], preferred_element_type=F32)
    o_ref[...] = x_ref[...] + mod_ref[0, 2:3, :] * y


def out_projection(t, w_out_bf, x, mods, lay, tn=1024):
    m, d = x.shape
    tm = TRUNK_TM
    mod_idx = _mod_index(lay, tm)
    single = pl.Buffered(1)
    return pl.pallas_call(
        _wout_kernel,
        grid=(m // tm, d // tn),
        in_specs=[pl.BlockSpec((tm, d), lambda i, j: (i, 0), pipeline_mode=single),
                  pl.BlockSpec((d, tn), lambda i, j: (0, j)),
                  pl.BlockSpec((tm, tn), lambda i, j: (i, j)),
                  pl.BlockSpec((1, MOD_ROWS, tn), lambda i, j: (mod_idx(i), 0, j))],
        out_specs=pl.BlockSpec((tm, tn), lambda i, j: (i, j)),
        out_shape=jax.ShapeDtypeStruct((m, d), F32),
        compiler_params=pltpu.CompilerParams(dimension_semantics=("arbitrary", "arbitrary"),
                                             vmem_limit_bytes=V7X_VMEM_LIMIT_BYTES),
    )(t, w_out_bf, x, mods)


def _mlp_kernel(x_ref, g_ref, mod_ref, w1_ref, w2_ref, o_ref, h_ref):
    f = pl.program_id(1)

    @pl.when(f == 0)
    def _():
        x = x_ref[...]
        xn = x * lax.rsqrt(jnp.mean(x * x, axis=-1, keepdims=True) + EPS) * g_ref[...]
        h = xn * (1.0 + mod_ref[0, 4:5, :]) + mod_ref[0, 3:4, :]
        h_ref[...] = h.astype(BF16)
        o_ref[...] = jnp.zeros_like(o_ref)

    a = jnp.dot(h_ref[...], w1_ref[...], preferred_element_type=F32)
    a = jnp.square(jnp.maximum(a, 0.0)).astype(BF16)
    o_ref[...] += jnp.dot(a, w2_ref[...], preferred_element_type=F32)

    @pl.when(f == pl.num_programs(1) - 1)
    def _():
        o_ref[...] = x_ref[...] + mod_ref[0, 5:6, :] * o_ref[...]


def mlp_block(x, g, mod, w1, w2, lay, tf=512):
    m, d = x.shape
    dff = w1.shape[1]
    tm = TRUNK_TM
    assert m % tm == 0 and dff % tf == 0
    mod_idx = _mod_index(lay, tm)
    single = pl.Buffered(1)
    return pl.pallas_call(
        _mlp_kernel,
        grid=(m // tm, dff // tf),
        in_specs=[
            pl.BlockSpec((tm, d), lambda i, f: (i, 0), pipeline_mode=single),
            pl.BlockSpec((1, d), lambda i, f: (0, 0), pipeline_mode=single),
            pl.BlockSpec((1, MOD_ROWS, d), lambda i, f: (mod_idx(i), 0, 0), pipeline_mode=single),
            pl.BlockSpec((d, tf), lambda i, f: (0, f)),
            pl.BlockSpec((tf, d), lambda i, f: (f, 0)),
        ],
        out_specs=pl.BlockSpec((tm, d), lambda i, f: (i, 0), pipeline_mode=single),
        out_shape=jax.ShapeDtypeStruct((m, d), F32),
        scratch_shapes=[pltpu.VMEM((tm, d), BF16)],
        compiler_params=pltpu.CompilerParams(
            dimension_semantics=("arbitrary", "arbitrary"),
            vmem_limit_bytes=V7X_VMEM_LIMIT_BYTES),
    )(x, g.reshape(1, d), mod, w1, w2)


def _pad_mod(mod):
    nb = mod.shape[0]
    mod = mod.reshape(nb, N_MOD, D_MODEL)
    return jnp.pad(mod, ((0, 0), (0, MOD_ROWS - N_MOD), (0, 0)))


def kernel(x, c, ctx, c_ctx, norm1_g, norm2_g, final_g, mod_down, mod_up, mod_b, w_in,
           m_conv_w, m_conv_b, m_wq, m_wk, m_wv, m_gate_w, m_gate_b, m_norm_g, m_skip,
           s5_lam_re, s5_lam_im, s5_log_dt, s5_b_re, s5_b_im, s5_c_re, s5_c_im, s5_d, s5_glu_w, s5_glu_b,
           ssd_conv_w, ssd_conv_b, ssd_a_log, ssd_dt_bias, ssd_d, ssd_norm_g,
           gate_w, gate_b, branch_w, w_out, mlp_w1, mlp_w2):
    bsz, n_lat, d = x.shape
    n_ctx = ctx.shape[1]
    lay = (bsz, n_lat, n_ctx)
    grid_rows = n_lat // GRID_W
    xs = jnp.concatenate([x.reshape(bsz * n_lat, d), ctx.reshape(bsz * n_ctx, d)], axis=0)
    for l in range(DEPTH):
        mods = _pad_mod(adaln(jnp.concatenate([c_ctx[None], c], axis=0), mod_down[l], mod_up[l], mod_b[l]))
        p_all = in_projection(xs, norm1_g[l], mods, _permute_w_in(w_in[l]), lay)
        ya = mlstm_mixer(p_all, lay, m_conv_w[l], m_conv_b[l], m_wq[l], m_wk[l], m_wv[l],
                         m_gate_w[l], m_gate_b[l], m_norm_g[l], m_skip[l])
        yb1, yb2 = s5_mixer(p_all, P_U // LANES, lay, s5_lam_re[l], s5_lam_im[l], s5_log_dt[l], s5_b_re[l], s5_b_im[l],
                            s5_c_re[l], s5_c_im[l], s5_d[l])
        yc = ssd_mixer_rows(p_all, lay, grid_rows, ssd_conv_w[l], ssd_conv_b[l], ssd_a_log[l], ssd_dt_bias[l], ssd_d[l])
        t = merge_projection(p_all, ya, yb1, yb2, yc, ssd_norm_g[l], s5_glu_w[l], s5_glu_b[l], gate_w[l], gate_b[l],
                             branch_w[l])
        xs = out_projection(t, w_out[l].astype(BF16), xs, mods, lay)
        xs = mlp_block(xs, norm2_g[l], mods, mlp_w1[l].astype(BF16), mlp_w2[l].astype(BF16), lay)
    return rmsnorm(xs[:bsz * n_lat].reshape(bsz, n_lat, d), final_g)
```

```python
import functools
import math
import jax
import jax.numpy as jnp
from jax import lax
import numpy as np
from jax.experimental import pallas as pl
from jax.experimental.pallas import tpu as pltpu

D_MODEL = 4096
BATCH = 2
SEQ = 8192
DEPTH = 4

CTX_LEN = 256
GRID_W = 64
W_MIX = D_MODEL // 4
W_A = W_MIX
H_A = 8
DH_A = W_A // H_A
QKV_BLOCK = 4
MLSTM_CHUNK = 64
W_B = W_MIX
S5_GROUP = 16
G_B = W_B // S5_GROUP
P_B = 64
W_C = W_MIX
HD_C = 64
H_C = W_C // HD_C
NG_C = 4
N_C = 128
SSD_CHUNK = 128
CONV_W = 5
XBC_W = W_C + 2 * NG_C * N_C
D_FF = 4 * D_MODEL
R_MOD = 512
R_GATE = 512
N_BRANCH = 3
N_MOD = 6
SPLIT_IDX = (W_A, 2 * W_A, 2 * W_A + W_B, 2 * W_A + W_B + W_C, 2 * W_A + W_B + W_C + XBC_W, 2 * W_A + W_B + W_C + XBC_W + 2 * H_C)
IN_COLS = 2 * W_A + W_B + W_C + XBC_W + 2 * H_C + R_GATE
EPS = 1e-6
F32 = jnp.float32
BF16 = jnp.bfloat16

V7X_VMEM_LIMIT_BYTES = 56 * 1024 * 1024
MOD_ROWS = 8
LANES = 128


def rmsnorm(x, g):
    xf = x.astype(F32)
    xf = xf * lax.rsqrt(jnp.mean(xf * xf, axis=-1, keepdims=True) + EPS)
    return xf.astype(x.dtype) * g


def modulate(x, g, shift, scale):
    return rmsnorm(x, g) * (1 + scale) + shift


def adaln(cvec, down, up, b):
    return (jax.nn.silu(cvec) @ down) @ up + b


def dwconv(x, w, b):
    pad = CONV_W // 2
    y = lax.conv_general_dilated(x, w[:, None, :].astype(x.dtype), window_strides=(1,), padding=[(pad, pad)],
                                 dimension_numbers=('NWC', 'WIO', 'NWC'), feature_group_count=x.shape[-1])
    return y + b


def to_colmajor(t, rows):
    bsz, n, ch = t.shape
    return t.reshape(bsz, rows, GRID_W, ch).swapaxes(1, 2).reshape(bsz, n, ch)


def from_colmajor(t, rows):
    bsz, n, ch = t.shape
    return t.reshape(bsz, GRID_W, rows, ch).swapaxes(1, 2).reshape(bsz, n, ch)


def blockdiag(x, w):
    xb = x.reshape(*x.shape[:-1], -1, QKV_BLOCK)
    return jnp.einsum('blnc,ncd->blnd', xb, w).reshape(x.shape)


SCAN_CHUNK = 128


def _cumsum_incl(x, axis, reverse):
    n = x.shape[axis]
    idx = lax.broadcasted_iota(jnp.int32, x.shape, axis)
    s = 1
    while s < n:
        if reverse:
            x = x + jnp.where(idx < n - s, pltpu.roll(x, n - s, axis), 0.0)
        else:
            x = x + jnp.where(idx >= s, pltpu.roll(x, s, axis), 0.0)
        s *= 2
    return x


def _scan_chunk_index(s, n_ctx_chunks, n_chunks, reverse):
    if not reverse:
        return s
    return jnp.where(s < n_ctx_chunks, n_ctx_chunks - 1 - s, n_chunks - 1 - (s - n_ctx_chunks))


def _mlstm_kernel(*refs, reverse, readout, gate_col0):
    if readout:
        (q_ref, k_ref, v_ref, gr_ref, gc_ref, hb_ref, xc_ref, zm_ref, ng_ref, sk_ref, o_ref, c_scr, n_scr, m_scr) = refs
    else:
        (q_ref, k_ref, v_ref, gr_ref, gc_ref, o_ref, c_scr, n_scr, m_scr) = refs
    t_len = q_ref.shape[0]

    @pl.when(pl.program_id(1) == 0)
    def _():
        c_scr[...] = jnp.zeros_like(c_scr)
        n_scr[...] = jnp.zeros_like(n_scr)
        m_scr[...] = jnp.zeros_like(m_scr)

    b_rows = _cumsum_incl(gr_ref[H_A:, :], 1, reverse)
    b_cols = _cumsum_incl(gc_ref[:, gate_col0 + H_A:gate_col0 + 2 * H_A], 0, reverse)
    i_rows = gr_ref[:H_A, :]
    i_cols = gc_ref[:, gate_col0:gate_col0 + H_A]
    row_id = lax.broadcasted_iota(jnp.int32, (t_len, t_len), 0)
    col_id = lax.broadcasted_iota(jnp.int32, (t_len, t_len), 1)
    allowed = (col_id >= row_id) if reverse else (col_id <= row_id)
    end = 0 if reverse else t_len - 1
    m_all = m_scr[...]
    n_all = n_scr[...]
    for h in range(H_A):
        sl = slice(h * DH_A, (h + 1) * DH_A)
        q = q_ref[:, sl]
        k = k_ref[:, sl]
        v = v_ref[:, sl]
        b_row, b_col = b_rows[h:h + 1, :], b_cols[:, h:h + 1]
        i_row, i_col = i_rows[h:h + 1, :], i_cols[:, h:h + 1]
        b_end = b_row[:, end:end + 1]
        m_st = m_all[h:h + 1, 0:1]
        n_st = n_all[h:h + 1, :]
        c_st = c_scr[h]
        d_log = jnp.where(allowed, b_col - b_row + i_row, -jnp.inf)
        inter = b_col + m_st
        m_t = jnp.maximum(inter, jnp.max(d_log, axis=-1, keepdims=True))
        w_carry = jnp.exp(inter - m_t)
        qb, kb, vb = q.astype(BF16), k.astype(BF16), v.astype(BF16)
        qk = lax.dot_general(qb, kb, (((1,), (1,)), ((), ())), preferred_element_type=F32)
        s = qk * jnp.exp(d_log - m_t)
        num = (jnp.dot(s.astype(BF16), vb, preferred_element_type=F32)
               + w_carry * jnp.dot(qb, c_st.astype(BF16), preferred_element_type=F32))
        den = jnp.sum(s, axis=-1, keepdims=True) + w_carry * jnp.sum(q * n_st, axis=-1, keepdims=True)
        hh = num / jnp.maximum(jnp.abs(den), jnp.exp(-m_t))
        log_ws = b_end - b_col + i_col
        m_new = jnp.maximum(b_end + m_st, jnp.max(log_ws, axis=0, keepdims=True))
        kw = k * jnp.exp(log_ws - m_new)
        w_prev = jnp.exp(b_end + m_st - m_new)
        c_scr[h] = w_prev * c_st + lax.dot_general(kw.astype(BF16), vb, (((0,), (0,)), ((), ())),
                                                   preferred_element_type=F32)
        n_scr[h:h + 1, :] = w_prev * n_st + jnp.sum(kw, axis=0, keepdims=True)
        m_scr[h:h + 1, :] = jnp.broadcast_to(m_new, (1, DH_A))
        if readout:
            hh = hh + hb_ref[:, sl]
            mu = jnp.mean(hh, axis=-1, keepdims=True)
            var = jnp.mean(jnp.square(hh - mu), axis=-1, keepdims=True)
            hn = (hh - mu) * lax.rsqrt(var + EPS)
            zm = zm_ref[:, sl]
            o_ref[:, sl] = (hn * ng_ref[:, sl] + sk_ref[:, sl] * xc_ref[:, sl]) * (zm * jax.nn.sigmoid(zm))
        else:
            o_ref[:, sl] = hh


def mlstm_scan(q, k, v, g_rows, g_cols, lay, direction, readout_args=None):
    bsz, n_lat, n_ctx = lay
    w = q.shape[1]
    reverse = direction == 1
    t_len = SCAN_CHUNK
    assert n_lat % t_len == 0 and n_ctx % t_len == 0
    n_chunks, n_ctx_chunks = (n_lat + n_ctx) // t_len, n_ctx // t_len
    order = functools.partial(_scan_chunk_index, n_ctx_chunks=n_ctx_chunks, n_chunks=n_chunks, reverse=reverse)
    blk = lambda b, s: _seq_block(lay, b, order(s), t_len)
    tok = pl.BlockSpec((t_len, w), lambda b, s: (blk(b, s), 0))
    in_specs = [tok, tok, tok,
                pl.BlockSpec((2 * H_A, t_len), lambda b, s: (direction, blk(b, s))),
                pl.BlockSpec((t_len, 4 * H_A), lambda b, s: (blk(b, s), 0))]
    args = [q, k, v, g_rows, g_cols]
    if readout_args is not None:
        h_other, xc, p_all, norm_g, skip = readout_args
        vec = pl.BlockSpec((1, w), lambda b, s: (0, 0))
        in_specs += [tok, tok, pl.BlockSpec((t_len, w), lambda b, s: (blk(b, s), P_ZM // w)), vec, vec]
        args += [h_other, xc, p_all, norm_g.reshape(1, w).astype(F32), skip.reshape(1, w).astype(F32)]
    return pl.pallas_call(
        functools.partial(_mlstm_kernel, reverse=reverse, readout=readout_args is not None,
                          gate_col0=2 * H_A * direction),
        grid=(bsz, n_chunks),
        in_specs=in_specs,
        out_specs=tok,
        out_shape=jax.ShapeDtypeStruct(q.shape, F32),
        scratch_shapes=[pltpu.VMEM((H_A, DH_A, DH_A), F32), pltpu.VMEM((H_A, DH_A), F32), pltpu.VMEM((H_A, DH_A), F32)],
        compiler_params=pltpu.CompilerParams(
            dimension_semantics=("arbitrary", "arbitrary"),
            vmem_limit_bytes=V7X_VMEM_LIMIT_BYTES),
    )(*args)


FEAT_TILE = 256
HALO = 8
N_LB = W_A // LANES


def _seq_edges(lay, i, tile):
    bsz, n_lat, n_ctx = lay
    lat_tiles, ctx_tiles, base = n_lat // tile, n_ctx // tile, (bsz * n_lat) // tile
    pos = jnp.where(i < base, i % lat_tiles, (i - base) % ctx_tiles)
    n = jnp.where(i < base, lat_tiles, ctx_tiles)
    return pos == 0, pos == n - 1


def _dwconv_silu(prev_ref, cur_ref, next_ref, w_ref, b_ref, xp_ref, first, last):
    tile = cur_ref.shape[0]
    xp_ref[0:HALO, :] = jnp.where(first, 0.0, prev_ref[...])
    xp_ref[HALO:HALO + tile, :] = cur_ref[...]
    xp_ref[HALO + tile:, :] = jnp.where(last, 0.0, next_ref[...])
    acc = b_ref[...] + w_ref[0:1, :] * xp_ref[pl.ds(HALO - CONV_W // 2, tile), :]
    for j in range(1, CONV_W):
        acc = acc + w_ref[j:j + 1, :] * xp_ref[pl.ds(HALO - CONV_W // 2 + j, tile), :]
    return acc * jax.nn.sigmoid(acc)


def _split_dot(x, w_hi, w_lo):
    x_hi = x.astype(BF16)
    x_lo = (x - x_hi.astype(F32)).astype(BF16)
    return (jnp.dot(x_hi, w_hi, preferred_element_type=F32) + jnp.dot(x_lo, w_hi, preferred_element_type=F32)
            + jnp.dot(x_hi, w_lo, preferred_element_type=F32))


def _mlstm_feat_kernel(prev_ref, cur_ref, next_ref, cw_ref, cb_ref, wqk_hi_ref, wqk_lo_ref, wv_hi_ref, wv_lo_ref,
                       gw_ref, gwt_ref, gbc_ref, gbr_ref, xc_ref, q_ref, k_ref, v_ref, gc_ref, gr_ref,
                       xp_ref, qkv_ref, *, lay):
    tile = cur_ref.shape[0]
    first, last = _seq_edges(lay, pl.program_id(0), tile)
    xc = _dwconv_silu(prev_ref, cur_ref, next_ref, cw_ref, cb_ref, xp_ref, first, last)
    xc_ref[...] = xc
    for lb in range(N_LB):
        sl = slice(lb * LANES, (lb + 1) * LANES)
        qk = _split_dot(xc[:, sl], wqk_hi_ref[lb], wqk_lo_ref[lb])
        v = _split_dot(cur_ref[:, sl], wv_hi_ref[lb], wv_lo_ref[lb])
        q_ref[:, sl] = qk[:, :LANES] * DH_A ** -0.5
        k_ref[:, sl] = qk[:, LANES:]
        v_ref[:, sl] = v
        qkv_ref[:, sl] = qk[:, :LANES].astype(BF16)
        qkv_ref[:, W_A + lb * LANES:W_A + (lb + 1) * LANES] = qk[:, LANES:].astype(BF16)
        qkv_ref[:, 2 * W_A + lb * LANES:2 * W_A + (lb + 1) * LANES] = v.astype(BF16)
    qkv = qkv_ref[...]
    g_cols = jnp.dot(qkv, gw_ref[...], preferred_element_type=F32) + gbc_ref[...]
    g_rows = lax.dot_general(gwt_ref[...], qkv, (((1,), (1,)), ((), ())), preferred_element_type=F32) + gbr_ref[...]
    is_f_c = (lax.broadcasted_iota(jnp.int32, g_cols.shape, 1) % (2 * H_A)) >= H_A
    is_f_r = (lax.broadcasted_iota(jnp.int32, g_rows.shape, 0) % (2 * H_A)) >= H_A
    gc_ref[...] = jnp.where(is_f_c, -_softplus(-g_cols), g_cols)
    gr_ref[...] = jnp.where(is_f_r, -_softplus(-g_rows), g_rows)


def _blockdiag_dense(w):
    per = LANES // QKV_BLOCK
    w = w.astype(F32).reshape(N_LB, per, QKV_BLOCK, QKV_BLOCK)
    eye = jnp.eye(per, dtype=F32)
    return (w[:, :, :, None, :] * eye[None, :, None, :, None]).reshape(N_LB, LANES, LANES)


def _hi_lo(w):
    hi = w.astype(BF16)
    return hi, (w - hi.astype(F32)).astype(BF16)


def mlstm_features(p_all, lay, conv_w, conv_b, wq, wk, wv, gate_w, gate_b):
    rows = p_all.shape[0]
    tile = FEAT_TILE
    hb = tile // HALO
    n_halo_blocks = rows // HALO
    wqk = jnp.concatenate([_blockdiag_dense(wq), _blockdiag_dense(wk)], axis=2)
    wqk_hi, wqk_lo = _hi_lo(wqk)
    wv_hi, wv_lo = _hi_lo(_blockdiag_dense(wv))
    gw = gate_w.astype(F32).transpose(1, 0, 2).reshape(3 * W_A, 4 * H_A)
    gb = gate_b.astype(F32).reshape(1, 4 * H_A)
    cw = jnp.pad(conv_w.astype(F32), ((0, HALO - CONV_W), (0, 0)))
    tok = pl.BlockSpec((tile, W_A), lambda i: (i, 0))
    full = lambda *shape: pl.BlockSpec(shape, lambda i: (0,) * len(shape))
    return pl.pallas_call(
        functools.partial(_mlstm_feat_kernel, lay=lay),
        grid=(rows // tile,),
        in_specs=[pl.BlockSpec((HALO, W_A), lambda i: (jnp.maximum(i * hb - 1, 0), P_XM // W_A)),
                  pl.BlockSpec((tile, W_A), lambda i: (i, P_XM // W_A)),
                  pl.BlockSpec((HALO, W_A), lambda i: (jnp.minimum((i + 1) * hb, n_halo_blocks - 1), P_XM // W_A)),
                  full(HALO, W_A), full(1, W_A),
                  full(N_LB, LANES, 2 * LANES), full(N_LB, LANES, 2 * LANES),
                  full(N_LB, LANES, LANES), full(N_LB, LANES, LANES),
                  full(3 * W_A, 4 * H_A), full(4 * H_A, 3 * W_A), full(1, 4 * H_A), full(4 * H_A, 1)],
        out_specs=[tok, tok, tok, tok,
                   pl.BlockSpec((tile, 4 * H_A), lambda i: (i, 0)),
                   pl.BlockSpec((4 * H_A, tile), lambda i: (0, i))],
        out_shape=[jax.ShapeDtypeStruct((rows, W_A), F32)] * 4
        + [jax.ShapeDtypeStruct((rows, 4 * H_A), F32), jax.ShapeDtypeStruct((4 * H_A, rows), F32)],
        scratch_shapes=[pltpu.VMEM((tile + 2 * HALO, W_A), F32), pltpu.VMEM((tile, 3 * W_A), BF16)],
        compiler_params=pltpu.CompilerParams(dimension_semantics=("arbitrary",),
                                             vmem_limit_bytes=V7X_VMEM_LIMIT_BYTES),
    )(p_all, p_all, p_all, cw, conv_b.reshape(1, W_A).astype(F32), wqk_hi, wqk_lo, wv_hi, wv_lo,
      gw.astype(BF16), gw.T.astype(BF16), gb, gb.reshape(4 * H_A, 1))


def mlstm_mixer(p_all, lay, conv_w, conv_b, wq, wk, wv, gate_w, gate_b, norm_g, skip):
    xc, q, k, v, g_cols, g_rows = mlstm_features(p_all, lay, conv_w, conv_b, wq, wk, wv, gate_w, gate_b)
    h_bwd = mlstm_scan(q, k, v, g_rows, g_cols, lay, 1)
    return mlstm_scan(q, k, v, g_rows, g_cols, lay, 0, readout_args=(h_bwd, xc, p_all, norm_g, skip))


def s5_glu(y, glu_w, glu_b, dtype):
    y = jax.nn.gelu(y)
    return (y * jax.nn.sigmoid(y @ glu_w.astype(F32) + glu_b.astype(F32))).astype(dtype)


S5_T = 8
S5_GPB = LANES // S5_GROUP
S5_NB = W_B // LANES
S5_NS = S5_GPB * P_B
S5_CW = S5_T * LANES


def _s5_tables(lam_re, lam_im, log_dt, b_re, b_im, c_re, c_im, n_rows_list):
    lam = lax.complex(lam_re.astype(F32), lam_im.astype(F32))
    ldt = lam * jnp.exp(log_dt.astype(F32))[..., None]
    lam_bar = jnp.exp(ldt)
    b_bar = ((lam_bar - 1) / lam)[..., None] * lax.complex(b_re.astype(F32), b_im.astype(F32))
    c_mat = lax.complex(c_re.astype(F32), c_im.astype(F32))
    steps = jnp.arange(S5_T + 1, dtype=F32)
    pw = jnp.exp(steps[None, :, None, None] * ldt[:, None])
    eye = jnp.eye(S5_GPB, dtype=F32)

    def blocks(t):
        return t.reshape(S5_NB, S5_GPB, *t.shape[1:])

    kk = jnp.real(jnp.einsum('dgcp,djgp,dgpe->djgce', c_mat, pw[:, :S5_T], b_bar))
    tt = jnp.arange(S5_T)
    diff = tt[None, :] - tt[:, None]
    kf = kk[0][jnp.clip(diff, 0, S5_T - 1)] * (diff >= 0)[..., None, None, None].astype(F32)
    kb = kk[1][jnp.clip(-diff, 0, S5_T - 1)] * (diff <= 0)[..., None, None, None].astype(F32)
    kt = (kf + kb).transpose(2, 0, 4, 1, 3)
    kt = blocks(kt).transpose(0, 2, 1, 3, 4, 5)
    m_toep = kt[:, :, :, :, :, None, :] * eye[None, None, :, None, None, :, None]
    m_toep = m_toep.reshape(S5_NB, S5_CW, S5_CW)

    def w_in(d, powers):
        w = powers[:, :, :, None] * b_bar[d][None]
        w = jnp.stack([jnp.real(w), jnp.imag(w)], axis=0)
        w = w.transpose(2, 1, 4, 0, 3)
        w = blocks(w).transpose(0, 2, 1, 3, 4, 5)
        w = w[:, :, :, :, :, None, :] * eye[None, None, :, None, None, :, None]
        return w.reshape(S5_NB, S5_CW, 2 * S5_NS)

    def w_out(d, powers):
        w = c_mat[d][None] * powers[:, :, None, :]
        w = jnp.stack([jnp.real(w), -jnp.imag(w)], axis=0)
        w = w.transpose(2, 0, 4, 1, 3)
        w = blocks(w).transpose(0, 2, 1, 3, 4, 5)
        w = w[:, :, :, :, :, None, :] * eye[None, None, :, None, None, :, None]
        return w.reshape(S5_NB, 2 * S5_NS, S5_CW)

    we_f = w_in(0, pw[0, S5_T - 1::-1][:S5_T])
    we_b = w_in(1, pw[1, :S5_T])
    wo_f = w_out(0, pw[0, 1:])
    wo_b = w_out(1, pw[1, S5_T:0:-1])

    def rows(t):
        n = t.shape[0]
        t = t.reshape(n, S5_NB, S5_NS).transpose(1, 0, 2)
        return jnp.stack([jnp.real(t), jnp.imag(t)], axis=1)

    weights = dict(m_toep=m_toep.astype(BF16), we_f=we_f.astype(BF16), we_b=we_b.astype(BF16),
                   wo_f=wo_f.astype(BF16), wo_b=wo_b.astype(BF16))

    def scan_tables(n_rows):
        n_lv = max(1, int(math.log2(n_rows)))
        lv_steps = S5_T * (2.0 ** jnp.arange(n_lv, dtype=F32))
        jj = jnp.arange(n_rows, dtype=F32)
        tabs = []
        for d in range(2):
            lv = jnp.exp(lv_steps[:, None, None] * ldt[d][None])
            expo = jj if d == 0 else (n_rows - 1 - jj)
            pt = jnp.exp((S5_T * expo)[:, None, None] * ldt[d][None])
            tabs.append((rows(lv), rows(pt)))
        return dict(weights, lv_f=tabs[0][0], pt_f=tabs[0][1], lv_b=tabs[1][0], pt_b=tabs[1][1])

    return [scan_tables(n) for n in n_rows_list]


def _s5_state_scan(e, lv_ref, pt_ref, h0, reverse):
    n_rows = e.shape[0]
    row = lax.broadcasted_iota(jnp.int32, (n_rows, S5_NS), 0)

    def shift(x, s):
        if reverse:
            return jnp.where(row < n_rows - s, pltpu.roll(x, n_rows - s, 0), 0.0)
        return jnp.where(row >= s, pltpu.roll(x, s, 0), 0.0)

    e_re, e_im = e[:, :S5_NS], e[:, S5_NS:]
    z_re, z_im = shift(e_re, 1), shift(e_im, 1)
    k = 0
    while (1 << k) < n_rows:
        a_re, a_im = lv_ref[0, 0, k:k + 1, :], lv_ref[0, 1, k:k + 1, :]
        s_re, s_im = shift(z_re, 1 << k), shift(z_im, 1 << k)
        z_re, z_im = z_re + (a_re * s_re - a_im * s_im), z_im + (a_re * s_im + a_im * s_re)
        k += 1
    p_re, p_im = pt_ref[0, 0], pt_ref[0, 1]
    h_re = z_re + (p_re * h0[0:1, :] - p_im * h0[1:2, :])
    h_im = z_im + (p_re * h0[1:2, :] + p_im * h0[0:1, :])
    last = 0 if reverse else n_rows - 1
    a_re, a_im = lv_ref[0, 0, 0:1, :], lv_ref[0, 1, 0:1, :]
    l_re, l_im = h_re[last:last + 1, :], h_im[last:last + 1, :]
    o_re = a_re * l_re - a_im * l_im + e_re[last:last + 1, :]
    o_im = a_re * l_im + a_im * l_re + e_im[last:last + 1, :]
    return jnp.concatenate([h_re, h_im], axis=1), jnp.concatenate([o_re, o_im], axis=0)


def _s5_kernel(*refs, n_pass_through):
    (uf_ref, ub_ref, mt_ref, wef_ref, web_ref, wof_ref, wob_ref, lvf_ref, ptf_ref, lvb_ref, ptb_ref,
     d_ref, h0f_ref, h0b_ref) = refs[:14]
    y1_ref, y2_ref, hff_ref, hfb_ref, sf_ref, sb_ref = refs[14 + n_pass_through:]
    i = pl.program_id(2)
    n_rows = uf_ref.shape[0] // S5_T

    @pl.when(i == 0)
    def _():
        sf_ref[...] = h0f_ref[...]
        sb_ref[...] = h0b_ref[...]

    def chunk_rows(ref):
        return jnp.concatenate([ref[pl.ds(t, n_rows, stride=S5_T), :] for t in range(S5_T)], axis=1)

    uf = chunk_rows(uf_ref)
    ub = chunk_rows(ub_ref)
    uf_bf = uf.astype(BF16)
    y_local = jnp.dot(uf_bf, mt_ref[0], preferred_element_type=F32)
    e_f = jnp.dot(uf_bf, wef_ref[0], preferred_element_type=F32)
    e_b = jnp.dot(ub.astype(BF16), web_ref[0], preferred_element_type=F32)
    h_f, out_f = _s5_state_scan(e_f, lvf_ref, ptf_ref, sf_ref[...], reverse=False)
    h_b, out_b = _s5_state_scan(e_b, lvb_ref, ptb_ref, sb_ref[...], reverse=True)
    sf_ref[...] = out_f
    sb_ref[...] = out_b
    y1 = y_local + jnp.dot(h_f.astype(BF16), wof_ref[0], preferred_element_type=F32)
    y2 = jnp.dot(h_b.astype(BF16), wob_ref[0], preferred_element_type=F32)
    d = d_ref[...]
    for t in range(S5_T):
        cols = slice(t * LANES, (t + 1) * LANES)
        y1_ref[pl.ds(t, n_rows, stride=S5_T), :] = y1[:, cols] + d * uf[:, cols]
        y2_ref[pl.ds(t, n_rows, stride=S5_T), :] = y2[:, cols]

    @pl.when(i == pl.num_programs(2) - 1)
    def _():
        hff_ref[...] = out_f
        hfb_ref[...] = out_b


def s5_scan(u, col0, row0, bsz, n_tiles, tile, tabs, d_skip, h0_f, h0_b, y_prev=None):
    n_rows = tile // S5_T
    assert tile % S5_T == 0 and tabs['pt_f'].shape[2] == n_rows
    n_lv = tabs['lv_f'].shape[2]
    u_spec_f = pl.BlockSpec((tile, LANES), lambda b, n_, i: (row0(n_) + i, col0 + b))
    u_spec_b = pl.BlockSpec((tile, LANES), lambda b, n_, i: (row0(n_) + n_tiles - 1 - i, col0 + b))
    y_spec_f = pl.BlockSpec((tile, LANES), lambda b, n_, i: (row0(n_) + i, b))
    y_spec_b = pl.BlockSpec((tile, LANES), lambda b, n_, i: (row0(n_) + n_tiles - 1 - i, b))
    w_spec = pl.BlockSpec((1, S5_CW, S5_CW), lambda b, n_, i: (b, 0, 0))
    lv_spec = pl.BlockSpec((1, 2, n_lv, S5_NS), lambda b, n_, i: (b, 0, 0, 0))
    pt_spec = pl.BlockSpec((1, 2, n_rows, S5_NS), lambda b, n_, i: (b, 0, 0, 0))
    st_spec = pl.BlockSpec((None, None, 2, S5_NS), lambda b, n_, i: (n_, b, 0, 0))
    in_specs = [u_spec_f, u_spec_b, w_spec, w_spec, w_spec, w_spec, w_spec, lv_spec, pt_spec, lv_spec, pt_spec,
                pl.BlockSpec((1, LANES), lambda b, n_, i: (0, b)), st_spec, st_spec]
    args = [u, u, tabs['m_toep'], tabs['we_f'], tabs['we_b'], tabs['wo_f'], tabs['wo_b'],
            tabs['lv_f'], tabs['pt_f'], tabs['lv_b'], tabs['pt_b'], d_skip.reshape(1, W_B).astype(F32), h0_f, h0_b]
    aliases = {}
    if y_prev is not None:
        aliases = {len(args): 0, len(args) + 1: 1}
        in_specs += [pl.BlockSpec(memory_space=pl.ANY), pl.BlockSpec(memory_space=pl.ANY)]
        args += list(y_prev)
    y_shape = jax.ShapeDtypeStruct((u.shape[0], W_B), F32)
    st_shape = jax.ShapeDtypeStruct((bsz, S5_NB, 2, S5_NS), F32)
    return pl.pallas_call(
        functools.partial(_s5_kernel, n_pass_through=len(aliases)),
        grid=(S5_NB, bsz, n_tiles),
        in_specs=in_specs,
        out_specs=[y_spec_f, y_spec_b, st_spec, st_spec],
        out_shape=[y_shape, y_shape, st_shape, st_shape],
        input_output_aliases=aliases,
        scratch_shapes=[pltpu.VMEM((2, S5_NS), F32), pltpu.VMEM((2, S5_NS), F32)],
        compiler_params=pltpu.CompilerParams(
            dimension_semantics=("arbitrary", "arbitrary", "arbitrary"),
            vmem_limit_bytes=V7X_VMEM_LIMIT_BYTES),
    )(*args)


def s5_mixer(u, col0, lay, lam_re, lam_im, log_dt, b_re, b_im, c_re, c_im, d_skip, tile_l=1024):
    bsz, n_lat, n_ctx = lay
    tile_l = min(tile_l, n_lat)
    assert n_lat % tile_l == 0 and (bsz * n_lat) % n_ctx == 0
    prm = (lam_re, lam_im, log_dt, b_re, b_im, c_re, c_im)
    tabs_c, tabs_l = _s5_tables(*prm, n_rows_list=(n_ctx // S5_T, tile_l // S5_T))
    zero = jnp.zeros((bsz, S5_NB, 2, S5_NS), F32)
    y1, y2, hf, hb = s5_scan(u, col0, lambda n_: _ctx_row(lay, n_) // n_ctx, bsz, 1, n_ctx, tabs_c, d_skip, zero, zero)
    y1, y2, _, _ = s5_scan(u, col0, lambda n_: _lat_row(lay, n_) // tile_l, bsz, n_lat // tile_l, tile_l, tabs_l,
                           d_skip, hf, hb, y_prev=(y1, y2))
    return y1, y2


HPG_C = H_C // NG_C
GW_C = HPG_C * HD_C


def _softplus(x):
    return jnp.maximum(x, 0.0) + jnp.log1p(jnp.exp(-jnp.abs(x)))


def _ssd_kernel(*refs, reverse, combine, n_ctx_chunks, n_chunks):
    if combine:
        (prev_ref, raw_ref, next_ref, cw_ref, cb_ref, dtr_ref, dtc_ref, pc_ref, pr_ref, yo_ref, dsk_ref, o_ref,
         s_scr, xp_ref, xbc_ref) = refs
    else:
        (prev_ref, raw_ref, next_ref, cw_ref, cb_ref, dtr_ref, dtc_ref, pc_ref, pr_ref, o_ref,
         s_scr, xp_ref, xbc_ref) = refs
    t_len = raw_ref.shape[0]

    @pl.when(pl.program_id(1) == 0)
    def _():
        s_scr[...] = jnp.zeros_like(s_scr)

    j = _scan_chunk_index(pl.program_id(1), n_ctx_chunks, n_chunks, reverse)
    first = jnp.logical_or(j == 0, j == n_ctx_chunks)
    last = jnp.logical_or(j == n_ctx_chunks - 1, j == n_chunks - 1)
    xbc_ref[...] = _dwconv_silu(prev_ref, raw_ref, next_ref, cw_ref, cb_ref, xp_ref, first, last)

    dt_rows = _softplus(dtr_ref[...] + pc_ref[:, 1:2])
    cs_rows = _cumsum_incl(-jnp.exp(pc_ref[:, 0:1]) * dt_rows, 1, reverse)
    dt_cols = _softplus(dtc_ref[...] + pr_ref[1:2, :])
    cs_cols = _cumsum_incl(-jnp.exp(pr_ref[0:1, :]) * dt_cols, 0, reverse)
    end = 0 if reverse else t_len - 1
    cs_end = cs_cols[end:end + 1, :]
    row_id = lax.broadcasted_iota(jnp.int32, (t_len, t_len), 0)
    col_id = lax.broadcasted_iota(jnp.int32, (t_len, t_len), 1)
    allowed = (col_id >= row_id) if reverse else (col_id <= row_id)
    for g in range(NG_C):
        bg = xbc_ref[:, W_C + g * N_C:W_C + (g + 1) * N_C].astype(BF16)
        cg = xbc_ref[:, W_C + NG_C * N_C + g * N_C:W_C + NG_C * N_C + (g + 1) * N_C].astype(BF16)
        cb = lax.dot_general(cg, bg, (((1,), (1,)), ((), ())), preferred_element_type=F32)
        s_old = s_scr[g]
        y_off = jnp.dot(cg, s_old.astype(BF16), preferred_element_type=F32)
        ys, xds, chunk_decay = [], [], []
        for hh in range(HPG_C):
            h = g * HPG_C + hh
            cs_c, cs_r = cs_cols[:, h:h + 1], cs_rows[h:h + 1, :]
            seg = jnp.exp(jnp.where(allowed, cs_c - cs_r, -jnp.inf))
            xv = xbc_ref[:, h * HD_C:(h + 1) * HD_C] * dt_cols[:, h:h + 1]
            y_diag = jnp.dot((cb * seg).astype(BF16), xv.astype(BF16), preferred_element_type=F32)
            ys.append(y_diag + y_off[:, hh * HD_C:(hh + 1) * HD_C] * jnp.exp(cs_c))
            xds.append(xv * jnp.exp(cs_end[:, h:h + 1] - cs_c))
            chunk_decay.append(jnp.broadcast_to(jnp.exp(cs_end[:, h:h + 1]), (1, HD_C)))
        xd = jnp.concatenate(xds, axis=1).astype(BF16)
        s_loc = lax.dot_general(bg, xd, (((0,), (0,)), ((), ())), preferred_element_type=F32)
        s_scr[g] = s_old * jnp.concatenate(chunk_decay, axis=1) + s_loc
        y = jnp.concatenate(ys, axis=1)
        sl = slice(g * GW_C, (g + 1) * GW_C)
        if combine:
            y = y + yo_ref[:, sl] + xbc_ref[:, sl] * dsk_ref[:, sl]
        o_ref[:, sl] = y


def ssd_scan(xbc, conv_w, conv_b, dt_rows, dt_cols, a_log, dt_bias, n_ctx, reverse, combine_args=None):
    bsz, n, _ = xbc.shape
    t_len = SCAN_CHUNK
    assert n % t_len == 0 and n_ctx % t_len == 0
    n_chunks, n_ctx_chunks = n // t_len, n_ctx // t_len
    order = functools.partial(_scan_chunk_index, n_ctx_chunks=n_ctx_chunks, n_chunks=n_chunks, reverse=reverse)
    prm_c = jnp.stack([a_log, dt_bias], axis=1).astype(F32)
    prm_r = jnp.stack([a_log, dt_bias], axis=0).astype(F32)
    hb = t_len // HALO
    y_spec = pl.BlockSpec((None, t_len, W_C), lambda b, s: (b, order(s), 0))
    in_specs = [pl.BlockSpec((None, HALO, XBC_W), lambda b, s: (b, jnp.maximum(order(s) * hb - 1, 0), 0)),
                pl.BlockSpec((None, t_len, XBC_W), lambda b, s: (b, order(s), 0)),
                pl.BlockSpec((None, HALO, XBC_W), lambda b, s: (b, jnp.minimum((order(s) + 1) * hb, n // HALO - 1), 0)),
                pl.BlockSpec((HALO, XBC_W), lambda b, s: (0, 0)),
                pl.BlockSpec((1, XBC_W), lambda b, s: (0, 0)),
                pl.BlockSpec((None, H_C, t_len), lambda b, s: (b, 0, order(s))),
                pl.BlockSpec((None, t_len, H_C), lambda b, s: (b, order(s), 0)),
                pl.BlockSpec((H_C, 2), lambda b, s: (0, 0)),
                pl.BlockSpec((2, H_C), lambda b, s: (0, 0))]
    args = [xbc, xbc, xbc, jnp.pad(conv_w.astype(F32), ((0, HALO - CONV_W), (0, 0))),
            conv_b.reshape(1, XBC_W).astype(F32), dt_rows, dt_cols, prm_c, prm_r]
    if combine_args is not None:
        y_other, d_skip = combine_args
        in_specs += [y_spec, pl.BlockSpec((1, W_C), lambda b, s: (0, 0))]
        args += [y_other, jnp.repeat(d_skip.astype(F32), HD_C).reshape(1, W_C)]
    return pl.pallas_call(
        functools.partial(_ssd_kernel, reverse=reverse, combine=combine_args is not None,
                          n_ctx_chunks=n_ctx_chunks, n_chunks=n_chunks),
        grid=(bsz, n_chunks),
        in_specs=in_specs,
        out_specs=y_spec,
        out_shape=jax.ShapeDtypeStruct((bsz, n, W_C), F32),
        scratch_shapes=[pltpu.VMEM((NG_C, N_C, GW_C), F32), pltpu.VMEM((t_len + 2 * HALO, XBC_W), F32),
                        pltpu.VMEM((t_len, XBC_W), F32)],
        compiler_params=pltpu.CompilerParams(
            dimension_semantics=("arbitrary", "arbitrary"),
            vmem_limit_bytes=V7X_VMEM_LIMIT_BYTES),
    )(*args)


def ssd_mixer_rows(p_all, lay, rows, conv_w, conv_b, a_log, dt_bias, d_skip):
    bsz, n_lat, n_ctx = lay
    lat, ctx = p_all[:bsz * n_lat], p_all[bsz * n_lat:]

    def cols(t, n, c0, w):
        return t[:, c0:c0 + w].reshape(bsz, n, w)

    y_c, y_l = ssd_mixer(cols(ctx, n_ctx, P_XBC, XBC_W), cols(ctx, n_ctx, P_DT, 2 * H_C),
                         cols(lat, n_lat, P_XBC, XBC_W), cols(lat, n_lat, P_DT, 2 * H_C),
                         rows, conv_w, conv_b, a_log, dt_bias, d_skip)
    return jnp.concatenate([y_l.reshape(bsz * n_lat, W_C), y_c.reshape(bsz * n_ctx, W_C)], axis=0)


def ssd_mixer(xbc_c, dt_c, xbc_l, dt_l, rows, conv_w, conv_b, a_log, dt_bias, d_skip):
    n_ctx = xbc_c.shape[1]
    xbc = jnp.concatenate([xbc_c, to_colmajor(xbc_l, rows)], axis=1).astype(F32)
    dt = jnp.concatenate([dt_c, to_colmajor(dt_l, rows)], axis=1).astype(F32)
    bsz, n, _ = dt.shape
    dt_cols = dt.reshape(bsz, n, 2, H_C).transpose(0, 2, 1, 3)
    dt_rows = dt.reshape(bsz, n, 2, H_C).transpose(0, 2, 3, 1)
    y_bwd = ssd_scan(xbc, conv_w, conv_b, dt_rows[:, 1], dt_cols[:, 1], a_log[1], dt_bias[1], n_ctx, reverse=True)
    y = ssd_scan(xbc, conv_w, conv_b, dt_rows[:, 0], dt_cols[:, 0], a_log[0], dt_bias[0], n_ctx, reverse=False,
                 combine_args=(y_bwd, d_skip))
    return y[:, :n_ctx], from_colmajor(y[:, n_ctx:], rows)


COLS_PER_STEP = 8


def _dwconv_silu_vals(prev, cur, nxt, w_ref, b_ref, xp_ref):
    tile = cur.shape[0]
    xp_ref[0:HALO, :] = prev
    xp_ref[HALO:HALO + tile, :] = cur
    xp_ref[HALO + tile:, :] = nxt
    acc = b_ref[...] + w_ref[0:1, :] * xp_ref[pl.ds(HALO - CONV_W // 2, tile), :]
    for j in range(1, CONV_W):
        acc = acc + w_ref[j:j + 1, :] * xp_ref[pl.ds(HALO - CONV_W // 2 + j, tile), :]
    return acc * jax.nn.sigmoid(acc)


def _ssd_chunk(xbc_ref, dt_rows_raw, dt_cols_raw, pc_ref, pr_ref, s_scr, reverse):
    t_len = xbc_ref.shape[0]
    dt_rows = _softplus(dt_rows_raw + pc_ref[:, 1:2])
    cs_rows = _cumsum_incl(-jnp.exp(pc_ref[:, 0:1]) * dt_rows, 1, reverse)
    dt_cols = _softplus(dt_cols_raw + pr_ref[1:2, :])
    cs_cols = _cumsum_incl(-jnp.exp(pr_ref[0:1, :]) * dt_cols, 0, reverse)
    end = 0 if reverse else t_len - 1
    cs_end = cs_cols[end:end + 1, :]
    row_id = lax.broadcasted_iota(jnp.int32, (t_len, t_len), 0)
    col_id = lax.broadcasted_iota(jnp.int32, (t_len, t_len), 1)
    allowed = (col_id >= row_id) if reverse else (col_id <= row_id)
    out = []
    for g in range(NG_C):
        bg = xbc_ref[:, W_C + g * N_C:W_C + (g + 1) * N_C].astype(BF16)
        cg = xbc_ref[:, W_C + NG_C * N_C + g * N_C:W_C + NG_C * N_C + (g + 1) * N_C].astype(BF16)
        cb = lax.dot_general(cg, bg, (((1,), (1,)), ((), ())), preferred_element_type=F32)
        s_old = s_scr[g]
        y_off = jnp.dot(cg, s_old.astype(BF16), preferred_element_type=F32)
        ys, xds, chunk_decay = [], [], []
        for hh in range(HPG_C):
            h = g * HPG_C + hh
            cs_c, cs_r = cs_cols[:, h:h + 1], cs_rows[h:h + 1, :]
            seg = jnp.exp(jnp.where(allowed, cs_c - cs_r, -jnp.inf))
            xv = xbc_ref[:, h * HD_C:(h + 1) * HD_C] * dt_cols[:, h:h + 1]
            y_diag = jnp.dot((cb * seg).astype(BF16), xv.astype(BF16), preferred_element_type=F32)
            ys.append(y_diag + y_off[:, hh * HD_C:(hh + 1) * HD_C] * jnp.exp(cs_c))
            xds.append(xv * jnp.exp(cs_end[:, h:h + 1] - cs_c))
            chunk_decay.append(jnp.broadcast_to(jnp.exp(cs_end[:, h:h + 1]), (1, HD_C)))
        xd = jnp.concatenate(xds, axis=1).astype(BF16)
        s_loc = lax.dot_general(bg, xd, (((0,), (0,)), ((), ())), preferred_element_type=F32)
        s_scr[g] = s_old * jnp.concatenate(chunk_decay, axis=1) + s_loc
        out.append(jnp.concatenate(ys, axis=1))
    return out


def _ssd_ctx_kernel(*refs, reverse, combine):
    if combine:
        (prev_ref, raw_ref, next_ref, cw_ref, cb_ref, dtr_ref, dtc_ref, pc_ref, pr_ref, yo_ref, dsk_ref,
         o_ref, sfin_ref, s_scr, xp_ref, xbc_ref) = refs
    else:
        (prev_ref, raw_ref, next_ref, cw_ref, cb_ref, dtr_ref, dtc_ref, pc_ref, pr_ref,
         o_ref, sfin_ref, s_scr, xp_ref, xbc_ref) = refs
    s, n = pl.program_id(1), pl.num_programs(1)

    @pl.when(s == 0)
    def _():
        s_scr[...] = jnp.zeros_like(s_scr)

    j = n - 1 - s if reverse else s
    prev = jnp.where(j == 0, 0.0, prev_ref[...])
    nxt = jnp.where(j == n - 1, 0.0, next_ref[...])
    xbc_ref[...] = _dwconv_silu_vals(prev, raw_ref[...], nxt, cw_ref, cb_ref, xp_ref)
    ys = _ssd_chunk(xbc_ref, dtr_ref[...], dtc_ref[...], pc_ref, pr_ref, s_scr, reverse)
    for g in range(NG_C):
        sl = slice(g * GW_C, (g + 1) * GW_C)
        y = ys[g]
        if combine:
            y = y + yo_ref[:, sl] + xbc_ref[:, sl] * dsk_ref[:, sl]
        o_ref[:, sl] = y

    @pl.when(s == n - 1)
    def _():
        sfin_ref[...] = s_scr[...]


def _ssd_lat_kernel(*refs, reverse, combine, n_pass_through):
    (pb_ref, x_ref, nb_ref, cw_ref, cb_ref, dtr_ref, dtc_ref, pc_ref, pr_ref, s0_ref) = refs[:10]
    rest = refs[10:]
    if combine:
        yo_ref, dsk_ref = rest[:2]
        rest = rest[2:]
    o_ref, s_scr, xp_ref, xbc_ref = rest[n_pass_through:]
    s, n = pl.program_id(1), pl.num_programs(1)
    t_len = x_ref.shape[0]

    @pl.when(s == 0)
    def _():
        s_scr[...] = s0_ref[...]

    blk = n - 1 - s if reverse else s
    for kk in range(COLS_PER_STEP):
        k = COLS_PER_STEP - 1 - kk if reverse else kk
        if k > 0:
            prev = x_ref[t_len - HALO:t_len, k - 1, :]
        else:
            prev = jnp.where(blk == 0, 0.0, pb_ref[:, COLS_PER_STEP - 1, :])
        if k < COLS_PER_STEP - 1:
            nxt = x_ref[0:HALO, k + 1, :]
        else:
            nxt = jnp.where(blk == n - 1, 0.0, nb_ref[:, 0, :])
        xbc_ref[...] = _dwconv_silu_vals(prev, x_ref[:, k, :], nxt, cw_ref, cb_ref, xp_ref)
        ys = _ssd_chunk(xbc_ref, dtr_ref[:, k * t_len:(k + 1) * t_len], dtc_ref[k * t_len:(k + 1) * t_len, :],
                        pc_ref, pr_ref, s_scr, reverse)
        for g in range(NG_C):
            sl = slice(g * GW_C, (g + 1) * GW_C)
            y = ys[g]
            if combine:
                y = y + yo_ref[:, k, sl] + xbc_ref[:, sl] * dsk_ref[:, sl]
            o_ref[:, k, sl] = y


def _ssd_params(conv_w, conv_b, a_log, dt_bias):
    return (jnp.pad(conv_w.astype(F32), ((0, HALO - CONV_W), (0, 0))), conv_b.reshape(1, XBC_W).astype(F32),
            jnp.stack([a_log, dt_bias], axis=1).astype(F32), jnp.stack([a_log, dt_bias], axis=0).astype(F32))


_SSD_PARAM_SPECS = [((HALO, XBC_W)), ((1, XBC_W)), ((H_C, 2)), ((2, H_C))]


def ssd_scan_ctx(p_all, lay, direction, dt_rows, dt_cols, prm, combine_args=None):
    bsz, n_lat, n_ctx = lay
    reverse = direction == 1
    t_len = SCAN_CHUNK
    n_chunks = n_ctx // t_len
    hb = t_len // HALO
    chunk = (lambda s: n_chunks - 1 - s) if reverse else (lambda s: s)
    row_blk = lambda b, s: _ctx_row(lay, b) // t_len + chunk(s)
    n_halo = p_all.shape[0] // HALO
    y_spec = pl.BlockSpec((t_len, W_C), lambda b, s: (row_blk(b, s), 0))
    in_specs = [pl.BlockSpec((HALO, XBC_W), lambda b, s: (jnp.maximum(row_blk(b, s) * hb - 1, 0), P_XBC // XBC_W)),
                pl.BlockSpec((t_len, XBC_W), lambda b, s: (row_blk(b, s), P_XBC // XBC_W)),
                pl.BlockSpec((HALO, XBC_W), lambda b, s: (jnp.minimum((row_blk(b, s) + 1) * hb, n_halo - 1),
                                                          P_XBC // XBC_W))]
    in_specs += [pl.BlockSpec(shape, lambda b, s: (0, 0)) for shape in _SSD_PARAM_SPECS]
    in_specs[5:5] = [pl.BlockSpec((None, None, H_C, t_len), lambda b, s: (b, direction, 0, chunk(s))),
                     pl.BlockSpec((None, None, t_len, H_C), lambda b, s: (b, direction, chunk(s), 0))]
    args = [p_all, p_all, p_all, prm[0], prm[1], dt_rows, dt_cols, prm[2], prm[3]]
    if combine_args is not None:
        y_other, d_skip = combine_args
        in_specs += [y_spec, pl.BlockSpec((1, W_C), lambda b, s: (0, 0))]
        args += [y_other, d_skip]
    st_spec = pl.BlockSpec((None, NG_C, N_C, GW_C), lambda b, s: (b, 0, 0, 0))
    return pl.pallas_call(
        functools.partial(_ssd_ctx_kernel, reverse=reverse, combine=combine_args is not None),
        grid=(bsz, n_chunks),
        in_specs=in_specs,
        out_specs=[y_spec, st_spec],
        out_shape=[jax.ShapeDtypeStruct((p_all.shape[0], W_C), F32),
                   jax.ShapeDtypeStruct((bsz, NG_C, N_C, GW_C), F32)],
        scratch_shapes=[pltpu.VMEM((NG_C, N_C, GW_C), F32), pltpu.VMEM((t_len + 2 * HALO, XBC_W), F32),
                        pltpu.VMEM((t_len, XBC_W), F32)],
        compiler_params=pltpu.CompilerParams(dimension_semantics=("arbitrary", "arbitrary"),
                                             vmem_limit_bytes=V7X_VMEM_LIMIT_BYTES),
    )(*args)


def ssd_scan_lat(p_all, lay, direction, dt_rows, dt_cols, prm, state0, y_prev, combine_args=None):
    bsz, n_lat, n_ctx = lay
    reverse = direction == 1
    t_len = n_lat // GRID_W
    n_blocks = GRID_W // COLS_PER_STEP
    assert t_len % HALO == 0 and GRID_W == COLS_PER_STEP * n_blocks
    groups = p_all.shape[0] // GRID_W
    view = lambda t: t.reshape(groups, n_blocks, COLS_PER_STEP, t.shape[1])
    blk = (lambda s: n_blocks - 1 - s) if reverse else (lambda s: s)
    hb = t_len // HALO
    y_spec = pl.BlockSpec((t_len, None, COLS_PER_STEP, W_C), lambda b, s: (b, blk(s), 0, 0))
    xcol = P_XBC // XBC_W
    in_specs = [pl.BlockSpec((HALO, None, COLS_PER_STEP, XBC_W),
                             lambda b, s: ((b + 1) * hb - 1, jnp.maximum(blk(s) - 1, 0), 0, xcol)),
                pl.BlockSpec((t_len, None, COLS_PER_STEP, XBC_W), lambda b, s: (b, blk(s), 0, xcol)),
                pl.BlockSpec((HALO, None, COLS_PER_STEP, XBC_W),
                             lambda b, s: (b * hb, jnp.minimum(blk(s) + 1, n_blocks - 1), 0, xcol))]
    in_specs += [pl.BlockSpec(shape, lambda b, s: (0, 0)) for shape in _SSD_PARAM_SPECS]
    step = COLS_PER_STEP * t_len
    in_specs[5:5] = [pl.BlockSpec((None, None, H_C, step), lambda b, s: (b, direction, 0, blk(s))),
                     pl.BlockSpec((None, None, step, H_C), lambda b, s: (b, direction, blk(s), 0))]
    st_spec = pl.BlockSpec((None, NG_C, N_C, GW_C), lambda b, s: (b, 0, 0, 0))
    in_specs.append(st_spec)
    p4 = view(p_all)
    args = [p4, p4, p4, prm[0], prm[1], dt_rows, dt_cols, prm[2], prm[3], state0]
    if combine_args is not None:
        y_other, d_skip = combine_args
        in_specs += [y_spec, pl.BlockSpec((1, W_C), lambda b, s: (0, 0))]
        args += [view(y_other), d_skip]
    aliases = {len(args): 0}
    in_specs.append(pl.BlockSpec(memory_space=pl.ANY))
    args.append(view(y_prev))
    y4 = pl.pallas_call(
        functools.partial(_ssd_lat_kernel, reverse=reverse, combine=combine_args is not None, n_pass_through=1),
        grid=(bsz, n_blocks),
        in_specs=in_specs,
        out_specs=y_spec,
        out_shape=jax.ShapeDtypeStruct((groups, n_blocks, COLS_PER_STEP, W_C), F32),
        input_output_aliases=aliases,
        scratch_shapes=[pltpu.VMEM((NG_C, N_C, GW_C), F32), pltpu.VMEM((t_len + 2 * HALO, XBC_W), F32),
                        pltpu.VMEM((t_len, XBC_W), F32)],
        compiler_params=pltpu.CompilerParams(dimension_semantics=("arbitrary", "arbitrary"),
                                             vmem_limit_bytes=V7X_VMEM_LIMIT_BYTES),
    )(*args)
    return y4.reshape(p_all.shape[0], W_C)


def ssd_mixer_inplace(p_all, lay, conv_w, conv_b, a_log, dt_bias, d_skip):
    bsz, n_lat, n_ctx = lay
    grid_rows = n_lat // GRID_W

    def dt_layouts(t, n, colmajor):
        t = t[:, P_DT:P_DT + 2 * H_C].reshape(bsz, n, 2, H_C)
        if colmajor:
            t = t.reshape(bsz, grid_rows, GRID_W, 2, H_C).swapaxes(1, 2).reshape(bsz, n, 2, H_C)
        return t.transpose(0, 2, 3, 1), t.transpose(0, 2, 1, 3)

    dtr_c, dtc_c = dt_layouts(p_all[bsz * n_lat:], n_ctx, False)
    dtr_l, dtc_l = dt_layouts(p_all[:bsz * n_lat], n_lat, True)
    d_row = jnp.repeat(d_skip.astype(F32), HD_C).reshape(1, W_C)
    y = {}
    for direction in (1, 0):
        prm = _ssd_params(conv_w, conv_b, a_log[direction], dt_bias[direction])
        combine = None if direction == 1 else (y[1], d_row)
        y_ctx, state = ssd_scan_ctx(p_all, lay, direction, dtr_c, dtc_c, prm, combine)
        y[direction] = ssd_scan_lat(p_all, lay, direction, dtr_l, dtc_l, prm, state, y_ctx, combine)
    return y[0]


TRUNK_TM = 512
P_COLS = 7168
P_XM, P_ZM, P_U, P_ZS, P_XBC, P_GD, P_DT = 0, W_A, 2 * W_A, 2 * W_A + W_B, 2 * W_A + W_B + W_C, 6144, 6656


def _lat_row(lay, b):
    bsz, n_lat, n_ctx = lay
    return b * n_lat


def _ctx_row(lay, b):
    bsz, n_lat, n_ctx = lay
    return bsz * n_lat + b * n_ctx


def _seq_block(lay, b, j, t_len):
    bsz, n_lat, n_ctx = lay
    n_cb = n_ctx // t_len
    return jnp.where(j < n_cb, (bsz * n_lat + b * n_ctx) // t_len + j, (b * n_lat) // t_len + (j - n_cb))


def _mod_index(lay, tm):
    bsz, n_lat, n_ctx = lay
    assert n_lat % tm == 0 and (bsz * n_ctx) % tm == 0
    return lambda i: jnp.where(i < (bsz * n_lat) // tm, 1 + i // (n_lat // tm), 0)


def _permute_w_in(w):
    xm_zm_u_zs_xbc, dt, gd = w[:, :SPLIT_IDX[4]], w[:, SPLIT_IDX[4]:SPLIT_IDX[5]], w[:, SPLIT_IDX[5]:]
    pad = jnp.zeros((w.shape[0], P_COLS - IN_COLS), w.dtype)
    return jnp.concatenate([xm_zm_u_zs_xbc, gd, dt, pad], axis=1).astype(BF16)


def _modulated_norm(x, g, shift, scale):
    xn = x * lax.rsqrt(jnp.mean(x * x, axis=-1, keepdims=True) + EPS) * g
    return xn * (1.0 + scale) + shift


def _win_kernel(x_ref, g_ref, mod_ref, w_ref, o_ref, h_ref):
    @pl.when(pl.program_id(1) == 0)
    def _():
        h_ref[...] = _modulated_norm(x_ref[...], g_ref[...], mod_ref[0, 0:1, :], mod_ref[0, 1:2, :]).astype(BF16)

    o_ref[...] = jnp.dot(h_ref[...], w_ref[...], preferred_element_type=F32)


def in_projection(x, g, mods, w_p, lay, tn=1024):
    m, d = x.shape
    tm = TRUNK_TM
    mod_idx = _mod_index(lay, tm)
    single = pl.Buffered(1)
    return pl.pallas_call(
        _win_kernel,
        grid=(m // tm, P_COLS // tn),
        in_specs=[pl.BlockSpec((tm, d), lambda i, j: (i, 0)),
                  pl.BlockSpec((1, d), lambda i, j: (0, 0), pipeline_mode=single),
                  pl.BlockSpec((1, MOD_ROWS, d), lambda i, j: (mod_idx(i), 0, 0), pipeline_mode=single),
                  pl.BlockSpec((d, tn), lambda i, j: (0, j))],
        out_specs=pl.BlockSpec((tm, tn), lambda i, j: (i, j)),
        out_shape=jax.ShapeDtypeStruct((m, P_COLS), F32),
        scratch_shapes=[pltpu.VMEM((tm, d), BF16)],
        compiler_params=pltpu.CompilerParams(dimension_semantics=("arbitrary", "arbitrary"),
                                             vmem_limit_bytes=V7X_VMEM_LIMIT_BYTES),
    )(x, g.reshape(1, d).astype(F32), mods, w_p)


def _merge_kernel(gd_ref, ya_ref, y1_ref, y2_ref, yc_ref, zs_ref, ng_ref, gluw_ref, glub_ref, gw_ref, gb_ref, bw_ref,
                  o_ref, gds_ref, ys_ref):
    @pl.when(pl.program_id(1) == 0)
    def _():
        gds_ref[...] = gd_ref[...].astype(BF16)
        ys_ref[0] = ya_ref[...].astype(BF16)
        yb = jax.nn.gelu(y1_ref[...] + y2_ref[...])
        gate = jnp.dot(yb.astype(BF16), gluw_ref[...], preferred_element_type=F32) + glub_ref[...]
        ys_ref[1] = (yb * jax.nn.sigmoid(gate)).astype(BF16)
        zs = zs_ref[...]
        yg = yc_ref[...] * (zs * jax.nn.sigmoid(zs))
        yg = yg * lax.rsqrt(jnp.mean(yg * yg, axis=-1, keepdims=True) + EPS) * ng_ref[...]
        ys_ref[2] = yg.astype(BF16)

    acc = None
    for i in range(N_BRANCH):
        gate = jax.nn.sigmoid(jnp.dot(gds_ref[...], gw_ref[i], preferred_element_type=F32) + gb_ref[i])
        term = gate * jnp.dot(ys_ref[i], bw_ref[i], preferred_element_type=F32)
        acc = term if acc is None else acc + term
    o_ref[...] = acc.astype(o_ref.dtype)


def merge_projection(p_all, ya, y1, y2, yc, ssd_norm_g, glu_w, glu_b, gate_w, gate_b, branch_w, tn=1024):
    m = p_all.shape[0]
    d = gate_w.shape[-1]
    tm = TRUNK_TM
    single = pl.Buffered(1)
    row = lambda w: pl.BlockSpec((tm, w), lambda i, j: (i, 0))
    return pl.pallas_call(
        _merge_kernel,
        grid=(m // tm, d // tn),
        in_specs=[pl.BlockSpec((tm, R_GATE), lambda i, j: (i, P_GD // R_GATE)),
                  row(W_A), row(W_B), row(W_B), row(W_C),
                  pl.BlockSpec((tm, W_C), lambda i, j: (i, P_ZS // W_C)),
                  pl.BlockSpec((1, W_C), lambda i, j: (0, 0), pipeline_mode=single),
                  pl.BlockSpec((W_B, W_B), lambda i, j: (0, 0), pipeline_mode=single),
                  pl.BlockSpec((1, W_B), lambda i, j: (0, 0), pipeline_mode=single),
                  pl.BlockSpec((N_BRANCH, R_GATE, tn), lambda i, j: (0, 0, j)),
                  pl.BlockSpec((N_BRANCH, 1, tn), lambda i, j: (0, 0, j)),
                  pl.BlockSpec((N_BRANCH, W_MIX, tn), lambda i, j: (0, 0, j))],
        out_specs=pl.BlockSpec((tm, tn), lambda i, j: (i, j)),
        out_shape=jax.ShapeDtypeStruct((m, d), BF16),
        scratch_shapes=[pltpu.VMEM((tm, R_GATE), BF16), pltpu.VMEM((N_BRANCH, tm, W_MIX), BF16)],
        compiler_params=pltpu.CompilerParams(dimension_semantics=("arbitrary", "arbitrary"),
                                             vmem_limit_bytes=V7X_VMEM_LIMIT_BYTES),
    )(p_all, ya, y1, y2, yc, p_all, ssd_norm_g.reshape(1, W_C).astype(F32), glu_w.astype(BF16),
      glu_b.reshape(1, W_B).astype(F32),
      gate_w.astype(BF16), gate_b.reshape(N_BRANCH, 1, d).astype(F32), branch_w.astype(BF16))


def _wout_kernel(t_ref, w_ref, x_ref, mod_ref, o_ref):
    y = jnp.dot(t_ref[...], w_ref[...], preferred_element_type=F32)
    o_ref[...] = x_ref[...] + mod_ref[0, 2:3, :] * y


def out_projection(t, w_out_bf, x, mods, lay, tn=1024):
    m, d = x.shape
    tm = TRUNK_TM
    mod_idx = _mod_index(lay, tm)
    return pl.pallas_call(
        _wout_kernel,
        grid=(m // tm, d // tn),
        in_specs=[pl.BlockSpec((tm, d), lambda i, j: (i, 0)),
                  pl.BlockSpec((d, tn), lambda i, j: (0, j)),
                  pl.BlockSpec((tm, tn), lambda i, j: (i, j)),
                  pl.BlockSpec((1, MOD_ROWS, tn), lambda i, j: (mod_idx(i), 0, j))],
        out_specs=pl.BlockSpec((tm, tn), lambda i, j: (i, j)),
        out_shape=jax.ShapeDtypeStruct((m, d), F32),
        compiler_params=pltpu.CompilerParams(dimension_semantics=("arbitrary", "arbitrary"),
                                             vmem_limit_bytes=V7X_VMEM_LIMIT_BYTES),
    )(t, w_out_bf, x, mods)


def _mlp_kernel(x_ref, g_ref, mod_ref, w1_ref, w2_ref, o_ref, h_ref):
    f = pl.program_id(1)

    @pl.when(f == 0)
    def _():
        x = x_ref[...]
        xn = x * lax.rsqrt(jnp.mean(x * x, axis=-1, keepdims=True) + EPS) * g_ref[...]
        h = xn * (1.0 + mod_ref[0, 4:5, :]) + mod_ref[0, 3:4, :]
        h_ref[...] = h.astype(BF16)
        o_ref[...] = jnp.zeros_like(o_ref)

    a = jnp.dot(h_ref[...], w1_ref[...], preferred_element_type=F32)
    a = jnp.square(jnp.maximum(a, 0.0)).astype(BF16)
    o_ref[...] += jnp.dot(a, w2_ref[...], preferred_element_type=F32)

    @pl.when(f == pl.num_programs(1) - 1)
    def _():
        o_ref[...] = x_ref[...] + mod_ref[0, 5:6, :] * o_ref[...]


def mlp_block(x, g, mod, w1, w2, lay, tf=512):
    m, d = x.shape
    dff = w1.shape[1]
    tm = TRUNK_TM
    assert m % tm == 0 and dff % tf == 0
    mod_idx = _mod_index(lay, tm)
    single = pl.Buffered(1)
    return pl.pallas_call(
        _mlp_kernel,
        grid=(m // tm, dff // tf),
        in_specs=[
            pl.BlockSpec((tm, d), lambda i, f: (i, 0), pipeline_mode=single),
            pl.BlockSpec((1, d), lambda i, f: (0, 0), pipeline_mode=single),
            pl.BlockSpec((1, MOD_ROWS, d), lambda i, f: (mod_idx(i), 0, 0), pipeline_mode=single),
            pl.BlockSpec((d, tf), lambda i, f: (0, f)),
            pl.BlockSpec((tf, d), lambda i, f: (f, 0)),
        ],
        out_specs=pl.BlockSpec((tm, d), lambda i, f: (i, 0), pipeline_mode=single),
        out_shape=jax.ShapeDtypeStruct((m, d), F32),
        scratch_shapes=[pltpu.VMEM((tm, d), BF16)],
        compiler_params=pltpu.CompilerParams(
            dimension_semantics=("arbitrary", "arbitrary"),
            vmem_limit_bytes=V7X_VMEM_LIMIT_BYTES),
    )(x, g.reshape(1, d), mod, w1, w2)


def _pad_mod(mod):
    nb = mod.shape[0]
    mod = mod.reshape(nb, N_MOD, D_MODEL)
    return jnp.pad(mod, ((0, 0), (0, MOD_ROWS - N_MOD), (0, 0)))


def kernel(x, c, ctx, c_ctx, norm1_g, norm2_g, final_g, mod_down, mod_up, mod_b, w_in,
           m_conv_w, m_conv_b, m_wq, m_wk, m_wv, m_gate_w, m_gate_b, m_norm_g, m_skip,
           s5_lam_re, s5_lam_im, s5_log_dt, s5_b_re, s5_b_im, s5_c_re, s5_c_im, s5_d, s5_glu_w, s5_glu_b,
           ssd_conv_w, ssd_conv_b, ssd_a_log, ssd_dt_bias, ssd_d, ssd_norm_g,
           gate_w, gate_b, branch_w, w_out, mlp_w1, mlp_w2):
    bsz, n_lat, d = x.shape
    n_ctx = ctx.shape[1]
    lay = (bsz, n_lat, n_ctx)
    grid_rows = n_lat // GRID_W
    xs = jnp.concatenate([x.reshape(bsz * n_lat, d), ctx.reshape(bsz * n_ctx, d)], axis=0)
    for l in range(DEPTH):
        mods = _pad_mod(adaln(jnp.concatenate([c_ctx[None], c], axis=0), mod_down[l], mod_up[l], mod_b[l]))
        p_all = in_projection(xs, norm1_g[l], mods, _permute_w_in(w_in[l]), lay)
        ya = mlstm_mixer(p_all, lay, m_conv_w[l], m_conv_b[l], m_wq[l], m_wk[l], m_wv[l],
                         m_gate_w[l], m_gate_b[l], m_norm_g[l], m_skip[l])
        yb1, yb2 = s5_mixer(p_all, P_U // LANES, lay, s5_lam_re[l], s5_lam_im[l], s5_log_dt[l], s5_b_re[l], s5_b_im[l],
                            s5_c_re[l], s5_c_im[l], s5_d[l])
        yc = ssd_mixer_inplace(p_all, lay, ssd_conv_w[l], ssd_conv_b[l], ssd_a_log[l], ssd_dt_bias[l], ssd_d[l])
        t = merge_projection(p_all, ya, yb1, yb2, yc, ssd_norm_g[l], s5_glu_w[l], s5_glu_b[l], gate_w[l], gate_b[l],
                             branch_w[l])
        xs = out_projection(t, w_out[l].astype(BF16), xs, mods, lay)
        xs = mlp_block(xs, norm2_g[l], mods, mlp_w1[l].astype(BF16), mlp_w2[l].astype(BF16), lay)
    return rmsnorm(xs[:bsz * n_lat].reshape(bsz, n_lat, d), final_g)
```

```python
import functools
import math
import jax
import jax.numpy as jnp
from jax import lax
import numpy as np
from jax.experimental import pallas as pl
from jax.experimental.pallas import tpu as pltpu

D_MODEL = 4096
BATCH = 2
SEQ = 8192
DEPTH = 4

CTX_LEN = 256
GRID_W = 64
W_MIX = D_MODEL // 4
W_A = W_MIX
H_A = 8
DH_A = W_A // H_A
QKV_BLOCK = 4
MLSTM_CHUNK = 64
W_B = W_MIX
S5_GROUP = 16
G_B = W_B // S5_GROUP
P_B = 64
W_C = W_MIX
HD_C = 64
H_C = W_C // HD_C
NG_C = 4
N_C = 128
SSD_CHUNK = 128
CONV_W = 5
XBC_W = W_C + 2 * NG_C * N_C
D_FF = 4 * D_MODEL
R_MOD = 512
R_GATE = 512
N_BRANCH = 3
N_MOD = 6
SPLIT_IDX = (W_A, 2 * W_A, 2 * W_A + W_B, 2 * W_A + W_B + W_C, 2 * W_A + W_B + W_C + XBC_W, 2 * W_A + W_B + W_C + XBC_W + 2 * H_C)
IN_COLS = 2 * W_A + W_B + W_C + XBC_W + 2 * H_C + R_GATE
EPS = 1e-6
F32 = jnp.float32
BF16 = jnp.bfloat16

V7X_VMEM_LIMIT_BYTES = 56 * 1024 * 1024
MOD_ROWS = 8
LANES = 128


def rmsnorm(x, g):
    xf = x.astype(F32)
    xf = xf * lax.rsqrt(jnp.mean(xf * xf, axis=-1, keepdims=True) + EPS)
    return xf.astype(x.dtype) * g


def adaln(cvec, down, up, b):
    return (jax.nn.silu(cvec) @ down) @ up + b


SCAN_CHUNK = 128


def _cumsum_incl(x, axis, reverse):
    n = x.shape[axis]
    idx = lax.broadcasted_iota(jnp.int32, x.shape, axis)
    s = 1
    while s < n:
        if reverse:
            x = x + jnp.where(idx < n - s, pltpu.roll(x, n - s, axis), 0.0)
        else:
            x = x + jnp.where(idx >= s, pltpu.roll(x, s, axis), 0.0)
        s *= 2
    return x


def _scan_chunk_index(s, n_ctx_chunks, n_chunks, reverse):
    if not reverse:
        return s
    return jnp.where(s < n_ctx_chunks, n_ctx_chunks - 1 - s, n_chunks - 1 - (s - n_ctx_chunks))


def _mlstm_kernel(*refs, reverse, readout, gate_col0):
    if readout:
        (q_ref, k_ref, v_ref, gr_ref, gc_ref, hb_ref, xc_ref, zm_ref, ng_ref, sk_ref, o_ref, c_scr, n_scr, m_scr) = refs
    else:
        (q_ref, k_ref, v_ref, gr_ref, gc_ref, o_ref, c_scr, n_scr, m_scr) = refs
    t_len = q_ref.shape[0]

    @pl.when(pl.program_id(1) == 0)
    def _():
        c_scr[...] = jnp.zeros_like(c_scr)
        n_scr[...] = jnp.zeros_like(n_scr)
        m_scr[...] = jnp.zeros_like(m_scr)

    b_rows = _cumsum_incl(gr_ref[H_A:, :], 1, reverse)
    b_cols = _cumsum_incl(gc_ref[:, gate_col0 + H_A:gate_col0 + 2 * H_A], 0, reverse)
    i_rows = gr_ref[:H_A, :]
    i_cols = gc_ref[:, gate_col0:gate_col0 + H_A]
    row_id = lax.broadcasted_iota(jnp.int32, (t_len, t_len), 0)
    col_id = lax.broadcasted_iota(jnp.int32, (t_len, t_len), 1)
    allowed = (col_id >= row_id) if reverse else (col_id <= row_id)
    end = 0 if reverse else t_len - 1
    m_all = m_scr[...]
    n_all = n_scr[...]
    for h in range(H_A):
        sl = slice(h * DH_A, (h + 1) * DH_A)
        q = q_ref[:, sl]
        k = k_ref[:, sl]
        v = v_ref[:, sl]
        b_row, b_col = b_rows[h:h + 1, :], b_cols[:, h:h + 1]
        i_row, i_col = i_rows[h:h + 1, :], i_cols[:, h:h + 1]
        b_end = b_row[:, end:end + 1]
        m_st = m_all[h:h + 1, 0:1]
        n_st = n_all[h:h + 1, :]
        c_st = c_scr[h]
        d_log = jnp.where(allowed, b_col - b_row + i_row, -jnp.inf)
        inter = b_col + m_st
        m_t = jnp.maximum(inter, jnp.max(d_log, axis=-1, keepdims=True))
        w_carry = jnp.exp(inter - m_t)
        qb, kb, vb = q.astype(BF16), k.astype(BF16), v.astype(BF16)
        qk = lax.dot_general(qb, kb, (((1,), (1,)), ((), ())), preferred_element_type=F32)
        s = qk * jnp.exp(d_log - m_t)
        num = (jnp.dot(s.astype(BF16), vb, preferred_element_type=F32)
               + w_carry * jnp.dot(qb, c_st.astype(BF16), preferred_element_type=F32))
        den = jnp.sum(s, axis=-1, keepdims=True) + w_carry * jnp.sum(q * n_st, axis=-1, keepdims=True)
        hh = num / jnp.maximum(jnp.abs(den), jnp.exp(-m_t))
        log_ws = b_end - b_col + i_col
        m_new = jnp.maximum(b_end + m_st, jnp.max(log_ws, axis=0, keepdims=True))
        kw = k * jnp.exp(log_ws - m_new)
        w_prev = jnp.exp(b_end + m_st - m_new)
        c_scr[h] = w_prev * c_st + lax.dot_general(kw.astype(BF16), vb, (((0,), (0,)), ((), ())),
                                                   preferred_element_type=F32)
        n_scr[h:h + 1, :] = w_prev * n_st + jnp.sum(kw, axis=0, keepdims=True)
        m_scr[h:h + 1, :] = jnp.broadcast_to(m_new, (1, DH_A))
        if readout:
            hh = hh + hb_ref[:, sl]
            mu = jnp.mean(hh, axis=-1, keepdims=True)
            var = jnp.mean(jnp.square(hh - mu), axis=-1, keepdims=True)
            hn = (hh - mu) * lax.rsqrt(var + EPS)
            zm = zm_ref[:, sl]
            o_ref[:, sl] = (hn * ng_ref[:, sl] + sk_ref[:, sl] * xc_ref[:, sl]) * (zm * jax.nn.sigmoid(zm))
        else:
            o_ref[:, sl] = hh


def mlstm_scan(q, k, v, g_rows, g_cols, lay, direction, readout_args=None):
    bsz, n_lat, n_ctx = lay
    w = q.shape[1]
    reverse = direction == 1
    t_len = SCAN_CHUNK
    assert n_lat % t_len == 0 and n_ctx % t_len == 0
    n_chunks, n_ctx_chunks = (n_lat + n_ctx) // t_len, n_ctx // t_len
    order = functools.partial(_scan_chunk_index, n_ctx_chunks=n_ctx_chunks, n_chunks=n_chunks, reverse=reverse)
    blk = lambda b, s: _seq_block(lay, b, order(s), t_len)
    tok = pl.BlockSpec((t_len, w), lambda b, s: (blk(b, s), 0))
    in_specs = [tok, tok, tok,
                pl.BlockSpec((2 * H_A, t_len), lambda b, s: (direction, blk(b, s))),
                pl.BlockSpec((t_len, 4 * H_A), lambda b, s: (blk(b, s), 0))]
    args = [q, k, v, g_rows, g_cols]
    if readout_args is not None:
        h_other, xc, p_all, norm_g, skip = readout_args
        vec = pl.BlockSpec((1, w), lambda b, s: (0, 0))
        in_specs += [tok, tok, pl.BlockSpec((t_len, w), lambda b, s: (blk(b, s), P_ZM // w)), vec, vec]
        args += [h_other, xc, p_all, norm_g.reshape(1, w).astype(F32), skip.reshape(1, w).astype(F32)]
    return pl.pallas_call(
        functools.partial(_mlstm_kernel, reverse=reverse, readout=readout_args is not None,
                          gate_col0=2 * H_A * direction),
        grid=(bsz, n_chunks),
        in_specs=in_specs,
        out_specs=tok,
        out_shape=jax.ShapeDtypeStruct(q.shape, F32),
        scratch_shapes=[pltpu.VMEM((H_A, DH_A, DH_A), F32), pltpu.VMEM((H_A, DH_A), F32), pltpu.VMEM((H_A, DH_A), F32)],
        compiler_params=pltpu.CompilerParams(
            dimension_semantics=("arbitrary", "arbitrary"),
            vmem_limit_bytes=V7X_VMEM_LIMIT_BYTES),
    )(*args)


FEAT_TILE = 256
HALO = 8
N_LB = W_A // LANES


def _seq_edges(lay, i, tile):
    bsz, n_lat, n_ctx = lay
    lat_tiles, ctx_tiles, base = n_lat // tile, n_ctx // tile, (bsz * n_lat) // tile
    pos = jnp.where(i < base, i % lat_tiles, (i - base) % ctx_tiles)
    n = jnp.where(i < base, lat_tiles, ctx_tiles)
    return pos == 0, pos == n - 1


def _dwconv_silu(prev_ref, cur_ref, next_ref, w_ref, b_ref, xp_ref, first, last):
    tile = cur_ref.shape[0]
    xp_ref[0:HALO, :] = jnp.where(first, 0.0, prev_ref[...])
    xp_ref[HALO:HALO + tile, :] = cur_ref[...]
    xp_ref[HALO + tile:, :] = jnp.where(last, 0.0, next_ref[...])
    acc = b_ref[...] + w_ref[0:1, :] * xp_ref[pl.ds(HALO - CONV_W // 2, tile), :]
    for j in range(1, CONV_W):
        acc = acc + w_ref[j:j + 1, :] * xp_ref[pl.ds(HALO - CONV_W // 2 + j, tile), :]
    return acc * jax.nn.sigmoid(acc)


def _split_dot(x, w_hi, w_lo):
    x_hi = x.astype(BF16)
    x_lo = (x - x_hi.astype(F32)).astype(BF16)
    return (jnp.dot(x_hi, w_hi, preferred_element_type=F32) + jnp.dot(x_lo, w_hi, preferred_element_type=F32)
            + jnp.dot(x_hi, w_lo, preferred_element_type=F32))


def _mlstm_feat_kernel(prev_ref, cur_ref, next_ref, cw_ref, cb_ref, wqk_hi_ref, wqk_lo_ref, wv_hi_ref, wv_lo_ref,
                       gw_ref, gwt_ref, gbc_ref, gbr_ref, xc_ref, q_ref, k_ref, v_ref, gc_ref, gr_ref,
                       xp_ref, qkv_ref, *, lay):
    tile = cur_ref.shape[0]
    first, last = _seq_edges(lay, pl.program_id(0), tile)
    xc = _dwconv_silu(prev_ref, cur_ref, next_ref, cw_ref, cb_ref, xp_ref, first, last)
    xc_ref[...] = xc
    for lb in range(N_LB):
        sl = slice(lb * LANES, (lb + 1) * LANES)
        qk = _split_dot(xc[:, sl], wqk_hi_ref[lb], wqk_lo_ref[lb])
        v = _split_dot(cur_ref[:, sl], wv_hi_ref[lb], wv_lo_ref[lb])
        q_ref[:, sl] = qk[:, :LANES] * DH_A ** -0.5
        k_ref[:, sl] = qk[:, LANES:]
        v_ref[:, sl] = v
        qkv_ref[:, sl] = qk[:, :LANES].astype(BF16)
        qkv_ref[:, W_A + lb * LANES:W_A + (lb + 1) * LANES] = qk[:, LANES:].astype(BF16)
        qkv_ref[:, 2 * W_A + lb * LANES:2 * W_A + (lb + 1) * LANES] = v.astype(BF16)
    qkv = qkv_ref[...]
    g_cols = jnp.dot(qkv, gw_ref[...], preferred_element_type=F32) + gbc_ref[...]
    g_rows = lax.dot_general(gwt_ref[...], qkv, (((1,), (1,)), ((), ())), preferred_element_type=F32) + gbr_ref[...]
    is_f_c = (lax.broadcasted_iota(jnp.int32, g_cols.shape, 1) % (2 * H_A)) >= H_A
    is_f_r = (lax.broadcasted_iota(jnp.int32, g_rows.shape, 0) % (2 * H_A)) >= H_A
    gc_ref[...] = jnp.where(is_f_c, -_softplus(-g_cols), g_cols)
    gr_ref[...] = jnp.where(is_f_r, -_softplus(-g_rows), g_rows)


def _blockdiag_dense(w):
    per = LANES // QKV_BLOCK
    w = w.astype(F32).reshape(N_LB, per, QKV_BLOCK, QKV_BLOCK)
    eye = jnp.eye(per, dtype=F32)
    return (w[:, :, :, None, :] * eye[None, :, None, :, None]).reshape(N_LB, LANES, LANES)


def _hi_lo(w):
    hi = w.astype(BF16)
    return hi, (w - hi.astype(F32)).astype(BF16)


def mlstm_features(p_all, lay, conv_w, conv_b, wq, wk, wv, gate_w, gate_b):
    rows = p_all.shape[0]
    tile = FEAT_TILE
    hb = tile // HALO
    n_halo_blocks = rows // HALO
    wqk = jnp.concatenate([_blockdiag_dense(wq), _blockdiag_dense(wk)], axis=2)
    wqk_hi, wqk_lo = _hi_lo(wqk)
    wv_hi, wv_lo = _hi_lo(_blockdiag_dense(wv))
    gw = gate_w.astype(F32).transpose(1, 0, 2).reshape(3 * W_A, 4 * H_A)
    gb = gate_b.astype(F32).reshape(1, 4 * H_A)
    cw = jnp.pad(conv_w.astype(F32), ((0, HALO - CONV_W), (0, 0)))
    tok = pl.BlockSpec((tile, W_A), lambda i: (i, 0))
    full = lambda *shape: pl.BlockSpec(shape, lambda i: (0,) * len(shape))
    return pl.pallas_call(
        functools.partial(_mlstm_feat_kernel, lay=lay),
        grid=(rows // tile,),
        in_specs=[pl.BlockSpec((HALO, W_A), lambda i: (jnp.maximum(i * hb - 1, 0), P_XM // W_A)),
                  pl.BlockSpec((tile, W_A), lambda i: (i, P_XM // W_A)),
                  pl.BlockSpec((HALO, W_A), lambda i: (jnp.minimum((i + 1) * hb, n_halo_blocks - 1), P_XM // W_A)),
                  full(HALO, W_A), full(1, W_A),
                  full(N_LB, LANES, 2 * LANES), full(N_LB, LANES, 2 * LANES),
                  full(N_LB, LANES, LANES), full(N_LB, LANES, LANES),
                  full(3 * W_A, 4 * H_A), full(4 * H_A, 3 * W_A), full(1, 4 * H_A), full(4 * H_A, 1)],
        out_specs=[tok, tok, tok, tok,
                   pl.BlockSpec((tile, 4 * H_A), lambda i: (i, 0)),
                   pl.BlockSpec((4 * H_A, tile), lambda i: (0, i))],
        out_shape=[jax.ShapeDtypeStruct((rows, W_A), F32)] * 4
        + [jax.ShapeDtypeStruct((rows, 4 * H_A), F32), jax.ShapeDtypeStruct((4 * H_A, rows), F32)],
        scratch_shapes=[pltpu.VMEM((tile + 2 * HALO, W_A), F32), pltpu.VMEM((tile, 3 * W_A), BF16)],
        compiler_params=pltpu.CompilerParams(dimension_semantics=("arbitrary",),
                                             vmem_limit_bytes=V7X_VMEM_LIMIT_BYTES),
    )(p_all, p_all, p_all, cw, conv_b.reshape(1, W_A).astype(F32), wqk_hi, wqk_lo, wv_hi, wv_lo,
      gw.astype(BF16), gw.T.astype(BF16), gb, gb.reshape(4 * H_A, 1))


def mlstm_mixer(p_all, lay, conv_w, conv_b, wq, wk, wv, gate_w, gate_b, norm_g, skip):
    xc, q, k, v, g_cols, g_rows = mlstm_features(p_all, lay, conv_w, conv_b, wq, wk, wv, gate_w, gate_b)
    h_bwd = mlstm_scan(q, k, v, g_rows, g_cols, lay, 1)
    return mlstm_scan(q, k, v, g_rows, g_cols, lay, 0, readout_args=(h_bwd, xc, p_all, norm_g, skip))


S5_T = 8
S5_GPB = LANES // S5_GROUP
S5_NB = W_B // LANES
S5_NS = S5_GPB * P_B
S5_CW = S5_T * LANES


def _s5_tables(lam_re, lam_im, log_dt, b_re, b_im, c_re, c_im, n_rows_list):
    lam = lax.complex(lam_re.astype(F32), lam_im.astype(F32))
    ldt = lam * jnp.exp(log_dt.astype(F32))[..., None]
    lam_bar = jnp.exp(ldt)
    b_bar = ((lam_bar - 1) / lam)[..., None] * lax.complex(b_re.astype(F32), b_im.astype(F32))
    c_mat = lax.complex(c_re.astype(F32), c_im.astype(F32))
    steps = jnp.arange(S5_T + 1, dtype=F32)
    pw = jnp.exp(steps[None, :, None, None] * ldt[:, None])
    g_local = jnp.arange(G_B) % S5_GPB
    mask_y = (jnp.arange(LANES)[None, :] // S5_GROUP == g_local[:, None]).astype(F32)
    mask_x = (jnp.arange(S5_NS)[None, :] // P_B == g_local[:, None]).astype(F32)
    c_til = jnp.tile(c_mat, (1, 1, S5_GPB, 1))

    def blocks(t):
        return t.reshape(S5_NB, S5_GPB, *t.shape[1:]).transpose(0, 2, 1, 3, 4, 5)

    kk = jnp.real(jnp.einsum('dgyp,djgp,dgpe->djgey', c_til, pw[:, :S5_T], b_bar)) * mask_y[None, None, :, None, :]
    tt = jnp.arange(S5_T)
    diff = tt[None, :] - tt[:, None]
    kf = kk[0][jnp.clip(diff, 0, S5_T - 1)] * (diff >= 0)[..., None, None, None].astype(F32)
    kb = kk[1][jnp.clip(-diff, 0, S5_T - 1)] * (diff <= 0)[..., None, None, None].astype(F32)
    kt = (kf + kb).astype(BF16).transpose(2, 0, 3, 1, 4)
    m_toep = blocks(kt).reshape(S5_NB, S5_CW, S5_CW)

    def w_in(d, powers):
        w = jnp.tile(powers, (1, 1, S5_GPB))[:, :, None, :] * jnp.tile(b_bar[d], (1, S5_GPB, 1)).transpose(0, 2, 1)[None]
        w = w * mask_x[None, :, None, :]
        w = jnp.stack([jnp.real(w), jnp.imag(w)], axis=0).astype(BF16)
        w = blocks(w.transpose(2, 1, 3, 0, 4))
        return w.reshape(S5_NB, S5_CW, 2 * S5_NS)

    def w_out(d, powers):
        w = c_til[d][None] * powers[:, :, None, :] * mask_y[None, :, :, None]
        w = jnp.stack([jnp.real(w), -jnp.imag(w)], axis=0).astype(BF16)
        w = blocks(w.transpose(2, 0, 4, 1, 3))
        return w.reshape(S5_NB, 2 * S5_NS, S5_CW)

    we_f = w_in(0, pw[0, S5_T - 1::-1][:S5_T])
    we_b = w_in(1, pw[1, :S5_T])
    wo_f = w_out(0, pw[0, 1:])
    wo_b = w_out(1, pw[1, S5_T:0:-1])

    def rows(t):
        n = t.shape[0]
        t = t.reshape(n, S5_NB, S5_NS).transpose(1, 0, 2)
        return jnp.stack([jnp.real(t), jnp.imag(t)], axis=1)

    weights = dict(m_toep=m_toep.astype(BF16), we_f=we_f.astype(BF16), we_b=we_b.astype(BF16),
                   wo_f=wo_f.astype(BF16), wo_b=wo_b.astype(BF16))

    def scan_tables(n_rows):
        n_lv = max(1, int(math.log2(n_rows)))
        lv_steps = S5_T * (2.0 ** jnp.arange(n_lv, dtype=F32))
        jj = jnp.arange(n_rows, dtype=F32)
        tabs = []
        for d in range(2):
            lv = jnp.exp(lv_steps[:, None, None] * ldt[d][None])
            expo = jj if d == 0 else (n_rows - 1 - jj)
            pt = jnp.exp((S5_T * expo)[:, None, None] * ldt[d][None])
            tabs.append((rows(lv), rows(pt)))
        return dict(weights, lv_f=tabs[0][0], pt_f=tabs[0][1], lv_b=tabs[1][0], pt_b=tabs[1][1])

    return [scan_tables(n) for n in n_rows_list]


def _s5_state_scan(e, lv_ref, pt_ref, h0, reverse):
    n_rows = e.shape[0]
    row = lax.broadcasted_iota(jnp.int32, (n_rows, S5_NS), 0)

    def shift(x, s):
        if reverse:
            return jnp.where(row < n_rows - s, pltpu.roll(x, n_rows - s, 0), 0.0)
        return jnp.where(row >= s, pltpu.roll(x, s, 0), 0.0)

    e_re, e_im = e[:, :S5_NS], e[:, S5_NS:]
    z_re, z_im = shift(e_re, 1), shift(e_im, 1)
    k = 0
    while (1 << k) < n_rows:
        a_re, a_im = lv_ref[0, 0, k:k + 1, :], lv_ref[0, 1, k:k + 1, :]
        s_re, s_im = shift(z_re, 1 << k), shift(z_im, 1 << k)
        z_re, z_im = z_re + (a_re * s_re - a_im * s_im), z_im + (a_re * s_im + a_im * s_re)
        k += 1
    p_re, p_im = pt_ref[0, 0], pt_ref[0, 1]
    h_re = z_re + (p_re * h0[0:1, :] - p_im * h0[1:2, :])
    h_im = z_im + (p_re * h0[1:2, :] + p_im * h0[0:1, :])
    last = 0 if reverse else n_rows - 1
    a_re, a_im = lv_ref[0, 0, 0:1, :], lv_ref[0, 1, 0:1, :]
    l_re, l_im = h_re[last:last + 1, :], h_im[last:last + 1, :]
    o_re = a_re * l_re - a_im * l_im + e_re[last:last + 1, :]
    o_im = a_re * l_im + a_im * l_re + e_im[last:last + 1, :]
    return jnp.concatenate([h_re, h_im], axis=1), jnp.concatenate([o_re, o_im], axis=0)


def _s5_kernel(*refs, n_pass_through):
    (uf_ref, ub_ref, mt_ref, wef_ref, web_ref, wof_ref, wob_ref, lvf_ref, ptf_ref, lvb_ref, ptb_ref,
     d_ref, h0f_ref, h0b_ref) = refs[:14]
    y1_ref, y2_ref, hff_ref, hfb_ref, sf_ref, sb_ref = refs[14 + n_pass_through:]
    i = pl.program_id(2)
    n_rows = uf_ref.shape[0] // S5_T

    @pl.when(i == 0)
    def _():
        sf_ref[...] = h0f_ref[...]
        sb_ref[...] = h0b_ref[...]

    def chunk_rows(ref):
        return jnp.concatenate([ref[pl.ds(t, n_rows, stride=S5_T), :] for t in range(S5_T)], axis=1)

    uf = chunk_rows(uf_ref)
    ub = chunk_rows(ub_ref)
    uf_bf = uf.astype(BF16)
    y_local = jnp.dot(uf_bf, mt_ref[0], preferred_element_type=F32)
    e_f = jnp.dot(uf_bf, wef_ref[0], preferred_element_type=F32)
    e_b = jnp.dot(ub.astype(BF16), web_ref[0], preferred_element_type=F32)
    h_f, out_f = _s5_state_scan(e_f, lvf_ref, ptf_ref, sf_ref[...], reverse=False)
    h_b, out_b = _s5_state_scan(e_b, lvb_ref, ptb_ref, sb_ref[...], reverse=True)
    sf_ref[...] = out_f
    sb_ref[...] = out_b
    y1 = y_local + jnp.dot(h_f.astype(BF16), wof_ref[0], preferred_element_type=F32)
    y2 = jnp.dot(h_b.astype(BF16), wob_ref[0], preferred_element_type=F32)
    d = d_ref[...]
    for t in range(S5_T):
        cols = slice(t * LANES, (t + 1) * LANES)
        y1_ref[pl.ds(t, n_rows, stride=S5_T), :] = y1[:, cols] + d * uf[:, cols]
        y2_ref[pl.ds(t, n_rows, stride=S5_T), :] = y2[:, cols]

    @pl.when(i == pl.num_programs(2) - 1)
    def _():
        hff_ref[...] = out_f
        hfb_ref[...] = out_b


def s5_scan(u, col0, row0, bsz, n_tiles, tile, tabs, d_skip, h0_f, h0_b, y_prev=None):
    n_rows = tile // S5_T
    assert tile % S5_T == 0 and tabs['pt_f'].shape[2] == n_rows
    n_lv = tabs['lv_f'].shape[2]
    u_spec_f = pl.BlockSpec((tile, LANES), lambda b, n_, i: (row0(n_) + i, col0 + b))
    u_spec_b = pl.BlockSpec((tile, LANES), lambda b, n_, i: (row0(n_) + n_tiles - 1 - i, col0 + b))
    y_spec_f = pl.BlockSpec((tile, LANES), lambda b, n_, i: (row0(n_) + i, b))
    y_spec_b = pl.BlockSpec((tile, LANES), lambda b, n_, i: (row0(n_) + n_tiles - 1 - i, b))
    w_spec = pl.BlockSpec((1, S5_CW, S5_CW), lambda b, n_, i: (b, 0, 0))
    lv_spec = pl.BlockSpec((1, 2, n_lv, S5_NS), lambda b, n_, i: (b, 0, 0, 0))
    pt_spec = pl.BlockSpec((1, 2, n_rows, S5_NS), lambda b, n_, i: (b, 0, 0, 0))
    st_spec = pl.BlockSpec((None, None, 2, S5_NS), lambda b, n_, i: (n_, b, 0, 0))
    in_specs = [u_spec_f, u_spec_b, w_spec, w_spec, w_spec, w_spec, w_spec, lv_spec, pt_spec, lv_spec, pt_spec,
                pl.BlockSpec((1, LANES), lambda b, n_, i: (0, b)), st_spec, st_spec]
    args = [u, u, tabs['m_toep'], tabs['we_f'], tabs['we_b'], tabs['wo_f'], tabs['wo_b'],
            tabs['lv_f'], tabs['pt_f'], tabs['lv_b'], tabs['pt_b'], d_skip.reshape(1, W_B).astype(F32), h0_f, h0_b]
    aliases = {}
    if y_prev is not None:
        aliases = {len(args): 0, len(args) + 1: 1}
        in_specs += [pl.BlockSpec(memory_space=pl.ANY), pl.BlockSpec(memory_space=pl.ANY)]
        args += list(y_prev)
    y_shape = jax.ShapeDtypeStruct((u.shape[0], W_B), F32)
    st_shape = jax.ShapeDtypeStruct((bsz, S5_NB, 2, S5_NS), F32)
    return pl.pallas_call(
        functools.partial(_s5_kernel, n_pass_through=len(aliases)),
        grid=(S5_NB, bsz, n_tiles),
        in_specs=in_specs,
        out_specs=[y_spec_f, y_spec_b, st_spec, st_spec],
        out_shape=[y_shape, y_shape, st_shape, st_shape],
        input_output_aliases=aliases,
        scratch_shapes=[pltpu.VMEM((2, S5_NS), F32), pltpu.VMEM((2, S5_NS), F32)],
        compiler_params=pltpu.CompilerParams(
            dimension_semantics=("arbitrary", "arbitrary", "arbitrary"),
            vmem_limit_bytes=V7X_VMEM_LIMIT_BYTES),
    )(*args)


def s5_mixer(u, col0, lay, lam_re, lam_im, log_dt, b_re, b_im, c_re, c_im, d_skip, tile_l=1024):
    bsz, n_lat, n_ctx = lay
    tile_l = min(tile_l, n_lat)
    assert n_lat % tile_l == 0 and (bsz * n_lat) % n_ctx == 0
    prm = (lam_re, lam_im, log_dt, b_re, b_im, c_re, c_im)
    tabs_c, tabs_l = _s5_tables(*prm, n_rows_list=(n_ctx // S5_T, tile_l // S5_T))
    zero = jnp.zeros((bsz, S5_NB, 2, S5_NS), F32)
    y1, y2, hf, hb = s5_scan(u, col0, lambda n_: _ctx_row(lay, n_) // n_ctx, bsz, 1, n_ctx, tabs_c, d_skip, zero, zero)
    y1, y2, _, _ = s5_scan(u, col0, lambda n_: _lat_row(lay, n_) // tile_l, bsz, n_lat // tile_l, tile_l, tabs_l,
                           d_skip, hf, hb, y_prev=(y1, y2))
    return y1, y2


HPG_C = H_C // NG_C
GW_C = HPG_C * HD_C


def _softplus(x):
    return jnp.maximum(x, 0.0) + jnp.log1p(jnp.exp(-jnp.abs(x)))


COLS_PER_STEP = 8


def _dwconv_silu_vals(prev, cur, nxt, w_ref, b_ref, xp_ref):
    tile = cur.shape[0]
    xp_ref[0:HALO, :] = prev
    xp_ref[HALO:HALO + tile, :] = cur
    xp_ref[HALO + tile:, :] = nxt
    acc = b_ref[...] + w_ref[0:1, :] * xp_ref[pl.ds(HALO - CONV_W // 2, tile), :]
    for j in range(1, CONV_W):
        acc = acc + w_ref[j:j + 1, :] * xp_ref[pl.ds(HALO - CONV_W // 2 + j, tile), :]
    return acc * jax.nn.sigmoid(acc)


def _ssd_chunk(xbc_ref, dt_rows_raw, dt_cols_raw, pc_ref, pr_ref, s_scr, reverse):
    t_len = xbc_ref.shape[0]
    dt_rows = _softplus(dt_rows_raw + pc_ref[:, 1:2])
    cs_rows = _cumsum_incl(-jnp.exp(pc_ref[:, 0:1]) * dt_rows, 1, reverse)
    dt_cols = _softplus(dt_cols_raw + pr_ref[1:2, :])
    cs_cols = _cumsum_incl(-jnp.exp(pr_ref[0:1, :]) * dt_cols, 0, reverse)
    end = 0 if reverse else t_len - 1
    cs_end = cs_cols[end:end + 1, :]
    row_id = lax.broadcasted_iota(jnp.int32, (t_len, t_len), 0)
    col_id = lax.broadcasted_iota(jnp.int32, (t_len, t_len), 1)
    allowed = (col_id >= row_id) if reverse else (col_id <= row_id)
    out = []
    for g in range(NG_C):
        bg = xbc_ref[:, W_C + g * N_C:W_C + (g + 1) * N_C].astype(BF16)
        cg = xbc_ref[:, W_C + NG_C * N_C + g * N_C:W_C + NG_C * N_C + (g + 1) * N_C].astype(BF16)
        cb = lax.dot_general(cg, bg, (((1,), (1,)), ((), ())), preferred_element_type=F32)
        s_old = s_scr[g]
        y_off = jnp.dot(cg, s_old.astype(BF16), preferred_element_type=F32)
        ys, xds, chunk_decay = [], [], []
        for hh in range(HPG_C):
            h = g * HPG_C + hh
            cs_c, cs_r = cs_cols[:, h:h + 1], cs_rows[h:h + 1, :]
            seg = jnp.exp(jnp.where(allowed, cs_c - cs_r, -jnp.inf))
            xv = xbc_ref[:, h * HD_C:(h + 1) * HD_C] * dt_cols[:, h:h + 1]
            y_diag = jnp.dot((cb * seg).astype(BF16), xv.astype(BF16), preferred_element_type=F32)
            ys.append(y_diag + y_off[:, hh * HD_C:(hh + 1) * HD_C] * jnp.exp(cs_c))
            xds.append(xv * jnp.exp(cs_end[:, h:h + 1] - cs_c))
            chunk_decay.append(jnp.broadcast_to(jnp.exp(cs_end[:, h:h + 1]), (1, HD_C)))
        xd = jnp.concatenate(xds, axis=1).astype(BF16)
        s_loc = lax.dot_general(bg, xd, (((0,), (0,)), ((), ())), preferred_element_type=F32)
        s_scr[g] = s_old * jnp.concatenate(chunk_decay, axis=1) + s_loc
        out.append(jnp.concatenate(ys, axis=1))
    return out


def _ssd_ctx_kernel(*refs, reverse, combine):
    if combine:
        (prev_ref, raw_ref, next_ref, cw_ref, cb_ref, dtr_ref, dtc_ref, pc_ref, pr_ref, yo_ref, dsk_ref,
         o_ref, sfin_ref, s_scr, xp_ref, xbc_ref) = refs
    else:
        (prev_ref, raw_ref, next_ref, cw_ref, cb_ref, dtr_ref, dtc_ref, pc_ref, pr_ref,
         o_ref, sfin_ref, s_scr, xp_ref, xbc_ref) = refs
    s, n = pl.program_id(1), pl.num_programs(1)

    @pl.when(s == 0)
    def _():
        s_scr[...] = jnp.zeros_like(s_scr)

    j = n - 1 - s if reverse else s
    prev = jnp.where(j == 0, 0.0, prev_ref[...])
    nxt = jnp.where(j == n - 1, 0.0, next_ref[...])
    xbc_ref[...] = _dwconv_silu_vals(prev, raw_ref[...], nxt, cw_ref, cb_ref, xp_ref)
    ys = _ssd_chunk(xbc_ref, dtr_ref[...], dtc_ref[...], pc_ref, pr_ref, s_scr, reverse)
    for g in range(NG_C):
        sl = slice(g * GW_C, (g + 1) * GW_C)
        y = ys[g]
        if combine:
            y = y + yo_ref[:, sl] + xbc_ref[:, sl] * dsk_ref[:, sl]
        o_ref[:, sl] = y

    @pl.when(s == n - 1)
    def _():
        sfin_ref[...] = s_scr[...]


def _ssd_lat_kernel(*refs, reverse, combine, n_pass_through):
    (pb_ref, x_ref, nb_ref, cw_ref, cb_ref, dtr_ref, dtc_ref, pc_ref, pr_ref, s0_ref) = refs[:10]
    rest = refs[10:]
    if combine:
        yo_ref, dsk_ref = rest[:2]
        rest = rest[2:]
    o_ref, s_scr, xp_ref, xbc_ref = rest[n_pass_through:]
    s, n = pl.program_id(1), pl.num_programs(1)
    t_len = x_ref.shape[0]

    @pl.when(s == 0)
    def _():
        s_scr[...] = s0_ref[...]

    blk = n - 1 - s if reverse else s
    for kk in range(COLS_PER_STEP):
        k = COLS_PER_STEP - 1 - kk if reverse else kk
        if k > 0:
            prev = x_ref[t_len - HALO:t_len, k - 1, :]
        else:
            prev = jnp.where(blk == 0, 0.0, pb_ref[:, COLS_PER_STEP - 1, :])
        if k < COLS_PER_STEP - 1:
            nxt = x_ref[0:HALO, k + 1, :]
        else:
            nxt = jnp.where(blk == n - 1, 0.0, nb_ref[:, 0, :])
        xbc_ref[...] = _dwconv_silu_vals(prev, x_ref[:, k, :], nxt, cw_ref, cb_ref, xp_ref)
        ys = _ssd_chunk(xbc_ref, dtr_ref[:, k * t_len:(k + 1) * t_len], dtc_ref[k * t_len:(k + 1) * t_len, :],
                        pc_ref, pr_ref, s_scr, reverse)
        for g in range(NG_C):
            sl = slice(g * GW_C, (g + 1) * GW_C)
            y = ys[g]
            if combine:
                y = y + yo_ref[:, k, sl] + xbc_ref[:, sl] * dsk_ref[:, sl]
            o_ref[:, k, sl] = y


def _ssd_params(conv_w, conv_b, a_log, dt_bias):
    return (jnp.pad(conv_w.astype(F32), ((0, HALO - CONV_W), (0, 0))), conv_b.reshape(1, XBC_W).astype(F32),
            jnp.stack([a_log, dt_bias], axis=1).astype(F32), jnp.stack([a_log, dt_bias], axis=0).astype(F32))


_SSD_PARAM_SPECS = [((HALO, XBC_W)), ((1, XBC_W)), ((H_C, 2)), ((2, H_C))]


def ssd_scan_ctx(p_all, lay, direction, dt_rows, dt_cols, prm, combine_args=None):
    bsz, n_lat, n_ctx = lay
    reverse = direction == 1
    t_len = SCAN_CHUNK
    n_chunks = n_ctx // t_len
    hb = t_len // HALO
    chunk = (lambda s: n_chunks - 1 - s) if reverse else (lambda s: s)
    row_blk = lambda b, s: _ctx_row(lay, b) // t_len + chunk(s)
    n_halo = p_all.shape[0] // HALO
    y_spec = pl.BlockSpec((t_len, W_C), lambda b, s: (row_blk(b, s), 0))
    in_specs = [pl.BlockSpec((HALO, XBC_W), lambda b, s: (jnp.maximum(row_blk(b, s) * hb - 1, 0), P_XBC // XBC_W)),
                pl.BlockSpec((t_len, XBC_W), lambda b, s: (row_blk(b, s), P_XBC // XBC_W)),
                pl.BlockSpec((HALO, XBC_W), lambda b, s: (jnp.minimum((row_blk(b, s) + 1) * hb, n_halo - 1),
                                                          P_XBC // XBC_W))]
    in_specs += [pl.BlockSpec(shape, lambda b, s: (0, 0)) for shape in _SSD_PARAM_SPECS]
    in_specs[5:5] = [pl.BlockSpec((None, None, H_C, t_len), lambda b, s: (b, direction, 0, chunk(s))),
                     pl.BlockSpec((None, None, t_len, H_C), lambda b, s: (b, direction, chunk(s), 0))]
    args = [p_all, p_all, p_all, prm[0], prm[1], dt_rows, dt_cols, prm[2], prm[3]]
    if combine_args is not None:
        y_other, d_skip = combine_args
        in_specs += [y_spec, pl.BlockSpec((1, W_C), lambda b, s: (0, 0))]
        args += [y_other, d_skip]
    st_spec = pl.BlockSpec((None, NG_C, N_C, GW_C), lambda b, s: (b, 0, 0, 0))
    return pl.pallas_call(
        functools.partial(_ssd_ctx_kernel, reverse=reverse, combine=combine_args is not None),
        grid=(bsz, n_chunks),
        in_specs=in_specs,
        out_specs=[y_spec, st_spec],
        out_shape=[jax.ShapeDtypeStruct((p_all.shape[0], W_C), F32),
                   jax.ShapeDtypeStruct((bsz, NG_C, N_C, GW_C), F32)],
        scratch_shapes=[pltpu.VMEM((NG_C, N_C, GW_C), F32), pltpu.VMEM((t_len + 2 * HALO, XBC_W), F32),
                        pltpu.VMEM((t_len, XBC_W), F32)],
        compiler_params=pltpu.CompilerParams(dimension_semantics=("arbitrary", "arbitrary"),
                                             vmem_limit_bytes=V7X_VMEM_LIMIT_BYTES),
    )(*args)


def ssd_scan_lat(p_all, lay, direction, dt_rows, dt_cols, prm, state0, y_prev, combine_args=None):
    bsz, n_lat, n_ctx = lay
    reverse = direction == 1
    t_len = n_lat // GRID_W
    n_blocks = GRID_W // COLS_PER_STEP
    assert t_len % HALO == 0 and GRID_W == COLS_PER_STEP * n_blocks
    groups = p_all.shape[0] // GRID_W
    view = lambda t: t.reshape(groups, n_blocks, COLS_PER_STEP, t.shape[1])
    blk = (lambda s: n_blocks - 1 - s) if reverse else (lambda s: s)
    hb = t_len // HALO
    y_spec = pl.BlockSpec((t_len, None, COLS_PER_STEP, W_C), lambda b, s: (b, blk(s), 0, 0))
    xcol = P_XBC // XBC_W
    in_specs = [pl.BlockSpec((HALO, None, COLS_PER_STEP, XBC_W),
                             lambda b, s: ((b + 1) * hb - 1, jnp.maximum(blk(s) - 1, 0), 0, xcol)),
                pl.BlockSpec((t_len, None, COLS_PER_STEP, XBC_W), lambda b, s: (b, blk(s), 0, xcol)),
                pl.BlockSpec((HALO, None, COLS_PER_STEP, XBC_W),
                             lambda b, s: (b * hb, jnp.minimum(blk(s) + 1, n_blocks - 1), 0, xcol))]
    in_specs += [pl.BlockSpec(shape, lambda b, s: (0, 0)) for shape in _SSD_PARAM_SPECS]
    step = COLS_PER_STEP * t_len
    in_specs[5:5] = [pl.BlockSpec((None, None, H_C, step), lambda b, s: (b, direction, 0, blk(s))),
                     pl.BlockSpec((None, None, step, H_C), lambda b, s: (b, direction, blk(s), 0))]
    st_spec = pl.BlockSpec((None, NG_C, N_C, GW_C), lambda b, s: (b, 0, 0, 0))
    in_specs.append(st_spec)
    p4 = view(p_all)
    args = [p4, p4, p4, prm[0], prm[1], dt_rows, dt_cols, prm[2], prm[3], state0]
    if combine_args is not None:
        y_other, d_skip = combine_args
        in_specs += [y_spec, pl.BlockSpec((1, W_C), lambda b, s: (0, 0))]
        args += [view(y_other), d_skip]
    aliases = {len(args): 0}
    in_specs.append(pl.BlockSpec(memory_space=pl.ANY))
    args.append(view(y_prev))
    y4 = pl.pallas_call(
        functools.partial(_ssd_lat_kernel, reverse=reverse, combine=combine_args is not None, n_pass_through=1),
        grid=(bsz, n_blocks),
        in_specs=in_specs,
        out_specs=y_spec,
        out_shape=jax.ShapeDtypeStruct((groups, n_blocks, COLS_PER_STEP, W_C), F32),
        input_output_aliases=aliases,
        scratch_shapes=[pltpu.VMEM((NG_C, N_C, GW_C), F32), pltpu.VMEM((t_len + 2 * HALO, XBC_W), F32),
                        pltpu.VMEM((t_len, XBC_W), F32)],
        compiler_params=pltpu.CompilerParams(dimension_semantics=("arbitrary", "arbitrary"),
                                             vmem_limit_bytes=V7X_VMEM_LIMIT_BYTES),
    )(*args)
    return y4.reshape(p_all.shape[0], W_C)


def ssd_mixer_inplace(p_all, lay, conv_w, conv_b, a_log, dt_bias, d_skip):
    bsz, n_lat, n_ctx = lay
    grid_rows = n_lat // GRID_W

    def dt_layouts(t, n, colmajor):
        t = t[:, P_DT:P_DT + 2 * H_C].reshape(bsz, n, 2, H_C)
        if colmajor:
            t = t.reshape(bsz, grid_rows, GRID_W, 2, H_C).swapaxes(1, 2).reshape(bsz, n, 2, H_C)
        return t.transpose(0, 2, 3, 1), t.transpose(0, 2, 1, 3)

    dtr_c, dtc_c = dt_layouts(p_all[bsz * n_lat:], n_ctx, False)
    dtr_l, dtc_l = dt_layouts(p_all[:bsz * n_lat], n_lat, True)
    d_row = jnp.repeat(d_skip.astype(F32), HD_C).reshape(1, W_C)
    y = {}
    for direction in (1, 0):
        prm = _ssd_params(conv_w, conv_b, a_log[direction], dt_bias[direction])
        combine = None if direction == 1 else (y[1], d_row)
        y_ctx, state = ssd_scan_ctx(p_all, lay, direction, dtr_c, dtc_c, prm, combine)
        y[direction] = ssd_scan_lat(p_all, lay, direction, dtr_l, dtc_l, prm, state, y_ctx, combine)
    return y[0]


TRUNK_TM = 512
P_COLS = 7168
P_XM, P_ZM, P_U, P_ZS, P_XBC, P_GD, P_DT = 0, W_A, 2 * W_A, 2 * W_A + W_B, 2 * W_A + W_B + W_C, 6144, 6656


def _lat_row(lay, b):
    bsz, n_lat, n_ctx = lay
    return b * n_lat


def _ctx_row(lay, b):
    bsz, n_lat, n_ctx = lay
    return bsz * n_lat + b * n_ctx


def _seq_block(lay, b, j, t_len):
    bsz, n_lat, n_ctx = lay
    n_cb = n_ctx // t_len
    return jnp.where(j < n_cb, (bsz * n_lat + b * n_ctx) // t_len + j, (b * n_lat) // t_len + (j - n_cb))


def _mod_index(lay, tm):
    bsz, n_lat, n_ctx = lay
    assert n_lat % tm == 0 and (bsz * n_ctx) % tm == 0
    return lambda i: jnp.where(i < (bsz * n_lat) // tm, 1 + i // (n_lat // tm), 0)


def _permute_w_in(w):
    xm_zm_u_zs_xbc, dt, gd = w[:, :SPLIT_IDX[4]], w[:, SPLIT_IDX[4]:SPLIT_IDX[5]], w[:, SPLIT_IDX[5]:]
    pad = jnp.zeros((w.shape[0], P_COLS - IN_COLS), w.dtype)
    return jnp.concatenate([xm_zm_u_zs_xbc, gd, dt, pad], axis=1).astype(BF16)


def _modulated_norm(x, g, shift, scale):
    xn = x * lax.rsqrt(jnp.mean(x * x, axis=-1, keepdims=True) + EPS) * g
    return xn * (1.0 + scale) + shift


def _win_kernel(x_ref, g_ref, mod_ref, w_ref, o_ref, h_ref):
    @pl.when(pl.program_id(1) == 0)
    def _():
        h_ref[...] = _modulated_norm(x_ref[...], g_ref[...], mod_ref[0, 0:1, :], mod_ref[0, 1:2, :]).astype(BF16)

    o_ref[...] = jnp.dot(h_ref[...], w_ref[...], preferred_element_type=F32)


def in_projection(x, g, mods, w_p, lay, tn=1024):
    m, d = x.shape
    tm = TRUNK_TM
    mod_idx = _mod_index(lay, tm)
    single = pl.Buffered(1)
    return pl.pallas_call(
        _win_kernel,
        grid=(m // tm, P_COLS // tn),
        in_specs=[pl.BlockSpec((tm, d), lambda i, j: (i, 0)),
                  pl.BlockSpec((1, d), lambda i, j: (0, 0), pipeline_mode=single),
                  pl.BlockSpec((1, MOD_ROWS, d), lambda i, j: (mod_idx(i), 0, 0), pipeline_mode=single),
                  pl.BlockSpec((d, tn), lambda i, j: (0, j))],
        out_specs=pl.BlockSpec((tm, tn), lambda i, j: (i, j)),
        out_shape=jax.ShapeDtypeStruct((m, P_COLS), F32),
        scratch_shapes=[pltpu.VMEM((tm, d), BF16)],
        compiler_params=pltpu.CompilerParams(dimension_semantics=("arbitrary", "arbitrary"),
                                             vmem_limit_bytes=V7X_VMEM_LIMIT_BYTES),
    )(x, g.reshape(1, d).astype(F32), mods, w_p)


def _merge_kernel(gd_ref, ya_ref, y1_ref, y2_ref, yc_ref, zs_ref, ng_ref, gluw_ref, glub_ref, gw_ref, gb_ref, bw_ref,
                  o_ref, gds_ref, ys_ref):
    @pl.when(pl.program_id(1) == 0)
    def _():
        gds_ref[...] = gd_ref[...].astype(BF16)
        ys_ref[0] = ya_ref[...].astype(BF16)
        yb = jax.nn.gelu(y1_ref[...] + y2_ref[...])
        gate = jnp.dot(yb.astype(BF16), gluw_ref[...], preferred_element_type=F32) + glub_ref[...]
        ys_ref[1] = (yb * jax.nn.sigmoid(gate)).astype(BF16)
        zs = zs_ref[...]
        yg = yc_ref[...] * (zs * jax.nn.sigmoid(zs))
        yg = yg * lax.rsqrt(jnp.mean(yg * yg, axis=-1, keepdims=True) + EPS) * ng_ref[...]
        ys_ref[2] = yg.astype(BF16)

    acc = None
    for i in range(N_BRANCH):
        gate = jax.nn.sigmoid(jnp.dot(gds_ref[...], gw_ref[i], preferred_element_type=F32) + gb_ref[i])
        term = gate * jnp.dot(ys_ref[i], bw_ref[i], preferred_element_type=F32)
        acc = term if acc is None else acc + term
    o_ref[...] = acc.astype(o_ref.dtype)


def merge_projection(p_all, ya, y1, y2, yc, ssd_norm_g, glu_w, glu_b, gate_w, gate_b, branch_w, tn=1024):
    m = p_all.shape[0]
    d = gate_w.shape[-1]
    tm = TRUNK_TM
    single = pl.Buffered(1)
    row = lambda w: pl.BlockSpec((tm, w), lambda i, j: (i, 0))
    return pl.pallas_call(
        _merge_kernel,
        grid=(m // tm, d // tn),
        in_specs=[pl.BlockSpec((tm, R_GATE), lambda i, j: (i, P_GD // R_GATE)),
                  row(W_A), row(W_B), row(W_B), row(W_C),
                  pl.BlockSpec((tm, W_C), lambda i, j: (i, P_ZS // W_C)),
                  pl.BlockSpec((1, W_C), lambda i, j: (0, 0), pipeline_mode=single),
                  pl.BlockSpec((W_B, W_B), lambda i, j: (0, 0), pipeline_mode=single),
                  pl.BlockSpec((1, W_B), lambda i, j: (0, 0), pipeline_mode=single),
                  pl.BlockSpec((N_BRANCH, R_GATE, tn), lambda i, j: (0, 0, j)),
                  pl.BlockSpec((N_BRANCH, 1, tn), lambda i, j: (0, 0, j)),
                  pl.BlockSpec((N_BRANCH, W_MIX, tn), lambda i, j: (0, 0, j))],
        out_specs=pl.BlockSpec((tm, tn), lambda i, j: (i, j)),
        out_shape=jax.ShapeDtypeStruct((m, d), BF16),
        scratch_shapes=[pltpu.VMEM((tm, R_GATE), BF16), pltpu.VMEM((N_BRANCH, tm, W_MIX), BF16)],
        compiler_params=pltpu.CompilerParams(dimension_semantics=("arbitrary", "arbitrary"),
                                             vmem_limit_bytes=V7X_VMEM_LIMIT_BYTES),
    )(p_all, ya, y1, y2, yc, p_all, ssd_norm_g.reshape(1, W_C).astype(F32), glu_w.astype(BF16),
      glu_b.reshape(1, W_B).astype(F32),
      gate_w.astype(BF16), gate_b.reshape(N_BRANCH, 1, d).astype(F32), branch_w.astype(BF16))


def _wout_kernel(t_ref, w_ref, x_ref, mod_ref, o_ref):
    y = jnp.dot(t_ref[...], w_ref[...], preferred_element_type=F32)
    o_ref[...] = x_ref[...] + mod_ref[0, 2:3, :] * y


def out_projection(t, w_out_bf, x, mods, lay, tn=1024):
    m, d = x.shape
    tm = TRUNK_TM
    mod_idx = _mod_index(lay, tm)
    return pl.pallas_call(
        _wout_kernel,
        grid=(m // tm, d // tn),
        in_specs=[pl.BlockSpec((tm, d), lambda i, j: (i, 0)),
                  pl.BlockSpec((d, tn), lambda i, j: (0, j)),
                  pl.BlockSpec((tm, tn), lambda i, j: (i, j)),
                  pl.BlockSpec((1, MOD_ROWS, tn), lambda i, j: (mod_idx(i), 0, j))],
        out_specs=pl.BlockSpec((tm, tn), lambda i, j: (i, j)),
        out_shape=jax.ShapeDtypeStruct((m, d), F32),
        compiler_params=pltpu.CompilerParams(dimension_semantics=("arbitrary", "arbitrary"),
                                             vmem_limit_bytes=V7X_VMEM_LIMIT_BYTES),
    )(t, w_out_bf, x, mods)


def _mlp_kernel(x_ref, g_ref, mod_ref, w1_ref, w2_ref, o_ref, h_ref):
    f = pl.program_id(1)

    @pl.when(f == 0)
    def _():
        x = x_ref[...]
        xn = x * lax.rsqrt(jnp.mean(x * x, axis=-1, keepdims=True) + EPS) * g_ref[...]
        h = xn * (1.0 + mod_ref[0, 4:5, :]) + mod_ref[0, 3:4, :]
        h_ref[...] = h.astype(BF16)
        o_ref[...] = jnp.zeros_like(o_ref)

    a = jnp.dot(h_ref[...], w1_ref[...], preferred_element_type=F32)
    a = jnp.square(jnp.maximum(a, 0.0)).astype(BF16)
    o_ref[...] += jnp.dot(a, w2_ref[...], preferred_element_type=F32)

    @pl.when(f == pl.num_programs(1) - 1)
    def _():
        o_ref[...] = x_ref[...] + mod_ref[0, 5:6, :] * o_ref[...]


def mlp_block(x, g, mod, w1, w2, lay, tf=512):
    m, d = x.shape
    dff = w1.shape[1]
    tm = TRUNK_TM
    assert m % tm == 0 and dff % tf == 0
    mod_idx = _mod_index(lay, tm)
    single = pl.Buffered(1)
    return pl.pallas_call(
        _mlp_kernel,
        grid=(m // tm, dff // tf),
        in_specs=[
            pl.BlockSpec((tm, d), lambda i, f: (i, 0), pipeline_mode=single),
            pl.BlockSpec((1, d), lambda i, f: (0, 0), pipeline_mode=single),
            pl.BlockSpec((1, MOD_ROWS, d), lambda i, f: (mod_idx(i), 0, 0), pipeline_mode=single),
            pl.BlockSpec((d, tf), lambda i, f: (0, f)),
            pl.BlockSpec((tf, d), lambda i, f: (f, 0)),
        ],
        out_specs=pl.BlockSpec((tm, d), lambda i, f: (i, 0), pipeline_mode=single),
        out_shape=jax.ShapeDtypeStruct((m, d), F32),
        scratch_shapes=[pltpu.VMEM((tm, d), BF16)],
        compiler_params=pltpu.CompilerParams(
            dimension_semantics=("arbitrary", "arbitrary"),
            vmem_limit_bytes=V7X_VMEM_LIMIT_BYTES),
    )(x, g.reshape(1, d), mod, w1, w2)


def _pad_mod(mod):
    nb = mod.shape[0]
    mod = mod.reshape(nb, N_MOD, D_MODEL)
    return jnp.pad(mod, ((0, 0), (0, MOD_ROWS - N_MOD), (0, 0)))


def kernel(x, c, ctx, c_ctx, norm1_g, norm2_g, final_g, mod_down, mod_up, mod_b, w_in,
           m_conv_w, m_conv_b, m_wq, m_wk, m_wv, m_gate_w, m_gate_b, m_norm_g, m_skip,
           s5_lam_re, s5_lam_im, s5_log_dt, s5_b_re, s5_b_im, s5_c_re, s5_c_im, s5_d, s5_glu_w, s5_glu_b,
           ssd_conv_w, ssd_conv_b, ssd_a_log, ssd_dt_bias, ssd_d, ssd_norm_g,
           gate_w, gate_b, branch_w, w_out, mlp_w1, mlp_w2):
    bsz, n_lat, d = x.shape
    n_ctx = ctx.shape[1]
    lay = (bsz, n_lat, n_ctx)
    grid_rows = n_lat // GRID_W
    xs = jnp.concatenate([x.reshape(bsz * n_lat, d), ctx.reshape(bsz * n_ctx, d)], axis=0)
    for l in range(DEPTH):
        mods = _pad_mod(adaln(jnp.concatenate([c_ctx[None], c], axis=0), mod_down[l], mod_up[l], mod_b[l]))
        p_all = in_projection(xs, norm1_g[l], mods, _permute_w_in(w_in[l]), lay)
        ya = mlstm_mixer(p_all, lay, m_conv_w[l], m_conv_b[l], m_wq[l], m_wk[l], m_wv[l],
                         m_gate_w[l], m_gate_b[l], m_norm_g[l], m_skip[l])
        yb1, yb2 = s5_mixer(p_all, P_U // LANES, lay, s5_lam_re[l], s5_lam_im[l], s5_log_dt[l], s5_b_re[l], s5_b_im[l],
                            s5_c_re[l], s5_c_im[l], s5_d[l])
        yc = ssd_mixer_inplace(p_all, lay, ssd_conv_w[l], ssd_conv_b[l], ssd_a_log[l], ssd_dt_bias[l], ssd_d[l])
        t = merge_projection(p_all, ya, yb1, yb2, yc, ssd_norm_g[l], s5_glu_w[l], s5_glu_b[l], gate_w[l], gate_b[l],
                             branch_w[l])
        xs = out_projection(t, w_out[l].astype(BF16), xs, mods, lay)
        xs = mlp_block(xs, norm2_g[l], mods, mlp_w1[l].astype(BF16), mlp_w2[l].astype(BF16), lay)
    return rmsnorm(xs[:bsz * n_lat].reshape(bsz, n_lat, d), final_g)
```

```python
import functools
import math
import jax
import jax.numpy as jnp
from jax import lax
import numpy as np
from jax.experimental import pallas as pl
from jax.experimental.pallas import tpu as pltpu

D_MODEL = 4096
BATCH = 2
SEQ = 8192
DEPTH = 4

CTX_LEN = 256
GRID_W = 64
W_MIX = D_MODEL // 4
W_A = W_MIX
H_A = 8
DH_A = W_A // H_A
QKV_BLOCK = 4
MLSTM_CHUNK = 64
W_B = W_MIX
S5_GROUP = 16
G_B = W_B // S5_GROUP
P_B = 64
W_C = W_MIX
HD_C = 64
H_C = W_C // HD_C
NG_C = 4
N_C = 128
SSD_CHUNK = 128
CONV_W = 5
XBC_W = W_C + 2 * NG_C * N_C
D_FF = 4 * D_MODEL
R_MOD = 512
R_GATE = 512
N_BRANCH = 3
N_MOD = 6
SPLIT_IDX = (W_A, 2 * W_A, 2 * W_A + W_B, 2 * W_A + W_B + W_C, 2 * W_A + W_B + W_C + XBC_W, 2 * W_A + W_B + W_C + XBC_W + 2 * H_C)
IN_COLS = 2 * W_A + W_B + W_C + XBC_W + 2 * H_C + R_GATE
EPS = 1e-6
F32 = jnp.float32
BF16 = jnp.bfloat16

V7X_VMEM_LIMIT_BYTES = 56 * 1024 * 1024
MOD_ROWS = 8
LANES = 128


def adaln(cvec, down, up, b):
    return (jax.nn.silu(cvec) @ down) @ up + b


SCAN_CHUNK = 128


def _cumsum_incl(x, axis, reverse):
    n = x.shape[axis]
    idx = lax.broadcasted_iota(jnp.int32, x.shape, axis)
    s = 1
    while s < n:
        if reverse:
            x = x + jnp.where(idx < n - s, pltpu.roll(x, n - s, axis), 0.0)
        else:
            x = x + jnp.where(idx >= s, pltpu.roll(x, s, axis), 0.0)
        s *= 2
    return x


def _scan_chunk_index(s, n_ctx_chunks, n_chunks, reverse):
    if not reverse:
        return s
    return jnp.where(s < n_ctx_chunks, n_ctx_chunks - 1 - s, n_chunks - 1 - (s - n_ctx_chunks))


def _mlstm_kernel(*refs, reverse, readout, gate_col0):
    if readout:
        (q_ref, k_ref, v_ref, gr_ref, gc_ref, hb_ref, xc_ref, zm_ref, ng_ref, sk_ref, o_ref, c_scr, n_scr, m_scr) = refs
    else:
        (q_ref, k_ref, v_ref, gr_ref, gc_ref, o_ref, c_scr, n_scr, m_scr) = refs
    t_len = q_ref.shape[0]

    @pl.when(pl.program_id(1) == 0)
    def _():
        c_scr[...] = jnp.zeros_like(c_scr)
        n_scr[...] = jnp.zeros_like(n_scr)
        m_scr[...] = jnp.zeros_like(m_scr)

    b_rows = _cumsum_incl(gr_ref[H_A:, :], 1, reverse)
    b_cols = _cumsum_incl(gc_ref[:, gate_col0 + H_A:gate_col0 + 2 * H_A], 0, reverse)
    i_rows = gr_ref[:H_A, :]
    i_cols = gc_ref[:, gate_col0:gate_col0 + H_A]
    row_id = lax.broadcasted_iota(jnp.int32, (t_len, t_len), 0)
    col_id = lax.broadcasted_iota(jnp.int32, (t_len, t_len), 1)
    allowed = (col_id >= row_id) if reverse else (col_id <= row_id)
    end = 0 if reverse else t_len - 1
    m_all = m_scr[...]
    n_all = n_scr[...]
    for h in range(H_A):
        sl = slice(h * DH_A, (h + 1) * DH_A)
        q = q_ref[:, sl]
        k = k_ref[:, sl]
        v = v_ref[:, sl]
        b_row, b_col = b_rows[h:h + 1, :], b_cols[:, h:h + 1]
        i_row, i_col = i_rows[h:h + 1, :], i_cols[:, h:h + 1]
        b_end = b_row[:, end:end + 1]
        m_st = m_all[h:h + 1, 0:1]
        n_st = n_all[h:h + 1, :]
        c_st = c_scr[h]
        d_log = jnp.where(allowed, b_col - b_row + i_row, -jnp.inf)
        inter = b_col + m_st
        m_t = jnp.maximum(inter, jnp.max(d_log, axis=-1, keepdims=True))
        w_carry = jnp.exp(inter - m_t)
        qb, kb, vb = q.astype(BF16), k.astype(BF16), v.astype(BF16)
        qk = lax.dot_general(qb, kb, (((1,), (1,)), ((), ())), preferred_element_type=F32)
        s = qk * jnp.exp(d_log - m_t)
        num = (jnp.dot(s.astype(BF16), vb, preferred_element_type=F32)
               + w_carry * jnp.dot(qb, c_st.astype(BF16), preferred_element_type=F32))
        den = jnp.sum(s, axis=-1, keepdims=True) + w_carry * jnp.sum(q * n_st, axis=-1, keepdims=True)
        hh = num / jnp.maximum(jnp.abs(den), jnp.exp(-m_t))
        log_ws = b_end - b_col + i_col
        m_new = jnp.maximum(b_end + m_st, jnp.max(log_ws, axis=0, keepdims=True))
        kw = k * jnp.exp(log_ws - m_new)
        w_prev = jnp.exp(b_end + m_st - m_new)
        c_scr[h] = w_prev * c_st + lax.dot_general(kw.astype(BF16), vb, (((0,), (0,)), ((), ())),
                                                   preferred_element_type=F32)
        n_scr[h:h + 1, :] = w_prev * n_st + jnp.sum(kw, axis=0, keepdims=True)
        m_scr[h:h + 1, :] = jnp.broadcast_to(m_new, (1, DH_A))
        if readout:
            hh = hh + hb_ref[:, sl]
            mu = jnp.mean(hh, axis=-1, keepdims=True)
            var = jnp.mean(jnp.square(hh - mu), axis=-1, keepdims=True)
            hn = (hh - mu) * lax.rsqrt(var + EPS)
            zm = zm_ref[:, sl]
            o_ref[:, sl] = (hn * ng_ref[:, sl] + sk_ref[:, sl] * xc_ref[:, sl]) * (zm * jax.nn.sigmoid(zm))
        else:
            o_ref[:, sl] = hh


def mlstm_scan(q, k, v, g_rows, g_cols, lay, direction, readout_args=None):
    bsz, n_lat, n_ctx = lay
    w = q.shape[1]
    reverse = direction == 1
    t_len = SCAN_CHUNK
    assert n_lat % t_len == 0 and n_ctx % t_len == 0
    n_chunks, n_ctx_chunks = (n_lat + n_ctx) // t_len, n_ctx // t_len
    order = functools.partial(_scan_chunk_index, n_ctx_chunks=n_ctx_chunks, n_chunks=n_chunks, reverse=reverse)
    blk = lambda b, s: _seq_block(lay, b, order(s), t_len)
    tok = pl.BlockSpec((t_len, w), lambda b, s: (blk(b, s), 0))
    in_specs = [tok, tok, tok,
                pl.BlockSpec((2 * H_A, t_len), lambda b, s: (direction, blk(b, s))),
                pl.BlockSpec((t_len, 4 * H_A), lambda b, s: (blk(b, s), 0))]
    args = [q, k, v, g_rows, g_cols]
    if readout_args is not None:
        h_other, xc, p_all, norm_g, skip = readout_args
        vec = pl.BlockSpec((1, w), lambda b, s: (0, 0))
        in_specs += [tok, tok, pl.BlockSpec((t_len, w), lambda b, s: (blk(b, s), P_ZM // w)), vec, vec]
        args += [h_other, xc, p_all, norm_g.reshape(1, w).astype(F32), skip.reshape(1, w).astype(F32)]
    return pl.pallas_call(
        functools.partial(_mlstm_kernel, reverse=reverse, readout=readout_args is not None,
                          gate_col0=2 * H_A * direction),
        grid=(bsz, n_chunks),
        in_specs=in_specs,
        out_specs=tok,
        out_shape=jax.ShapeDtypeStruct(q.shape, F32),
        scratch_shapes=[pltpu.VMEM((H_A, DH_A, DH_A), F32), pltpu.VMEM((H_A, DH_A), F32), pltpu.VMEM((H_A, DH_A), F32)],
        compiler_params=pltpu.CompilerParams(
            dimension_semantics=("arbitrary", "arbitrary"),
            vmem_limit_bytes=V7X_VMEM_LIMIT_BYTES),
    )(*args)


FEAT_TILE = 256
HALO = 8
N_LB = W_A // LANES


def _seq_edges(lay, i, tile):
    bsz, n_lat, n_ctx = lay
    lat_tiles, ctx_tiles, base = n_lat // tile, n_ctx // tile, (bsz * n_lat) // tile
    pos = jnp.where(i < base, i % lat_tiles, (i - base) % ctx_tiles)
    n = jnp.where(i < base, lat_tiles, ctx_tiles)
    return pos == 0, pos == n - 1


def _dwconv_silu(prev_ref, cur_ref, next_ref, w_ref, b_ref, xp_ref, first, last):
    tile = cur_ref.shape[0]
    xp_ref[0:HALO, :] = jnp.where(first, 0.0, prev_ref[...])
    xp_ref[HALO:HALO + tile, :] = cur_ref[...]
    xp_ref[HALO + tile:, :] = jnp.where(last, 0.0, next_ref[...])
    acc = b_ref[...] + w_ref[0:1, :] * xp_ref[pl.ds(HALO - CONV_W // 2, tile), :]
    for j in range(1, CONV_W):
        acc = acc + w_ref[j:j + 1, :] * xp_ref[pl.ds(HALO - CONV_W // 2 + j, tile), :]
    return acc * jax.nn.sigmoid(acc)


def _split_dot(x, w_hi, w_lo):
    x_hi = x.astype(BF16)
    x_lo = (x - x_hi.astype(F32)).astype(BF16)
    return (jnp.dot(x_hi, w_hi, preferred_element_type=F32) + jnp.dot(x_lo, w_hi, preferred_element_type=F32)
            + jnp.dot(x_hi, w_lo, preferred_element_type=F32))


def _mlstm_feat_kernel(prev_ref, cur_ref, next_ref, cw_ref, cb_ref, wqk_hi_ref, wqk_lo_ref, wv_hi_ref, wv_lo_ref,
                       gw_ref, gwt_ref, gbc_ref, gbr_ref, xc_ref, q_ref, k_ref, v_ref, gc_ref, gr_ref,
                       xp_ref, qkv_ref, *, lay):
    tile = cur_ref.shape[0]
    first, last = _seq_edges(lay, pl.program_id(0), tile)
    xc = _dwconv_silu(prev_ref, cur_ref, next_ref, cw_ref, cb_ref, xp_ref, first, last)
    xc_ref[...] = xc
    for lb in range(N_LB):
        sl = slice(lb * LANES, (lb + 1) * LANES)
        qk = _split_dot(xc[:, sl], wqk_hi_ref[lb], wqk_lo_ref[lb])
        v = _split_dot(cur_ref[:, sl], wv_hi_ref[lb], wv_lo_ref[lb])
        q_ref[:, sl] = qk[:, :LANES] * DH_A ** -0.5
        k_ref[:, sl] = qk[:, LANES:]
        v_ref[:, sl] = v
        qkv_ref[:, sl] = qk[:, :LANES].astype(BF16)
        qkv_ref[:, W_A + lb * LANES:W_A + (lb + 1) * LANES] = qk[:, LANES:].astype(BF16)
        qkv_ref[:, 2 * W_A + lb * LANES:2 * W_A + (lb + 1) * LANES] = v.astype(BF16)
    qkv = qkv_ref[...]
    g_cols = jnp.dot(qkv, gw_ref[...], preferred_element_type=F32) + gbc_ref[...]
    g_rows = lax.dot_general(gwt_ref[...], qkv, (((1,), (1,)), ((), ())), preferred_element_type=F32) + gbr_ref[...]
    is_f_c = (lax.broadcasted_iota(jnp.int32, g_cols.shape, 1) % (2 * H_A)) >= H_A
    is_f_r = (lax.broadcasted_iota(jnp.int32, g_rows.shape, 0) % (2 * H_A)) >= H_A
    gc_ref[...] = jnp.where(is_f_c, -_softplus(-g_cols), g_cols)
    gr_ref[...] = jnp.where(is_f_r, -_softplus(-g_rows), g_rows)


def _blockdiag_dense(w):
    per = LANES // QKV_BLOCK
    w = w.astype(F32).reshape(N_LB, per, QKV_BLOCK, QKV_BLOCK)
    eye = jnp.eye(per, dtype=F32)
    return (w[:, :, :, None, :] * eye[None, :, None, :, None]).reshape(N_LB, LANES, LANES)


def _hi_lo(w):
    hi = w.astype(BF16)
    return hi, (w - hi.astype(F32)).astype(BF16)


def mlstm_features(p_all, lay, conv_w, conv_b, wq, wk, wv, gate_w, gate_b):
    rows = p_all.shape[0]
    tile = FEAT_TILE
    hb = tile // HALO
    n_halo_blocks = rows // HALO
    wqk = jnp.concatenate([_blockdiag_dense(wq), _blockdiag_dense(wk)], axis=2)
    wqk_hi, wqk_lo = _hi_lo(wqk)
    wv_hi, wv_lo = _hi_lo(_blockdiag_dense(wv))
    gw = gate_w.astype(F32).transpose(1, 0, 2).reshape(3 * W_A, 4 * H_A)
    gb = gate_b.astype(F32).reshape(1, 4 * H_A)
    cw = jnp.pad(conv_w.astype(F32), ((0, HALO - CONV_W), (0, 0)))
    tok = pl.BlockSpec((tile, W_A), lambda i: (i, 0))
    full = lambda *shape: pl.BlockSpec(shape, lambda i: (0,) * len(shape))
    return pl.pallas_call(
        functools.partial(_mlstm_feat_kernel, lay=lay),
        grid=(rows // tile,),
        in_specs=[pl.BlockSpec((HALO, W_A), lambda i: (jnp.maximum(i * hb - 1, 0), P_XM // W_A)),
                  pl.BlockSpec((tile, W_A), lambda i: (i, P_XM // W_A)),
                  pl.BlockSpec((HALO, W_A), lambda i: (jnp.minimum((i + 1) * hb, n_halo_blocks - 1), P_XM // W_A)),
                  full(HALO, W_A), full(1, W_A),
                  full(N_LB, LANES, 2 * LANES), full(N_LB, LANES, 2 * LANES),
                  full(N_LB, LANES, LANES), full(N_LB, LANES, LANES),
                  full(3 * W_A, 4 * H_A), full(4 * H_A, 3 * W_A), full(1, 4 * H_A), full(4 * H_A, 1)],
        out_specs=[tok, tok, tok, tok,
                   pl.BlockSpec((tile, 4 * H_A), lambda i: (i, 0)),
                   pl.BlockSpec((4 * H_A, tile), lambda i: (0, i))],
        out_shape=[jax.ShapeDtypeStruct((rows, W_A), F32)] * 4
        + [jax.ShapeDtypeStruct((rows, 4 * H_A), F32), jax.ShapeDtypeStruct((4 * H_A, rows), F32)],
        scratch_shapes=[pltpu.VMEM((tile + 2 * HALO, W_A), F32), pltpu.VMEM((tile, 3 * W_A), BF16)],
        compiler_params=pltpu.CompilerParams(dimension_semantics=("arbitrary",),
                                             vmem_limit_bytes=V7X_VMEM_LIMIT_BYTES),
    )(p_all, p_all, p_all, cw, conv_b.reshape(1, W_A).astype(F32), wqk_hi, wqk_lo, wv_hi, wv_lo,
      gw.astype(BF16), gw.T.astype(BF16), gb, gb.reshape(4 * H_A, 1))


def mlstm_mixer(p_all, lay, conv_w, conv_b, wq, wk, wv, gate_w, gate_b, norm_g, skip):
    xc, q, k, v, g_cols, g_rows = mlstm_features(p_all, lay, conv_w, conv_b, wq, wk, wv, gate_w, gate_b)
    h_bwd = mlstm_scan(q, k, v, g_rows, g_cols, lay, 1)
    return mlstm_scan(q, k, v, g_rows, g_cols, lay, 0, readout_args=(h_bwd, xc, p_all, norm_g, skip))


S5_T = 8
S5_GPB = LANES // S5_GROUP
S5_NB = W_B // LANES
S5_NS = S5_GPB * P_B
S5_CW = S5_T * LANES


def _s5_tables(lam_re, lam_im, log_dt, b_re, b_im, c_re, c_im, n_rows_list):
    lam = lax.complex(lam_re.astype(F32), lam_im.astype(F32))
    ldt = lam * jnp.exp(log_dt.astype(F32))[..., None]
    lam_bar = jnp.exp(ldt)
    b_bar = ((lam_bar - 1) / lam)[..., None] * lax.complex(b_re.astype(F32), b_im.astype(F32))
    c_mat = lax.complex(c_re.astype(F32), c_im.astype(F32))
    steps = jnp.arange(S5_T + 1, dtype=F32)
    pw = jnp.exp(steps[None, :, None, None] * ldt[:, None])
    g_local = jnp.arange(G_B) % S5_GPB
    mask_y = (jnp.arange(LANES)[None, :] // S5_GROUP == g_local[:, None]).astype(F32)
    mask_x = (jnp.arange(S5_NS)[None, :] // P_B == g_local[:, None]).astype(F32)
    c_til = jnp.tile(c_mat, (1, 1, S5_GPB, 1))

    def blocks(t):
        return t.reshape(S5_NB, S5_GPB, *t.shape[1:]).transpose(0, 2, 1, 3, 4, 5)

    kk = jnp.real(jnp.einsum('dgyp,djgp,dgpe->djgey', c_til, pw[:, :S5_T], b_bar)) * mask_y[None, None, :, None, :]
    tt = jnp.arange(S5_T)
    diff = tt[None, :] - tt[:, None]
    kf = kk[0][jnp.clip(diff, 0, S5_T - 1)] * (diff >= 0)[..., None, None, None].astype(F32)
    kb = kk[1][jnp.clip(-diff, 0, S5_T - 1)] * (diff <= 0)[..., None, None, None].astype(F32)
    kt = (kf + kb).astype(BF16).transpose(2, 0, 3, 1, 4)
    m_toep = blocks(kt).reshape(S5_NB, S5_CW, S5_CW)

    def w_in(d, powers):
        w = jnp.tile(powers, (1, 1, S5_GPB))[:, :, None, :] * jnp.tile(b_bar[d], (1, S5_GPB, 1)).transpose(0, 2, 1)[None]
        w = w * mask_x[None, :, None, :]
        w = jnp.stack([jnp.real(w), jnp.imag(w)], axis=0).astype(BF16)
        w = blocks(w.transpose(2, 1, 3, 0, 4))
        return w.reshape(S5_NB, S5_CW, 2 * S5_NS)

    def w_out(d, powers):
        w = c_til[d][None] * powers[:, :, None, :] * mask_y[None, :, :, None]
        w = jnp.stack([jnp.real(w), -jnp.imag(w)], axis=0).astype(BF16)
        w = blocks(w.transpose(2, 0, 4, 1, 3))
        return w.reshape(S5_NB, 2 * S5_NS, S5_CW)

    we_f = w_in(0, pw[0, S5_T - 1::-1][:S5_T])
    we_b = w_in(1, pw[1, :S5_T])
    wo_f = w_out(0, pw[0, 1:])
    wo_b = w_out(1, pw[1, S5_T:0:-1])

    def rows(t):
        n = t.shape[0]
        t = t.reshape(n, S5_NB, S5_NS).transpose(1, 0, 2)
        return jnp.stack([jnp.real(t), jnp.imag(t)], axis=1)

    weights = dict(m_toep=m_toep.astype(BF16), we_f=we_f.astype(BF16), we_b=we_b.astype(BF16),
                   wo_f=wo_f.astype(BF16), wo_b=wo_b.astype(BF16))

    def scan_tables(n_rows):
        n_lv = max(1, int(math.log2(n_rows)))
        lv_steps = S5_T * (2.0 ** jnp.arange(n_lv, dtype=F32))
        jj = jnp.arange(n_rows, dtype=F32)
        tabs = []
        for d in range(2):
            lv = jnp.exp(lv_steps[:, None, None] * ldt[d][None])
            expo = jj if d == 0 else (n_rows - 1 - jj)
            pt = jnp.exp((S5_T * expo)[:, None, None] * ldt[d][None])
            tabs.append((rows(lv), rows(pt)))
        return dict(weights, lv_f=tabs[0][0], pt_f=tabs[0][1], lv_b=tabs[1][0], pt_b=tabs[1][1])

    return [scan_tables(n) for n in n_rows_list]


def _s5_state_scan(e, lv_ref, pt_ref, h0, reverse):
    n_rows = e.shape[0]
    row = lax.broadcasted_iota(jnp.int32, (n_rows, S5_NS), 0)

    def shift(x, s):
        if reverse:
            return jnp.where(row < n_rows - s, pltpu.roll(x, n_rows - s, 0), 0.0)
        return jnp.where(row >= s, pltpu.roll(x, s, 0), 0.0)

    e_re, e_im = e[:, :S5_NS], e[:, S5_NS:]
    z_re, z_im = shift(e_re, 1), shift(e_im, 1)
    k = 0
    while (1 << k) < n_rows:
        a_re, a_im = lv_ref[0, 0, k:k + 1, :], lv_ref[0, 1, k:k + 1, :]
        s_re, s_im = shift(z_re, 1 << k), shift(z_im, 1 << k)
        z_re, z_im = z_re + (a_re * s_re - a_im * s_im), z_im + (a_re * s_im + a_im * s_re)
        k += 1
    p_re, p_im = pt_ref[0, 0], pt_ref[0, 1]
    h_re = z_re + (p_re * h0[0:1, :] - p_im * h0[1:2, :])
    h_im = z_im + (p_re * h0[1:2, :] + p_im * h0[0:1, :])
    last = 0 if reverse else n_rows - 1
    a_re, a_im = lv_ref[0, 0, 0:1, :], lv_ref[0, 1, 0:1, :]
    l_re, l_im = h_re[last:last + 1, :], h_im[last:last + 1, :]
    o_re = a_re * l_re - a_im * l_im + e_re[last:last + 1, :]
    o_im = a_re * l_im + a_im * l_re + e_im[last:last + 1, :]
    return jnp.concatenate([h_re, h_im], axis=1), jnp.concatenate([o_re, o_im], axis=0)


def _s5_kernel(*refs, n_pass_through):
    (uf_ref, ub_ref, mt_ref, wef_ref, web_ref, wof_ref, wob_ref, lvf_ref, ptf_ref, lvb_ref, ptb_ref,
     d_ref, h0f_ref, h0b_ref) = refs[:14]
    y1_ref, y2_ref, hff_ref, hfb_ref, sf_ref, sb_ref = refs[14 + n_pass_through:]
    i = pl.program_id(2)
    n_rows = uf_ref.shape[0] // S5_T

    @pl.when(i == 0)
    def _():
        sf_ref[...] = h0f_ref[...]
        sb_ref[...] = h0b_ref[...]

    def chunk_rows(ref):
        return jnp.concatenate([ref[pl.ds(t, n_rows, stride=S5_T), :] for t in range(S5_T)], axis=1)

    uf = chunk_rows(uf_ref)
    ub = chunk_rows(ub_ref)
    uf_bf = uf.astype(BF16)
    y_local = jnp.dot(uf_bf, mt_ref[0], preferred_element_type=F32)
    e_f = jnp.dot(uf_bf, wef_ref[0], preferred_element_type=F32)
    e_b = jnp.dot(ub.astype(BF16), web_ref[0], preferred_element_type=F32)
    h_f, out_f = _s5_state_scan(e_f, lvf_ref, ptf_ref, sf_ref[...], reverse=False)
    h_b, out_b = _s5_state_scan(e_b, lvb_ref, ptb_ref, sb_ref[...], reverse=True)
    sf_ref[...] = out_f
    sb_ref[...] = out_b
    y1 = y_local + jnp.dot(h_f.astype(BF16), wof_ref[0], preferred_element_type=F32)
    y2 = jnp.dot(h_b.astype(BF16), wob_ref[0], preferred_element_type=F32)
    d = d_ref[...]
    for t in range(S5_T):
        cols = slice(t * LANES, (t + 1) * LANES)
        y1_ref[pl.ds(t, n_rows, stride=S5_T), :] = y1[:, cols] + d * uf[:, cols]
        y2_ref[pl.ds(t, n_rows, stride=S5_T), :] = y2[:, cols]

    @pl.when(i == pl.num_programs(2) - 1)
    def _():
        hff_ref[...] = out_f
        hfb_ref[...] = out_b


def s5_scan(u, col0, row0, bsz, n_tiles, tile, tabs, d_skip, h0_f, h0_b, y_prev=None):
    n_rows = tile // S5_T
    assert tile % S5_T == 0 and tabs['pt_f'].shape[2] == n_rows
    n_lv = tabs['lv_f'].shape[2]
    u_spec_f = pl.BlockSpec((tile, LANES), lambda b, n_, i: (row0(n_) + i, col0 + b))
    u_spec_b = pl.BlockSpec((tile, LANES), lambda b, n_, i: (row0(n_) + n_tiles - 1 - i, col0 + b))
    y_spec_f = pl.BlockSpec((tile, LANES), lambda b, n_, i: (row0(n_) + i, b))
    y_spec_b = pl.BlockSpec((tile, LANES), lambda b, n_, i: (row0(n_) + n_tiles - 1 - i, b))
    w_spec = pl.BlockSpec((1, S5_CW, S5_CW), lambda b, n_, i: (b, 0, 0))
    lv_spec = pl.BlockSpec((1, 2, n_lv, S5_NS), lambda b, n_, i: (b, 0, 0, 0))
    pt_spec = pl.BlockSpec((1, 2, n_rows, S5_NS), lambda b, n_, i: (b, 0, 0, 0))
    st_spec = pl.BlockSpec((None, None, 2, S5_NS), lambda b, n_, i: (n_, b, 0, 0))
    in_specs = [u_spec_f, u_spec_b, w_spec, w_spec, w_spec, w_spec, w_spec, lv_spec, pt_spec, lv_spec, pt_spec,
                pl.BlockSpec((1, LANES), lambda b, n_, i: (0, b)), st_spec, st_spec]
    args = [u, u, tabs['m_toep'], tabs['we_f'], tabs['we_b'], tabs['wo_f'], tabs['wo_b'],
            tabs['lv_f'], tabs['pt_f'], tabs['lv_b'], tabs['pt_b'], d_skip.reshape(1, W_B).astype(F32), h0_f, h0_b]
    aliases = {}
    if y_prev is not None:
        aliases = {len(args): 0, len(args) + 1: 1}
        in_specs += [pl.BlockSpec(memory_space=pl.ANY), pl.BlockSpec(memory_space=pl.ANY)]
        args += list(y_prev)
    y_shape = jax.ShapeDtypeStruct((u.shape[0], W_B), F32)
    st_shape = jax.ShapeDtypeStruct((bsz, S5_NB, 2, S5_NS), F32)
    return pl.pallas_call(
        functools.partial(_s5_kernel, n_pass_through=len(aliases)),
        grid=(S5_NB, bsz, n_tiles),
        in_specs=in_specs,
        out_specs=[y_spec_f, y_spec_b, st_spec, st_spec],
        out_shape=[y_shape, y_shape, st_shape, st_shape],
        input_output_aliases=aliases,
        scratch_shapes=[pltpu.VMEM((2, S5_NS), F32), pltpu.VMEM((2, S5_NS), F32)],
        compiler_params=pltpu.CompilerParams(
            dimension_semantics=("arbitrary", "arbitrary", "arbitrary"),
            vmem_limit_bytes=V7X_VMEM_LIMIT_BYTES),
    )(*args)


def s5_mixer(u, col0, lay, lam_re, lam_im, log_dt, b_re, b_im, c_re, c_im, d_skip, tile_l=1024):
    bsz, n_lat, n_ctx = lay
    tile_l = min(tile_l, n_lat)
    assert n_lat % tile_l == 0 and (bsz * n_lat) % n_ctx == 0
    prm = (lam_re, lam_im, log_dt, b_re, b_im, c_re, c_im)
    tabs_c, tabs_l = _s5_tables(*prm, n_rows_list=(n_ctx // S5_T, tile_l // S5_T))
    zero = jnp.zeros((bsz, S5_NB, 2, S5_NS), F32)
    y1, y2, hf, hb = s5_scan(u, col0, lambda n_: _ctx_row(lay, n_) // n_ctx, bsz, 1, n_ctx, tabs_c, d_skip, zero, zero)
    y1, y2, _, _ = s5_scan(u, col0, lambda n_: _lat_row(lay, n_) // tile_l, bsz, n_lat // tile_l, tile_l, tabs_l,
                           d_skip, hf, hb, y_prev=(y1, y2))
    return y1, y2


HPG_C = H_C // NG_C
GW_C = HPG_C * HD_C


def _softplus(x):
    return jnp.maximum(x, 0.0) + jnp.log1p(jnp.exp(-jnp.abs(x)))


COLS_PER_STEP = 8


def _dwconv_silu_vals(prev, cur, nxt, w_ref, b_ref, xp_ref):
    tile = cur.shape[0]
    xp_ref[0:HALO, :] = prev
    xp_ref[HALO:HALO + tile, :] = cur
    xp_ref[HALO + tile:, :] = nxt
    acc = b_ref[...] + w_ref[0:1, :] * xp_ref[pl.ds(HALO - CONV_W // 2, tile), :]
    for j in range(1, CONV_W):
        acc = acc + w_ref[j:j + 1, :] * xp_ref[pl.ds(HALO - CONV_W // 2 + j, tile), :]
    return acc * jax.nn.sigmoid(acc)


def _ssd_chunk(xbc_ref, dt_rows_raw, dt_cols_raw, pc_ref, pr_ref, s_scr, reverse):
    t_len = xbc_ref.shape[0]
    dt_rows = _softplus(dt_rows_raw + pc_ref[:, 1:2])
    cs_rows = _cumsum_incl(-jnp.exp(pc_ref[:, 0:1]) * dt_rows, 1, reverse)
    dt_cols = _softplus(dt_cols_raw + pr_ref[1:2, :])
    cs_cols = _cumsum_incl(-jnp.exp(pr_ref[0:1, :]) * dt_cols, 0, reverse)
    end = 0 if reverse else t_len - 1
    cs_end = cs_cols[end:end + 1, :]
    row_id = lax.broadcasted_iota(jnp.int32, (t_len, t_len), 0)
    col_id = lax.broadcasted_iota(jnp.int32, (t_len, t_len), 1)
    allowed = (col_id >= row_id) if reverse else (col_id <= row_id)
    out = []
    for g in range(NG_C):
        bg = xbc_ref[:, W_C + g * N_C:W_C + (g + 1) * N_C].astype(BF16)
        cg = xbc_ref[:, W_C + NG_C * N_C + g * N_C:W_C + NG_C * N_C + (g + 1) * N_C].astype(BF16)
        cb = lax.dot_general(cg, bg, (((1,), (1,)), ((), ())), preferred_element_type=F32)
        s_old = s_scr[g]
        y_off = jnp.dot(cg, s_old.astype(BF16), preferred_element_type=F32)
        ys, xds, chunk_decay = [], [], []
        for hh in range(HPG_C):
            h = g * HPG_C + hh
            cs_c, cs_r = cs_cols[:, h:h + 1], cs_rows[h:h + 1, :]
            seg = jnp.exp(jnp.where(allowed, cs_c - cs_r, -jnp.inf))
            xv = xbc_ref[:, h * HD_C:(h + 1) * HD_C] * dt_cols[:, h:h + 1]
            y_diag = jnp.dot((cb * seg).astype(BF16), xv.astype(BF16), preferred_element_type=F32)
            ys.append(y_diag + y_off[:, hh * HD_C:(hh + 1) * HD_C] * jnp.exp(cs_c))
            xds.append(xv * jnp.exp(cs_end[:, h:h + 1] - cs_c))
            chunk_decay.append(jnp.broadcast_to(jnp.exp(cs_end[:, h:h + 1]), (1, HD_C)))
        xd = jnp.concatenate(xds, axis=1).astype(BF16)
        s_loc = lax.dot_general(bg, xd, (((0,), (0,)), ((), ())), preferred_element_type=F32)
        s_scr[g] = s_old * jnp.concatenate(chunk_decay, axis=1) + s_loc
        out.append(jnp.concatenate(ys, axis=1))
    return out


def _ssd_ctx_kernel(*refs, reverse, combine):
    if combine:
        (prev_ref, raw_ref, next_ref, cw_ref, cb_ref, dtr_ref, dtc_ref, pc_ref, pr_ref, yo_ref, dsk_ref,
         o_ref, sfin_ref, s_scr, xp_ref, xbc_ref) = refs
    else:
        (prev_ref, raw_ref, next_ref, cw_ref, cb_ref, dtr_ref, dtc_ref, pc_ref, pr_ref,
         o_ref, sfin_ref, s_scr, xp_ref, xbc_ref) = refs
    s, n = pl.program_id(1), pl.num_programs(1)

    @pl.when(s == 0)
    def _():
        s_scr[...] = jnp.zeros_like(s_scr)

    j = n - 1 - s if reverse else s
    prev = jnp.where(j == 0, 0.0, prev_ref[...])
    nxt = jnp.where(j == n - 1, 0.0, next_ref[...])
    xbc_ref[...] = _dwconv_silu_vals(prev, raw_ref[...], nxt, cw_ref, cb_ref, xp_ref)
    ys = _ssd_chunk(xbc_ref, dtr_ref[...], dtc_ref[...], pc_ref, pr_ref, s_scr, reverse)
    for g in range(NG_C):
        sl = slice(g * GW_C, (g + 1) * GW_C)
        y = ys[g]
        if combine:
            y = y + yo_ref[:, sl] + xbc_ref[:, sl] * dsk_ref[:, sl]
        o_ref[:, sl] = y

    @pl.when(s == n - 1)
    def _():
        sfin_ref[...] = s_scr[...]


def _ssd_lat_kernel(*refs, reverse, combine, n_pass_through):
    (pb_ref, x_ref, nb_ref, cw_ref, cb_ref, dtr_ref, dtc_ref, pc_ref, pr_ref, s0_ref) = refs[:10]
    rest = refs[10:]
    if combine:
        yo_ref, dsk_ref = rest[:2]
        rest = rest[2:]
    o_ref, s_scr, xp_ref, xbc_ref = rest[n_pass_through:]
    s, n = pl.program_id(1), pl.num_programs(1)
    t_len = x_ref.shape[0]

    @pl.when(s == 0)
    def _():
        s_scr[...] = s0_ref[...]

    blk = n - 1 - s if reverse else s
    for kk in range(COLS_PER_STEP):
        k = COLS_PER_STEP - 1 - kk if reverse else kk
        if k > 0:
            prev = x_ref[t_len - HALO:t_len, k - 1, :]
        else:
            prev = jnp.where(blk == 0, 0.0, pb_ref[:, COLS_PER_STEP - 1, :])
        if k < COLS_PER_STEP - 1:
            nxt = x_ref[0:HALO, k + 1, :]
        else:
            nxt = jnp.where(blk == n - 1, 0.0, nb_ref[:, 0, :])
        xbc_ref[...] = _dwconv_silu_vals(prev, x_ref[:, k, :], nxt, cw_ref, cb_ref, xp_ref)
        ys = _ssd_chunk(xbc_ref, dtr_ref[:, k * t_len:(k + 1) * t_len], dtc_ref[k * t_len:(k + 1) * t_len, :],
                        pc_ref, pr_ref, s_scr, reverse)
        for g in range(NG_C):
            sl = slice(g * GW_C, (g + 1) * GW_C)
            y = ys[g]
            if combine:
                y = y + yo_ref[:, k, sl] + xbc_ref[:, sl] * dsk_ref[:, sl]
            o_ref[:, k, sl] = y


def _ssd_params(conv_w, conv_b, a_log, dt_bias):
    return (jnp.pad(conv_w.astype(F32), ((0, HALO - CONV_W), (0, 0))), conv_b.reshape(1, XBC_W).astype(F32),
            jnp.stack([a_log, dt_bias], axis=1).astype(F32), jnp.stack([a_log, dt_bias], axis=0).astype(F32))


_SSD_PARAM_SPECS = [((HALO, XBC_W)), ((1, XBC_W)), ((H_C, 2)), ((2, H_C))]


def ssd_scan_ctx(p_all, lay, direction, dt_rows, dt_cols, prm, combine_args=None):
    bsz, n_lat, n_ctx = lay
    reverse = direction == 1
    t_len = SCAN_CHUNK
    n_chunks = n_ctx // t_len
    hb = t_len // HALO
    chunk = (lambda s: n_chunks - 1 - s) if reverse else (lambda s: s)
    row_blk = lambda b, s: _ctx_row(lay, b) // t_len + chunk(s)
    n_halo = p_all.shape[0] // HALO
    y_spec = pl.BlockSpec((t_len, W_C), lambda b, s: (row_blk(b, s), 0))
    in_specs = [pl.BlockSpec((HALO, XBC_W), lambda b, s: (jnp.maximum(row_blk(b, s) * hb - 1, 0), P_XBC // XBC_W)),
                pl.BlockSpec((t_len, XBC_W), lambda b, s: (row_blk(b, s), P_XBC // XBC_W)),
                pl.BlockSpec((HALO, XBC_W), lambda b, s: (jnp.minimum((row_blk(b, s) + 1) * hb, n_halo - 1),
                                                          P_XBC // XBC_W))]
    in_specs += [pl.BlockSpec(shape, lambda b, s: (0, 0)) for shape in _SSD_PARAM_SPECS]
    in_specs[5:5] = [pl.BlockSpec((None, None, H_C, t_len), lambda b, s: (b, direction, 0, chunk(s))),
                     pl.BlockSpec((None, None, t_len, H_C), lambda b, s: (b, direction, chunk(s), 0))]
    args = [p_all, p_all, p_all, prm[0], prm[1], dt_rows, dt_cols, prm[2], prm[3]]
    if combine_args is not None:
        y_other, d_skip = combine_args
        in_specs += [y_spec, pl.BlockSpec((1, W_C), lambda b, s: (0, 0))]
        args += [y_other, d_skip]
    st_spec = pl.BlockSpec((None, NG_C, N_C, GW_C), lambda b, s: (b, 0, 0, 0))
    return pl.pallas_call(
        functools.partial(_ssd_ctx_kernel, reverse=reverse, combine=combine_args is not None),
        grid=(bsz, n_chunks),
        in_specs=in_specs,
        out_specs=[y_spec, st_spec],
        out_shape=[jax.ShapeDtypeStruct((p_all.shape[0], W_C), F32),
                   jax.ShapeDtypeStruct((bsz, NG_C, N_C, GW_C), F32)],
        scratch_shapes=[pltpu.VMEM((NG_C, N_C, GW_C), F32), pltpu.VMEM((t_len + 2 * HALO, XBC_W), F32),
                        pltpu.VMEM((t_len, XBC_W), F32)],
        compiler_params=pltpu.CompilerParams(dimension_semantics=("arbitrary", "arbitrary"),
                                             vmem_limit_bytes=V7X_VMEM_LIMIT_BYTES),
    )(*args)


def ssd_scan_lat(p_all, lay, direction, dt_rows, dt_cols, prm, state0, y_prev, combine_args=None):
    bsz, n_lat, n_ctx = lay
    reverse = direction == 1
    t_len = n_lat // GRID_W
    n_blocks = GRID_W // COLS_PER_STEP
    assert t_len % HALO == 0 and GRID_W == COLS_PER_STEP * n_blocks
    groups = p_all.shape[0] // GRID_W
    view = lambda t: t.reshape(groups, n_blocks, COLS_PER_STEP, t.shape[1])
    blk = (lambda s: n_blocks - 1 - s) if reverse else (lambda s: s)
    hb = t_len // HALO
    y_spec = pl.BlockSpec((t_len, None, COLS_PER_STEP, W_C), lambda b, s: (b, blk(s), 0, 0))
    xcol = P_XBC // XBC_W
    in_specs = [pl.BlockSpec((HALO, None, COLS_PER_STEP, XBC_W),
                             lambda b, s: ((b + 1) * hb - 1, jnp.maximum(blk(s) - 1, 0), 0, xcol)),
                pl.BlockSpec((t_len, None, COLS_PER_STEP, XBC_W), lambda b, s: (b, blk(s), 0, xcol)),
                pl.BlockSpec((HALO, None, COLS_PER_STEP, XBC_W),
                             lambda b, s: (b * hb, jnp.minimum(blk(s) + 1, n_blocks - 1), 0, xcol))]
    in_specs += [pl.BlockSpec(shape, lambda b, s: (0, 0)) for shape in _SSD_PARAM_SPECS]
    step = COLS_PER_STEP * t_len
    in_specs[5:5] = [pl.BlockSpec((None, None, H_C, step), lambda b, s: (b, direction, 0, blk(s))),
                     pl.BlockSpec((None, None, step, H_C), lambda b, s: (b, direction, blk(s), 0))]
    st_spec = pl.BlockSpec((None, NG_C, N_C, GW_C), lambda b, s: (b, 0, 0, 0))
    in_specs.append(st_spec)
    p4 = view(p_all)
    args = [p4, p4, p4, prm[0], prm[1], dt_rows, dt_cols, prm[2], prm[3], state0]
    if combine_args is not None:
        y_other, d_skip = combine_args
        in_specs += [y_spec, pl.BlockSpec((1, W_C), lambda b, s: (0, 0))]
        args += [view(y_other), d_skip]
    aliases = {len(args): 0}
    in_specs.append(pl.BlockSpec(memory_space=pl.ANY))
    args.append(view(y_prev))
    y4 = pl.pallas_call(
        functools.partial(_ssd_lat_kernel, reverse=reverse, combine=combine_args is not None, n_pass_through=1),
        grid=(bsz, n_blocks),
        in_specs=in_specs,
        out_specs=y_spec,
        out_shape=jax.ShapeDtypeStruct((groups, n_blocks, COLS_PER_STEP, W_C), F32),
        input_output_aliases=aliases,
        scratch_shapes=[pltpu.VMEM((NG_C, N_C, GW_C), F32), pltpu.VMEM((t_len + 2 * HALO, XBC_W), F32),
                        pltpu.VMEM((t_len, XBC_W), F32)],
        compiler_params=pltpu.CompilerParams(dimension_semantics=("arbitrary", "arbitrary"),
                                             vmem_limit_bytes=V7X_VMEM_LIMIT_BYTES),
    )(*args)
    return y4.reshape(p_all.shape[0], W_C)


def ssd_mixer_inplace(p_all, lay, conv_w, conv_b, a_log, dt_bias, d_skip):
    bsz, n_lat, n_ctx = lay
    grid_rows = n_lat // GRID_W

    def dt_layouts(t, n, colmajor):
        t = t[:, P_DT:P_DT + 2 * H_C].reshape(bsz, n, 2, H_C)
        if colmajor:
            t = t.reshape(bsz, grid_rows, GRID_W, 2, H_C).swapaxes(1, 2).reshape(bsz, n, 2, H_C)
        return t.transpose(0, 2, 3, 1), t.transpose(0, 2, 1, 3)

    dtr_c, dtc_c = dt_layouts(p_all[bsz * n_lat:], n_ctx, False)
    dtr_l, dtc_l = dt_layouts(p_all[:bsz * n_lat], n_lat, True)
    d_row = jnp.repeat(d_skip.astype(F32), HD_C).reshape(1, W_C)
    y = {}
    for direction in (1, 0):
        prm = _ssd_params(conv_w, conv_b, a_log[direction], dt_bias[direction])
        combine = None if direction == 1 else (y[1], d_row)
        y_ctx, state = ssd_scan_ctx(p_all, lay, direction, dtr_c, dtc_c, prm, combine)
        y[direction] = ssd_scan_lat(p_all, lay, direction, dtr_l, dtc_l, prm, state, y_ctx, combine)
    return y[0]


TRUNK_TM = 512
P_COLS = 7168
P_XM, P_ZM, P_U, P_ZS, P_XBC, P_GD, P_DT = 0, W_A, 2 * W_A, 2 * W_A + W_B, 2 * W_A + W_B + W_C, 6144, 6656


def _lat_row(lay, b):
    bsz, n_lat, n_ctx = lay
    return b * n_lat


def _ctx_row(lay, b):
    bsz, n_lat, n_ctx = lay
    return bsz * n_lat + b * n_ctx


def _seq_block(lay, b, j, t_len):
    bsz, n_lat, n_ctx = lay
    n_cb = n_ctx // t_len
    return jnp.where(j < n_cb, (bsz * n_lat + b * n_ctx) // t_len + j, (b * n_lat) // t_len + (j - n_cb))


def _mod_index(lay, tm):
    bsz, n_lat, n_ctx = lay
    assert n_lat % tm == 0 and (bsz * n_ctx) % tm == 0
    return lambda i: jnp.where(i < (bsz * n_lat) // tm, 1 + i // (n_lat // tm), 0)


def _permute_w_in(w):
    xm_zm_u_zs_xbc, dt, gd = w[:, :SPLIT_IDX[4]], w[:, SPLIT_IDX[4]:SPLIT_IDX[5]], w[:, SPLIT_IDX[5]:]
    pad = jnp.zeros((w.shape[0], P_COLS - IN_COLS), w.dtype)
    return jnp.concatenate([xm_zm_u_zs_xbc, gd, dt, pad], axis=1).astype(BF16)


def _modulated_norm(x, g, shift, scale):
    xn = x * lax.rsqrt(jnp.mean(x * x, axis=-1, keepdims=True) + EPS) * g
    return xn * (1.0 + scale) + shift


def _win_kernel(x_ref, g_ref, mod_ref, w_ref, o_ref, h_ref):
    @pl.when(pl.program_id(1) == 0)
    def _():
        h_ref[...] = _modulated_norm(x_ref[...], g_ref[...], mod_ref[0, 0:1, :], mod_ref[0, 1:2, :]).astype(BF16)

    o_ref[...] = jnp.dot(h_ref[...], w_ref[...], preferred_element_type=F32)


def in_projection(x, g, mods, w_p, lay, tn=1024):
    m, d = x.shape
    tm = TRUNK_TM
    mod_idx = _mod_index(lay, tm)
    single = pl.Buffered(1)
    return pl.pallas_call(
        _win_kernel,
        grid=(m // tm, P_COLS // tn),
        in_specs=[pl.BlockSpec((tm, d), lambda i, j: (i, 0)),
                  pl.BlockSpec((1, d), lambda i, j: (0, 0), pipeline_mode=single),
                  pl.BlockSpec((1, MOD_ROWS, d), lambda i, j: (mod_idx(i), 0, 0), pipeline_mode=single),
                  pl.BlockSpec((d, tn), lambda i, j: (0, j))],
        out_specs=pl.BlockSpec((tm, tn), lambda i, j: (i, j)),
        out_shape=jax.ShapeDtypeStruct((m, P_COLS), F32),
        scratch_shapes=[pltpu.VMEM((tm, d), BF16)],
        compiler_params=pltpu.CompilerParams(dimension_semantics=("arbitrary", "arbitrary"),
                                             vmem_limit_bytes=V7X_VMEM_LIMIT_BYTES),
    )(x, g.reshape(1, d).astype(F32), mods, w_p)


def _merge_kernel(gd_ref, ya_ref, y1_ref, y2_ref, yc_ref, zs_ref, ng_ref, gluw_ref, glub_ref, gw_ref, gb_ref, bw_ref,
                  o_ref, gds_ref, ys_ref):
    @pl.when(pl.program_id(1) == 0)
    def _():
        gds_ref[...] = gd_ref[...].astype(BF16)
        ys_ref[0] = ya_ref[...].astype(BF16)
        yb = jax.nn.gelu(y1_ref[...] + y2_ref[...])
        gate = jnp.dot(yb.astype(BF16), gluw_ref[...], preferred_element_type=F32) + glub_ref[...]
        ys_ref[1] = (yb * jax.nn.sigmoid(gate)).astype(BF16)
        zs = zs_ref[...]
        yg = yc_ref[...] * (zs * jax.nn.sigmoid(zs))
        yg = yg * lax.rsqrt(jnp.mean(yg * yg, axis=-1, keepdims=True) + EPS) * ng_ref[...]
        ys_ref[2] = yg.astype(BF16)

    acc = None
    for i in range(N_BRANCH):
        gate = jax.nn.sigmoid(jnp.dot(gds_ref[...], gw_ref[i], preferred_element_type=F32) + gb_ref[i])
        term = gate * jnp.dot(ys_ref[i], bw_ref[i], preferred_element_type=F32)
        acc = term if acc is None else acc + term
    o_ref[...] = acc.astype(o_ref.dtype)


def merge_projection(p_all, ya, y1, y2, yc, ssd_norm_g, glu_w, glu_b, gate_w, gate_b, branch_w, tn=1024):
    m = p_all.shape[0]
    d = gate_w.shape[-1]
    tm = TRUNK_TM
    single = pl.Buffered(1)
    row = lambda w: pl.BlockSpec((tm, w), lambda i, j: (i, 0))
    return pl.pallas_call(
        _merge_kernel,
        grid=(m // tm, d // tn),
        in_specs=[pl.BlockSpec((tm, R_GATE), lambda i, j: (i, P_GD // R_GATE)),
                  row(W_A), row(W_B), row(W_B), row(W_C),
                  pl.BlockSpec((tm, W_C), lambda i, j: (i, P_ZS // W_C)),
                  pl.BlockSpec((1, W_C), lambda i, j: (0, 0), pipeline_mode=single),
                  pl.BlockSpec((W_B, W_B), lambda i, j: (0, 0), pipeline_mode=single),
                  pl.BlockSpec((1, W_B), lambda i, j: (0, 0), pipeline_mode=single),
                  pl.BlockSpec((N_BRANCH, R_GATE, tn), lambda i, j: (0, 0, j)),
                  pl.BlockSpec((N_BRANCH, 1, tn), lambda i, j: (0, 0, j)),
                  pl.BlockSpec((N_BRANCH, W_MIX, tn), lambda i, j: (0, 0, j))],
        out_specs=pl.BlockSpec((tm, tn), lambda i, j: (i, j)),
        out_shape=jax.ShapeDtypeStruct((m, d), BF16),
        scratch_shapes=[pltpu.VMEM((tm, R_GATE), BF16), pltpu.VMEM((N_BRANCH, tm, W_MIX), BF16)],
        compiler_params=pltpu.CompilerParams(dimension_semantics=("arbitrary", "arbitrary"),
                                             vmem_limit_bytes=V7X_VMEM_LIMIT_BYTES),
    )(p_all, ya, y1, y2, yc, p_all, ssd_norm_g.reshape(1, W_C).astype(F32), glu_w.astype(BF16),
      glu_b.reshape(1, W_B).astype(F32),
      gate_w.astype(BF16), gate_b.reshape(N_BRANCH, 1, d).astype(F32), branch_w.astype(BF16))


def _wout_kernel(t_ref, w_ref, x_ref, mod_ref, o_ref):
    y = jnp.dot(t_ref[...], w_ref[...], preferred_element_type=F32)
    o_ref[...] = x_ref[...] + mod_ref[0, 2:3, :] * y


def out_projection(t, w_out_bf, x, mods, lay, tn=1024):
    m, d = x.shape
    tm = TRUNK_TM
    mod_idx = _mod_index(lay, tm)
    return pl.pallas_call(
        _wout_kernel,
        grid=(m // tm, d // tn),
        in_specs=[pl.BlockSpec((tm, d), lambda i, j: (i, 0)),
                  pl.BlockSpec((d, tn), lambda i, j: (0, j)),
                  pl.BlockSpec((tm, tn), lambda i, j: (i, j)),
                  pl.BlockSpec((1, MOD_ROWS, tn), lambda i, j: (mod_idx(i), 0, j))],
        out_specs=pl.BlockSpec((tm, tn), lambda i, j: (i, j)),
        out_shape=jax.ShapeDtypeStruct((m, d), F32),
        compiler_params=pltpu.CompilerParams(dimension_semantics=("arbitrary", "arbitrary"),
                                             vmem_limit_bytes=V7X_VMEM_LIMIT_BYTES),
    )(t, w_out_bf, x, mods)


def _mlp_kernel(x_ref, g_ref, mod_ref, w1_ref, w2_ref, fg_ref, o_ref, h_ref, *, final_norm):
    f = pl.program_id(1)

    @pl.when(f == 0)
    def _():
        x = x_ref[...]
        xn = x * lax.rsqrt(jnp.mean(x * x, axis=-1, keepdims=True) + EPS) * g_ref[...]
        h = xn * (1.0 + mod_ref[0, 4:5, :]) + mod_ref[0, 3:4, :]
        h_ref[...] = h.astype(BF16)
        o_ref[...] = jnp.zeros_like(o_ref)

    a = jnp.dot(h_ref[...], w1_ref[...], preferred_element_type=F32)
    a = jnp.square(jnp.maximum(a, 0.0)).astype(BF16)
    o_ref[...] += jnp.dot(a, w2_ref[...], preferred_element_type=F32)

    @pl.when(f == pl.num_programs(1) - 1)
    def _():
        y = x_ref[...] + mod_ref[0, 5:6, :] * o_ref[...]
        if final_norm:
            y = y * lax.rsqrt(jnp.mean(y * y, axis=-1, keepdims=True) + EPS) * fg_ref[...]
        o_ref[...] = y


def mlp_block(x, g, mod, w1, w2, lay, final_g, final_norm, tf=512):
    m, d = x.shape
    dff = w1.shape[1]
    tm = TRUNK_TM
    assert m % tm == 0 and dff % tf == 0
    mod_idx = _mod_index(lay, tm)
    single = pl.Buffered(1)
    return pl.pallas_call(
        functools.partial(_mlp_kernel, final_norm=final_norm),
        grid=(m // tm, dff // tf),
        in_specs=[
            pl.BlockSpec((tm, d), lambda i, f: (i, 0), pipeline_mode=single),
            pl.BlockSpec((1, d), lambda i, f: (0, 0), pipeline_mode=single),
            pl.BlockSpec((1, MOD_ROWS, d), lambda i, f: (mod_idx(i), 0, 0), pipeline_mode=single),
            pl.BlockSpec((d, tf), lambda i, f: (0, f)),
            pl.BlockSpec((tf, d), lambda i, f: (f, 0)),
            pl.BlockSpec((1, d), lambda i, f: (0, 0), pipeline_mode=single),
        ],
        out_specs=pl.BlockSpec((tm, d), lambda i, f: (i, 0), pipeline_mode=single),
        out_shape=jax.ShapeDtypeStruct((m, d), F32),
        scratch_shapes=[pltpu.VMEM((tm, d), BF16)],
        compiler_params=pltpu.CompilerParams(
            dimension_semantics=("arbitrary", "arbitrary"),
            vmem_limit_bytes=V7X_VMEM_LIMIT_BYTES),
    )(x, g.reshape(1, d), mod, w1, w2, final_g.reshape(1, d).astype(F32))


def _pad_mod(mod):
    nb = mod.shape[0]
    mod = mod.reshape(nb, N_MOD, D_MODEL)
    return jnp.pad(mod, ((0, 0), (0, MOD_ROWS - N_MOD), (0, 0)))


def kernel(x, c, ctx, c_ctx, norm1_g, norm2_g, final_g, mod_down, mod_up, mod_b, w_in,
           m_conv_w, m_conv_b, m_wq, m_wk, m_wv, m_gate_w, m_gate_b, m_norm_g, m_skip,
           s5_lam_re, s5_lam_im, s5_log_dt, s5_b_re, s5_b_im, s5_c_re, s5_c_im, s5_d, s5_glu_w, s5_glu_b,
           ssd_conv_w, ssd_conv_b, ssd_a_log, ssd_dt_bias, ssd_d, ssd_norm_g,
           gate_w, gate_b, branch_w, w_out, mlp_w1, mlp_w2):
    bsz, n_lat, d = x.shape
    n_ctx = ctx.shape[1]
    lay = (bsz, n_lat, n_ctx)
    xs = jnp.concatenate([x.reshape(bsz * n_lat, d), ctx.reshape(bsz * n_ctx, d)], axis=0)
    for l in range(DEPTH):
        mods = _pad_mod(adaln(jnp.concatenate([c_ctx[None], c], axis=0), mod_down[l], mod_up[l], mod_b[l]))
        p_all = in_projection(xs, norm1_g[l], mods, _permute_w_in(w_in[l]), lay)
        ya = mlstm_mixer(p_all, lay, m_conv_w[l], m_conv_b[l], m_wq[l], m_wk[l], m_wv[l],
                         m_gate_w[l], m_gate_b[l], m_norm_g[l], m_skip[l])
        yb1, yb2 = s5_mixer(p_all, P_U // LANES, lay, s5_lam_re[l], s5_lam_im[l], s5_log_dt[l], s5_b_re[l], s5_b_im[l],
                            s5_c_re[l], s5_c_im[l], s5_d[l])
        yc = ssd_mixer_inplace(p_all, lay, ssd_conv_w[l], ssd_conv_b[l], ssd_a_log[l], ssd_dt_bias[l], ssd_d[l])
        t = merge_projection(p_all, ya, yb1, yb2, yc, ssd_norm_g[l], s5_glu_w[l], s5_glu_b[l], gate_w[l], gate_b[l],
                             branch_w[l])
        xs = out_projection(t, w_out[l].astype(BF16), xs, mods, lay)
        xs = mlp_block(xs, norm2_g[l], mods, mlp_w1[l].astype(BF16), mlp_w2[l].astype(BF16), lay,
                       final_g, final_norm=(l == DEPTH - 1))
    return xs[:bsz * n_lat].reshape(bsz, n_lat, d)
```
